```python
import jax, jax.numpy as jnp
from jax import lax
import numpy as np

D_MODEL = 1024
BATCH = 8
SEQ = 4096
DEPTH = 2

N_MEM = 256
N_BRANCH = 4
GROUP_W = D_MODEL // N_BRANCH
N_SUB = 4
HEAD_DIM = GROUP_W // N_SUB
CHUNK = 128
GRID_W = 64
WIN_H_MAX = 8
WIN_W = 16
N_IN_SLOTS = 11
IN_COLS = N_IN_SLOTS * GROUP_W
EPS = 1e-6

kernel_name = "hybrid_parallel_gmlp_natten_fnet_memattn"


def rmsnorm(x, g):
    xf = x.astype(jnp.float32)
    y = xf * lax.rsqrt(jnp.mean(xf * xf, axis=-1, keepdims=True) + EPS)
    return (y * g.astype(jnp.float32)).astype(x.dtype)


def layernorm(x, g, b):
    xf = x.astype(jnp.float32)
    mu = jnp.mean(xf, axis=-1, keepdims=True)
    var = jnp.mean(jnp.square(xf - mu), axis=-1, keepdims=True)
    y = (xf - mu) * lax.rsqrt(var + EPS)
    return (y * g.astype(jnp.float32) + b.astype(jnp.float32)).astype(x.dtype)


def split_heads(x):
    return x.reshape(x.shape[:-1] + (N_SUB, HEAD_DIM))


def chunked_spatial_gating(u, v, ln_g, ln_b, w_s, b_s):
    B, S, _ = u.shape
    n_chunks = S // CHUNK
    vn = layernorm(v, ln_g, ln_b).reshape(B, n_chunks, CHUNK, N_SUB, HEAD_DIM)
    s = jnp.einsum('hpq,bnqhd->bnphd', w_s, vn) + b_s.T[None, None, :, :, None]
    return u * s.reshape(B, S, GROUP_W)


def neighbourhood_attention(q, k, v, qn_g, kn_g, rpb):
    B, S, _ = q.shape
    rows = S // GRID_W
    kh = min(WIN_H_MAX, rows)
    q = rmsnorm(split_heads(q), qn_g) * (HEAD_DIM ** -0.5)
    k = rmsnorm(split_heads(k), kn_g)
    v = split_heads(v)
    q = q.reshape(B, rows, GRID_W, N_SUB, HEAD_DIM)
    k = k.reshape(B, rows, GRID_W, N_SUB, HEAD_DIM)
    v = v.reshape(B, rows, GRID_W, N_SUB, HEAD_DIM)
    r = jnp.arange(rows)
    row_start = jnp.clip(r - kh // 2, 0, rows - kh)
    row_idx = row_start[:, None] + jnp.arange(kh)
    kg = k[:, row_idx]
    vg = v[:, row_idx]
    scores = jnp.einsum('brchd,brkwhd->bhrckw', q, kg).astype(jnp.float32)
    c = jnp.arange(GRID_W)
    col_start = jnp.clip(c - WIN_W // 2, 0, GRID_W - WIN_W)
    col_ok = (c[None, :] >= col_start[:, None]) & (c[None, :] < col_start[:, None] + WIN_W)
    d_row = row_idx - r[:, None] + (WIN_H_MAX - 1)
    d_col = jnp.clip(c[None, :] - c[:, None] + (WIN_W - 1), 0, 2 * WIN_W - 2)
    bias = rpb[:, d_row[:, None, :, None], d_col[None, :, None, :]]
    scores = scores + bias[None].astype(jnp.float32)
    scores = jnp.where(col_ok[None, None, None, :, None, :], scores, -jnp.inf)
    shp = scores.shape
    p = jax.nn.softmax(scores.reshape(shp[:4] + (kh * GRID_W,)), axis=-1).reshape(shp).astype(v.dtype)
    out = jnp.einsum('bhrckw,brkwhd->brchd', p, vg)
    return out.reshape(B, S, GROUP_W)


def fourier_mixing(f, w_f):
    B, S, _ = f.shape
    fg = f.astype(jnp.float32).reshape(B, S, N_SUB, HEAD_DIM)
    z = jnp.fft.fft2(fg, axes=(1, 3), norm='ortho').real.astype(f.dtype)
    return jnp.einsum('bshd,hde->bshe', z, w_f).reshape(B, S, GROUP_W)


def memory_attention(q, mem_n, w_mkv, qn_g, kn_g):
    B, S, _ = q.shape
    km, vm = jnp.split(mem_n @ w_mkv, 2, axis=-1)
    q = rmsnorm(split_heads(q), qn_g) * (HEAD_DIM ** -0.5)
    km = rmsnorm(split_heads(km), kn_g)
    vm = split_heads(vm)
    s = jnp.einsum('bshd,bmhd->bhsm', q, km).astype(jnp.float32)
    p = jax.nn.softmax(s, axis=-1).astype(vm.dtype)
    return jnp.einsum('bhsm,bmhd->bshd', p, vm).reshape(B, S, GROUP_W)


def setup_inputs(seed: int = 0) -> dict:
    key = jax.random.key(seed)
    ks = jax.random.split(key, 20)
    nrm = lambda k, shp: jax.random.normal(k, shp, jnp.float32)
    return {
        "x": nrm(ks[0], (BATCH, SEQ, D_MODEL)),
        "mem": nrm(ks[1], (BATCH, N_MEM, D_MODEL)),
        "norm_g": 1.0 + 0.02 * nrm(ks[2], (DEPTH, D_MODEL)),
        "w_in": nrm(ks[3], (DEPTH, D_MODEL, IN_COLS)) * D_MODEL ** -0.5,
        "w_out": nrm(ks[4], (DEPTH, D_MODEL, D_MODEL)) * D_MODEL ** -0.5,
        "gm_ln_g": 1.0 + 0.02 * nrm(ks[5], (DEPTH, GROUP_W)),
        "gm_ln_b": 0.02 * nrm(ks[6], (DEPTH, GROUP_W)),
        "gm_w_s": nrm(ks[7], (DEPTH, N_SUB, CHUNK, CHUNK)) * 0.5 * CHUNK ** -0.5,
        "gm_b_s": 1.0 + 0.02 * nrm(ks[8], (DEPTH, N_SUB, CHUNK)),
        "na_qn_g": 1.0 + 0.02 * nrm(ks[9], (DEPTH, HEAD_DIM)),
        "na_kn_g": 1.0 + 0.02 * nrm(ks[10], (DEPTH, HEAD_DIM)),
        "na_rpb": 0.1 * nrm(ks[11], (DEPTH, N_SUB, 2 * WIN_H_MAX - 1, 2 * WIN_W - 1)),
        "fn_w": nrm(ks[12], (DEPTH, N_SUB, HEAD_DIM, HEAD_DIM)) * HEAD_DIM ** -0.5,
        "mem_norm_g": 1.0 + 0.02 * nrm(ks[13], (DEPTH, D_MODEL)),
        "mem_w_kv": nrm(ks[14], (DEPTH, D_MODEL, 2 * GROUP_W)) * D_MODEL ** -0.5,
        "mem_qn_g": 1.0 + 0.02 * nrm(ks[15], (DEPTH, HEAD_DIM)),
        "mem_kn_g": 1.0 + 0.02 * nrm(ks[16], (DEPTH, HEAD_DIM)),
    }


def reference(x, mem, norm_g, w_in, w_out, gm_ln_g, gm_ln_b, gm_w_s, gm_b_s,
              na_qn_g, na_kn_g, na_rpb, fn_w, mem_norm_g, mem_w_kv, mem_qn_g, mem_kn_g):
    for l in range(DEPTH):
        h = rmsnorm(x, norm_g[l])
        proj = h @ w_in[l]
        a_u, a_v, a_g, b_q, b_k, b_v, b_g, c_in, c_g, d_q, d_g = jnp.split(proj, N_IN_SLOTS, axis=-1)
        y_a = chunked_spatial_gating(a_u, a_v, gm_ln_g[l], gm_ln_b[l], gm_w_s[l], gm_b_s[l]) * jax.nn.silu(a_g)
        y_b = neighbourhood_attention(b_q, b_k, b_v, na_qn_g[l], na_kn_g[l], na_rpb[l]) * jax.nn.silu(b_g)
        y_c = fourier_mixing(c_in, fn_w[l]) * jax.nn.silu(c_g)
        mem_n = rmsnorm(mem, mem_norm_g[l])
        y_d = memory_attention(d_q, mem_n, mem_w_kv[l], mem_qn_g[l], mem_kn_g[l]) * jax.nn.silu(d_g)
        y = jnp.concatenate([y_a, y_b, y_c, y_d], axis=-1) @ w_out[l]
        x = x + y
    return x
```

```python
import functools

import numpy as np
import jax
import jax.numpy as jnp
from jax import lax
from jax.experimental import pallas as pl
from jax.experimental.pallas import tpu as pltpu

D_MODEL = 1024
BATCH = 8
SEQ = 4096
DEPTH = 2
N_MEM = 256
GROUP_W = 256
N_SUB = 4
HEAD_DIM = 64
CHUNK = 128
GRID_W = 64
GRID_H = SEQ // GRID_W
WIN_H = 8
WIN_W = 16
N_IN_SLOTS = 11
IN_COLS = N_IN_SLOTS * GROUP_W
EPS = 1e-6
NEG = -1e30

SLOT_C_IN = 7
N_MAIN_SLOTS = N_IN_SLOTS - 1
MAIN_COLS = N_MAIN_SLOTS * GROUP_W
M_AU, M_AV, M_AG, M_BQ, M_BK, M_BV, M_BG, M_CG, M_DQ, M_DG = range(10)

TM = 512
ROWS_PER_TILE = TM // GRID_W
FFT_G = 8

F32 = jnp.float32
BF16 = jnp.bfloat16
VMEM_LIMIT = 56 * 1024 * 1024


@functools.lru_cache(maxsize=None)
def _dft_tables():
    g = np.arange(GRID_W // FFT_G)
    j = np.arange(FFT_G)
    k = np.arange(GRID_W)
    n = np.arange(GRID_W)
    s = GRID_W * n[None, None, None, :] + FFT_G * g[:, None, None, None] + j[None, :, None, None]
    phase = (k[None, None, :, None] * s) % SEQ
    ang = 2.0 * np.pi * phase.astype(np.float64) / SEQ
    g1 = np.zeros((len(g), 2, FFT_G, GRID_W, GRID_W, FFT_G), np.float64)
    for jj in range(FFT_G):
        g1[:, 0, jj, :, :, jj] = np.cos(ang[:, jj])
        g1[:, 1, jj, :, :, jj] = np.sin(ang[:, jj])
    g1 = g1.reshape(len(g), 2 * FFT_G * GRID_W, GRID_W * FFT_G)
    ang2 = 2.0 * np.pi * ((k[:, None] * n[None, :]) % GRID_W).astype(np.float64) / GRID_W
    c2, s2 = np.cos(ang2), np.sin(ang2)
    m2 = np.zeros((2, GRID_W, FFT_G, 2, GRID_W, FFT_G), np.float64)
    for jj in range(FFT_G):
        m2[0, :, jj, 0, :, jj] = c2
        m2[0, :, jj, 1, :, jj] = -s2
        m2[1, :, jj, 0, :, jj] = s2
        m2[1, :, jj, 1, :, jj] = c2
    m2 = m2.reshape(2 * GRID_W * FFT_G, 2 * GRID_W * FFT_G)
    d = np.arange(HEAD_DIM)
    ang3 = 2.0 * np.pi * ((d[:, None] * d[None, :]) % HEAD_DIM).astype(np.float64) / HEAD_DIM
    scale = 1.0 / np.sqrt(float(SEQ * HEAD_DIM))
    cd = np.kron(np.eye(N_SUB), np.cos(ang3)) * scale
    sd = np.kron(np.eye(N_SUB), np.sin(ang3)) * scale
    m3 = np.concatenate([cd, -sd], axis=0)
    return g1.astype(np.float32), m2.astype(np.float32), m3.astype(np.float32)


@functools.lru_cache(maxsize=None)
def _head_mean_matrix():
    return np.kron(np.eye(N_SUB), np.full((HEAD_DIM, HEAD_DIM), 1.0 / HEAD_DIM)).astype(np.float32)


def _natten_bias_table(rpb):
    p = np.arange(WIN_H)
    k = np.arange(WIN_H)
    d_row = k[None, :] - p[:, None] + (WIN_H - 1)
    c = np.arange(GRID_W)
    d_col = np.clip(c[None, :] - c[:, None] + (WIN_W - 1), 0, 2 * WIN_W - 2)
    col_start = np.clip(c - WIN_W // 2, 0, GRID_W - WIN_W)
    col_ok = (c[None, :] >= col_start[:, None]) & (c[None, :] < col_start[:, None] + WIN_W)
    bias = rpb[:, d_row[:, None, :, None], d_col[None, :, None, :]]
    bias = jnp.where(col_ok[None, None, :, None, :], bias.astype(F32), NEG)
    return bias.transpose(1, 0, 2, 3, 4).reshape(WIN_H, N_SUB * GRID_W, WIN_H * GRID_W)


def _head_rmsnorm(y, gain, ones_bd):
    y2 = y * y
    hi = y2.astype(BF16)
    lo = (y2 - hi.astype(F32)).astype(BF16)
    ms = (jnp.dot(hi, ones_bd, preferred_element_type=F32)
          + jnp.dot(lo, ones_bd, preferred_element_type=F32))
    return y * lax.rsqrt(ms + EPS) * gain


def _silu(g):
    return g * jax.nn.sigmoid(g)


def _lane_head(rows):
    return lax.broadcasted_iota(jnp.int32, (rows, GROUP_W), 1) // HEAD_DIM


def _stack_heads(q, lane_head):
    qf = q.astype(F32)
    return jnp.concatenate(
        [jnp.where(lane_head == h, qf, 0.0) for h in range(N_SUB)], axis=0).astype(BF16)


def _pick_heads(o, lane_head, rows):
    out = o[0:rows]
    for h in range(1, N_SUB):
        out = jnp.where(lane_head == h, o[h * rows:(h + 1) * rows], out)
    return out


def _softmax_pv(s, v):
    m = jnp.max(s, axis=-1, keepdims=True)
    e = jnp.exp(s - m)
    l = jnp.sum(e, axis=-1, keepdims=True)
    return jnp.dot(e.astype(BF16), v, preferred_element_type=F32) / l


_NT = (((1,), (1,)), ((), ()))


def _mem_kv_kernel(mem_ref, g_ref, w_ref, kg_ref, ones_ref, k_ref, v_ref):
    m = mem_ref[0]
    ms = jnp.mean(m * m, axis=-1, keepdims=True)
    mn = (m * lax.rsqrt(ms + EPS) * g_ref[...]).astype(BF16)
    kv = jnp.dot(mn, w_ref[...], preferred_element_type=F32)
    k = _head_rmsnorm(kv[:, :GROUP_W], kg_ref[...], ones_ref[...])
    k_ref[0] = k.astype(BF16)
    v_ref[0] = kv[:, GROUP_W:].astype(BF16)


def _in_proj_kernel(x_ref, g_ref, w_ref, hg_ref, ones_ref, proj_ref, cin_ref):
    x = x_ref[...]
    ms = jnp.mean(x * x, axis=-1, keepdims=True)
    h = (x * lax.rsqrt(ms + EPS) * g_ref[...]).astype(BF16)
    head_gain_row = {3: 0, 4: 1, 9: 2}
    col = 0
    for j in range(N_IN_SLOTS):
        y = jnp.dot(h, w_ref[:, j * GROUP_W:(j + 1) * GROUP_W], preferred_element_type=F32)
        if j in head_gain_row:
            r = head_gain_row[j]
            y = _head_rmsnorm(y, hg_ref[r:r + 1, :], ones_ref[...])
        if j == SLOT_C_IN:
            cin_ref[...] = y
        else:
            proj_ref[:, col:col + GROUP_W] = y.astype(BF16)
            col += GROUP_W


def _fnet_stage1_kernel(x_ref, g1_ref, t_ref):
    x = x_ref[0].reshape(GRID_W * FFT_G, GROUP_W).astype(BF16)
    y = jnp.dot(g1_ref[0], x, preferred_element_type=F32)
    t_ref[0] = y.reshape(2, FFT_G, GRID_W, GROUP_W)


def _fnet_stage2_kernel(t_ref, m2_ref, m3_ref, wf_ref, z_ref):
    half = GRID_W * FFT_G
    t = t_ref[0].reshape(2 * half, GROUP_W).astype(BF16)
    ab = jnp.dot(m2_ref[...], t, preferred_element_type=F32)
    ab = jnp.concatenate([ab[:half], ab[half:]], axis=-1).astype(BF16)
    z = jnp.dot(ab, m3_ref[...], preferred_element_type=F32)
    zc = jnp.dot(z.astype(BF16), wf_ref[...], preferred_element_type=F32)
    z_ref[0] = zc.reshape(GRID_W, FFT_G, GROUP_W)


def _mixer_kernel(x_ref, au_ref, av_ref, ag_ref, bq_ref, bk_ref, bv_ref, bg_ref, cg_ref, dq_ref, dg_ref,
                  zc_ref, km_ref, vm_ref, wout_ref, lng_ref, lnb_ref, ws_ref, bs_ref, bias_ref,
                  o_ref, y_ref, yb_ref):
    t = pl.program_id(1)

    lh_chunk = _lane_head(CHUNK)
    for c in range(TM // CHUNK):
        rows = slice(c * CHUNK, (c + 1) * CHUNK)
        v = av_ref[0, rows, :].astype(F32)
        mu = jnp.mean(v, axis=-1, keepdims=True)
        var = jnp.mean(jnp.square(v - mu), axis=-1, keepdims=True)
        vn = ((v - mu) * lax.rsqrt(var + EPS) * lng_ref[...] + lnb_ref[...]).astype(BF16)
        sp = jnp.dot(ws_ref[...], vn, preferred_element_type=F32)
        s = _pick_heads(sp, lh_chunk, CHUNK) + bs_ref[...]
        ya = au_ref[0, rows, :].astype(F32) * s * _silu(ag_ref[0, rows, :].astype(F32))
        y_ref[rows, 0:GROUP_W] = ya.astype(BF16)

    lh_tile = _lane_head(TM)
    qf = dq_ref[0].astype(F32)
    km = km_ref[0]
    vm = vm_ref[0]
    yd = jnp.zeros((TM, GROUP_W), F32)
    for h in range(N_SUB):
        qm = jnp.where(lh_tile == h, qf, 0.0).astype(BF16)
        s = lax.dot_general(qm, km, _NT, preferred_element_type=F32)
        yd = jnp.where(lh_tile == h, _softmax_pv(s, vm), yd)
    y_ref[:, 3 * GROUP_W:4 * GROUP_W] = (yd * _silu(dg_ref[0].astype(F32))).astype(BF16)

    lh_row = _lane_head(GRID_W)

    def row_body(i, carry):
        r = t * ROWS_PER_TILE + i
        rs = jnp.clip(r - WIN_H // 2, 0, GRID_H - WIN_H)
        q0 = pl.multiple_of(i * GRID_W, GRID_W)
        k0 = pl.multiple_of(rs * GRID_W, GRID_W)
        qs = _stack_heads(bq_ref[0, pl.ds(q0, GRID_W), :], lh_row)
        kw = bk_ref[0, pl.ds(k0, WIN_H * GRID_W), :]
        vw = bv_ref[0, pl.ds(k0, WIN_H * GRID_W), :]
        s = lax.dot_general(qs, kw, _NT, preferred_element_type=F32) + bias_ref[r - rs]
        o = _softmax_pv(s, vw)
        yb_ref[pl.ds(q0, GRID_W), :] = _pick_heads(o, lh_row, GRID_W)
        return carry

    lax.fori_loop(0, ROWS_PER_TILE, row_body, 0)
    y_ref[:, GROUP_W:2 * GROUP_W] = (yb_ref[...] * _silu(bg_ref[0].astype(F32))).astype(BF16)

    y_ref[:, 2 * GROUP_W:3 * GROUP_W] = (zc_ref[0] * _silu(cg_ref[0].astype(F32))).astype(BF16)

    o_ref[0] = x_ref[0] + jnp.dot(y_ref[...], wout_ref[...], preferred_element_type=F32)


def _const_spec(shape):
    nd = len(shape)
    return pl.BlockSpec(shape, lambda *_: (0,) * nd)


def _mem_kv(mem, g, w_kv, kg, ones_bd):
    return pl.pallas_call(
        _mem_kv_kernel,
        grid=(BATCH,),
        in_specs=[
            pl.BlockSpec((1, N_MEM, D_MODEL), lambda b: (b, 0, 0)),
            _const_spec((1, D_MODEL)),
            _const_spec((D_MODEL, 2 * GROUP_W)),
            _const_spec((1, GROUP_W)),
            _const_spec((GROUP_W, GROUP_W)),
        ],
        out_specs=[pl.BlockSpec((1, N_MEM, GROUP_W), lambda b: (b, 0, 0))] * 2,
        out_shape=[jax.ShapeDtypeStruct((BATCH, N_MEM, GROUP_W), BF16)] * 2,
        compiler_params=pltpu.CompilerParams(dimension_semantics=("arbitrary",), vmem_limit_bytes=VMEM_LIMIT),
        name="mem_kv",
    )(mem, g, w_kv, kg, ones_bd)


def _in_proj(x2d, g, w_in, head_gains, ones_bd):
    n_tok = x2d.shape[0]
    return pl.pallas_call(
        _in_proj_kernel,
        grid=(n_tok // TM,),
        in_specs=[
            pl.BlockSpec((TM, D_MODEL), lambda i: (i, 0)),
            _const_spec((1, D_MODEL)),
            _const_spec((D_MODEL, IN_COLS)),
            _const_spec((8, GROUP_W)),
            _const_spec((GROUP_W, GROUP_W)),
        ],
        out_specs=[
            pl.BlockSpec((TM, MAIN_COLS), lambda i: (i, 0)),
            pl.BlockSpec((TM, GROUP_W), lambda i: (i, 0)),
        ],
        out_shape=[
            jax.ShapeDtypeStruct((n_tok, MAIN_COLS), BF16),
            jax.ShapeDtypeStruct((n_tok, GROUP_W), F32),
        ],
        compiler_params=pltpu.CompilerParams(dimension_semantics=("arbitrary",), vmem_limit_bytes=VMEM_LIMIT),
        name="in_proj",
    )(x2d, g, w_in, head_gains, ones_bd)


def _fnet(c_in, g1, m2, m3, wf_bd):
    n_groups = GRID_W // FFT_G
    x4 = c_in.reshape(BATCH, GRID_W, GRID_W, GROUP_W)
    t5 = pl.pallas_call(
        _fnet_stage1_kernel,
        grid=(BATCH, n_groups),
        in_specs=[
            pl.BlockSpec((1, GRID_W, FFT_G, GROUP_W), lambda b, g: (b, 0, g, 0)),
            pl.BlockSpec((1,) + g1.shape[1:], lambda b, g: (g, 0, 0)),
        ],
        out_specs=pl.BlockSpec((1, 2, FFT_G, GRID_W, GROUP_W), lambda b, g: (b, 0, g, 0, 0)),
        out_shape=jax.ShapeDtypeStruct((BATCH, 2, GRID_W, GRID_W, GROUP_W), F32),
        compiler_params=pltpu.CompilerParams(dimension_semantics=("arbitrary", "arbitrary"),
                                             vmem_limit_bytes=VMEM_LIMIT),
        name="fnet_s1",
    )(x4, g1)
    z4 = pl.pallas_call(
        _fnet_stage2_kernel,
        grid=(BATCH, n_groups),
        in_specs=[
            pl.BlockSpec((1, 2, GRID_W, FFT_G, GROUP_W), lambda b, g: (b, 0, 0, g, 0)),
            _const_spec(m2.shape),
            _const_spec(m3.shape),
            _const_spec(wf_bd.shape),
        ],
        out_specs=pl.BlockSpec((1, GRID_W, FFT_G, GROUP_W), lambda b, g: (b, 0, g, 0)),
        out_shape=jax.ShapeDtypeStruct((BATCH, GRID_W, GRID_W, GROUP_W), F32),
        compiler_params=pltpu.CompilerParams(dimension_semantics=("arbitrary", "arbitrary"),
                                             vmem_limit_bytes=VMEM_LIMIT),
        name="fnet_s2",
    )(t5, m2, m3, wf_bd)
    return z4.reshape(BATCH, SEQ, GROUP_W)


def _mixer(x, proj, zc, km, vm, w_out, ln_g, ln_b, ws_stack, bs_tab, bias_tab):
    def tile_slot(j):
        return pl.BlockSpec((1, TM, GROUP_W), lambda b, t, j=j: (b, t, j))

    def full_slot(j):
        return pl.BlockSpec((1, SEQ, GROUP_W), lambda b, t, j=j: (b, 0, j))

    in_specs = [
        pl.BlockSpec((1, TM, D_MODEL), lambda b, t: (b, t, 0)),
        tile_slot(M_AU), tile_slot(M_AV), tile_slot(M_AG),
        tile_slot(M_BQ), full_slot(M_BK), full_slot(M_BV), tile_slot(M_BG),
        tile_slot(M_CG), tile_slot(M_DQ), tile_slot(M_DG),
        pl.BlockSpec((1, TM, GROUP_W), lambda b, t: (b, t, 0)),
        pl.BlockSpec((1, N_MEM, GROUP_W), lambda b, t: (b, 0, 0)),
        pl.BlockSpec((1, N_MEM, GROUP_W), lambda b, t: (b, 0, 0)),
        _const_spec((D_MODEL, D_MODEL)),
        _const_spec((1, GROUP_W)), _const_spec((1, GROUP_W)),
        _const_spec((N_SUB * CHUNK, CHUNK)),
        _const_spec((CHUNK, GROUP_W)),
        _const_spec(bias_tab.shape),
    ]
    return pl.pallas_call(
        _mixer_kernel,
        grid=(BATCH, SEQ // TM),
        in_specs=in_specs,
        out_specs=pl.BlockSpec((1, TM, D_MODEL), lambda b, t: (b, t, 0)),
        out_shape=jax.ShapeDtypeStruct((BATCH, SEQ, D_MODEL), F32),
        scratch_shapes=[pltpu.VMEM((TM, D_MODEL), BF16), pltpu.VMEM((TM, GROUP_W), F32)],
        compiler_params=pltpu.CompilerParams(dimension_semantics=("arbitrary", "arbitrary"),
                                             vmem_limit_bytes=VMEM_LIMIT),
        name="mixer",
    )(x, *([proj] * 10), zc, km, vm, w_out, ln_g, ln_b, ws_stack, bs_tab, bias_tab)


def _tile_heads(g):
    return jnp.tile(g.astype(F32), N_SUB).reshape(1, GROUP_W)


def kernel(x, mem, norm_g, w_in, w_out, gm_ln_g, gm_ln_b, gm_w_s, gm_b_s, na_qn_g, na_kn_g, na_rpb, fn_w,
           mem_norm_g, mem_w_kv, mem_qn_g, mem_kn_g):
    g1_np, m2_np, m3_np = _dft_tables()
    g1 = jnp.asarray(g1_np).astype(BF16)
    m2 = jnp.asarray(m2_np).astype(BF16)
    m3 = jnp.asarray(m3_np).astype(BF16)
    ones_bd = jnp.asarray(_head_mean_matrix()).astype(BF16)
    qk_scale = HEAD_DIM ** -0.5

    for l in range(DEPTH):
        km, vm = _mem_kv(mem, mem_norm_g[l].reshape(1, D_MODEL), mem_w_kv[l].astype(BF16),
                         _tile_heads(mem_kn_g[l]), ones_bd)
        head_gains = jnp.concatenate([
            _tile_heads(na_qn_g[l]) * qk_scale,
            _tile_heads(na_kn_g[l]),
            _tile_heads(mem_qn_g[l]) * qk_scale,
            jnp.zeros((5, GROUP_W), F32)], axis=0)
        proj, c_in = _in_proj(x.reshape(BATCH * SEQ, D_MODEL), norm_g[l].reshape(1, D_MODEL),
                              w_in[l].astype(BF16), head_gains, ones_bd)
        wf_bd = jax.scipy.linalg.block_diag(*[fn_w[l, h] for h in range(N_SUB)]).astype(BF16)
        zc = _fnet(c_in.reshape(BATCH, SEQ, GROUP_W), g1, m2, m3, wf_bd)
        ws_stack = gm_w_s[l].reshape(N_SUB * CHUNK, CHUNK).astype(BF16)
        bs_tab = jnp.repeat(gm_b_s[l].T, HEAD_DIM, axis=1)
        x = _mixer(x, proj.reshape(BATCH, SEQ, MAIN_COLS), zc, km, vm, w_out[l].astype(BF16),
                   gm_ln_g[l].reshape(1, GROUP_W), gm_ln_b[l].reshape(1, GROUP_W),
                   ws_stack, bs_tab, _natten_bias_table(na_rpb[l]))
    return x
```

```python
import functools

import numpy as np
import jax
import jax.numpy as jnp
from jax import lax
from jax.experimental import pallas as pl
from jax.experimental.pallas import tpu as pltpu

D_MODEL = 1024
BATCH = 8
SEQ = 4096
DEPTH = 2
N_MEM = 256
GROUP_W = 256
N_SUB = 4
HEAD_DIM = 64
CHUNK = 128
GRID_W = 64
GRID_H = SEQ // GRID_W
WIN_H = 8
WIN_W = 16
N_IN_SLOTS = 11
IN_COLS = N_IN_SLOTS * GROUP_W
EPS = 1e-6
NEG = -1e30

SLOT_C_IN = 7
N_MAIN_SLOTS = N_IN_SLOTS - 1
MAIN_COLS = N_MAIN_SLOTS * GROUP_W
M_AU, M_AV, M_AG, M_BQ, M_BK, M_BV, M_BG, M_CG, M_DQ, M_DG = range(10)

TM = 512
ROWS_PER_TILE = TM // GRID_W
FFT_G = 8

F32 = jnp.float32
BF16 = jnp.bfloat16
VMEM_LIMIT = 56 * 1024 * 1024


@functools.lru_cache(maxsize=None)
def _dft_tables():
    g = np.arange(GRID_W // FFT_G)
    j = np.arange(FFT_G)
    k = np.arange(GRID_W)
    n = np.arange(GRID_W)
    s = GRID_W * n[None, None, None, :] + FFT_G * g[:, None, None, None] + j[None, :, None, None]
    phase = (k[None, None, :, None] * s) % SEQ
    ang = 2.0 * np.pi * phase.astype(np.float64) / SEQ
    g1 = np.zeros((len(g), 2, FFT_G, GRID_W, GRID_W, FFT_G), np.float64)
    for jj in range(FFT_G):
        g1[:, 0, jj, :, :, jj] = np.cos(ang[:, jj])
        g1[:, 1, jj, :, :, jj] = np.sin(ang[:, jj])
    g1 = g1.reshape(len(g), 2 * FFT_G * GRID_W, GRID_W * FFT_G)
    ang2 = 2.0 * np.pi * ((k[:, None] * n[None, :]) % GRID_W).astype(np.float64) / GRID_W
    c2, s2 = np.cos(ang2), np.sin(ang2)
    m2 = np.zeros((2, GRID_W, FFT_G, 2, GRID_W, FFT_G), np.float64)
    for jj in range(FFT_G):
        m2[0, :, jj, 0, :, jj] = c2
        m2[0, :, jj, 1, :, jj] = -s2
        m2[1, :, jj, 0, :, jj] = s2
        m2[1, :, jj, 1, :, jj] = c2
    m2 = m2.reshape(2 * GRID_W * FFT_G, 2 * GRID_W * FFT_G)
    d = np.arange(HEAD_DIM)
    ang3 = 2.0 * np.pi * ((d[:, None] * d[None, :]) % HEAD_DIM).astype(np.float64) / HEAD_DIM
    scale = 1.0 / np.sqrt(float(SEQ * HEAD_DIM))
    cd = np.kron(np.eye(N_SUB), np.cos(ang3)) * scale
    sd = np.kron(np.eye(N_SUB), np.sin(ang3)) * scale
    m3 = np.concatenate([cd, -sd], axis=0)
    return g1.astype(np.float32), m2.astype(np.float32), m3.astype(np.float32)


@functools.lru_cache(maxsize=None)
def _head_mean_matrix():
    return np.kron(np.eye(N_SUB), np.full((HEAD_DIM, HEAD_DIM), 1.0 / HEAD_DIM)).astype(np.float32)


def _natten_bias_table(rpb):
    c = np.arange(GRID_W)
    col_start = np.clip(c - WIN_W // 2, 0, GRID_W - WIN_W)
    col_ok = (c[None, :] >= col_start[:, None]) & (c[None, :] < col_start[:, None] + WIN_W)
    lo = GRID_W - WIN_W
    v = jnp.pad(rpb.astype(F32), ((0, 0), (0, 0), (lo, 2 * GRID_W - lo - (2 * WIN_W - 1))))
    flat = jnp.tile(v, (1, 1, GRID_W))[..., :GRID_W * (2 * GRID_W - 1)]
    toe = flat.reshape(N_SUB, 2 * WIN_H - 1, GRID_W, 2 * GRID_W - 1)[..., GRID_W - 1:]
    toe = toe.transpose(0, 2, 1, 3)
    bias = jnp.stack([toe[:, :, WIN_H - 1 - p:2 * WIN_H - 1 - p, :] for p in range(WIN_H)], axis=0)
    bias = jnp.where(col_ok[None, None, :, None, :], bias, NEG)
    return bias.reshape(WIN_H, N_SUB * GRID_W, WIN_H * GRID_W)


def _head_rmsnorm(y, gain, ones_bd):
    y2 = y * y
    hi = y2.astype(BF16)
    lo = (y2 - hi.astype(F32)).astype(BF16)
    ms = (jnp.dot(hi, ones_bd, preferred_element_type=F32)
          + jnp.dot(lo, ones_bd, preferred_element_type=F32))
    return y * lax.rsqrt(ms + EPS) * gain


def _silu(g):
    return g * jax.nn.sigmoid(g)


def _lane_head(rows):
    return lax.broadcasted_iota(jnp.int32, (rows, GROUP_W), 1) // HEAD_DIM


def _stack_heads(q, lane_head):
    qf = q.astype(F32)
    return jnp.concatenate(
        [jnp.where(lane_head == h, qf, 0.0) for h in range(N_SUB)], axis=0).astype(BF16)


def _pick_heads(o, lane_head, rows):
    out = o[0:rows]
    for h in range(1, N_SUB):
        out = jnp.where(lane_head == h, o[h * rows:(h + 1) * rows], out)
    return out


def _softmax_pv(s, v):
    m = jnp.max(s, axis=-1, keepdims=True)
    e = jnp.exp(s - m)
    l = jnp.sum(e, axis=-1, keepdims=True)
    return jnp.dot(e.astype(BF16), v, preferred_element_type=F32) / l


_NT = (((1,), (1,)), ((), ()))


def _mem_kv_kernel(mem_ref, g_ref, w_ref, kg_ref, ones_ref, k_ref, v_ref):
    m = mem_ref[0]
    ms = jnp.mean(m * m, axis=-1, keepdims=True)
    mn = (m * lax.rsqrt(ms + EPS) * g_ref[...]).astype(BF16)
    kv = jnp.dot(mn, w_ref[...], preferred_element_type=F32)
    k = _head_rmsnorm(kv[:, :GROUP_W], kg_ref[...], ones_ref[...])
    k_ref[0] = k.astype(BF16)
    v_ref[0] = kv[:, GROUP_W:].astype(BF16)


def _in_proj_kernel(x_ref, g_ref, w_ref, hg_ref, ones_ref, proj_ref, cin_ref):
    x = x_ref[...]
    ms = jnp.mean(x * x, axis=-1, keepdims=True)
    h = (x * lax.rsqrt(ms + EPS) * g_ref[...]).astype(BF16)
    head_gain_row = {3: 0, 4: 1, 9: 2}
    col = 0
    for j in range(N_IN_SLOTS):
        y = jnp.dot(h, w_ref[:, j * GROUP_W:(j + 1) * GROUP_W], preferred_element_type=F32)
        if j in head_gain_row:
            r = head_gain_row[j]
            y = _head_rmsnorm(y, hg_ref[r:r + 1, :], ones_ref[...])
        if j == SLOT_C_IN:
            cin_ref[...] = y
        else:
            proj_ref[:, col:col + GROUP_W] = y.astype(BF16)
            col += GROUP_W


def _fnet_stage1_kernel(x_ref, g1_ref, t_ref):
    x = x_ref[0].reshape(GRID_W * FFT_G, GROUP_W).astype(BF16)
    y = jnp.dot(g1_ref[0], x, preferred_element_type=F32)
    t_ref[0] = y.reshape(2, FFT_G, GRID_W, GROUP_W)


def _fnet_stage2_kernel(t_ref, m2_ref, m3_ref, wf_ref, z_ref):
    half = GRID_W * FFT_G
    t = t_ref[0].reshape(2 * half, GROUP_W).astype(BF16)
    ab = jnp.dot(m2_ref[...], t, preferred_element_type=F32)
    ab = jnp.concatenate([ab[:half], ab[half:]], axis=-1).astype(BF16)
    z = jnp.dot(ab, m3_ref[...], preferred_element_type=F32)
    zc = jnp.dot(z.astype(BF16), wf_ref[...], preferred_element_type=F32)
    z_ref[0] = zc.reshape(GRID_W, FFT_G, GROUP_W)


def _mixer_kernel(x_ref, au_ref, av_ref, ag_ref, bq_ref, bk_ref, bv_ref, bg_ref, cg_ref, dq_ref, dg_ref,
                  zc_ref, km_ref, vm_ref, wout_ref, lng_ref, lnb_ref, ws_ref, bs_ref, bias_ref,
                  o_ref, y_ref, yb_ref):
    t = pl.program_id(1)

    lh_chunk = _lane_head(CHUNK)
    for c in range(TM // CHUNK):
        rows = slice(c * CHUNK, (c + 1) * CHUNK)
        v = av_ref[0, rows, :].astype(F32)
        mu = jnp.mean(v, axis=-1, keepdims=True)
        var = jnp.mean(jnp.square(v - mu), axis=-1, keepdims=True)
        vn = ((v - mu) * lax.rsqrt(var + EPS) * lng_ref[...] + lnb_ref[...]).astype(BF16)
        sp = jnp.dot(ws_ref[...], vn, preferred_element_type=F32)
        s = _pick_heads(sp, lh_chunk, CHUNK) + bs_ref[...]
        ya = au_ref[0, rows, :].astype(F32) * s * _silu(ag_ref[0, rows, :].astype(F32))
        y_ref[rows, 0:GROUP_W] = ya.astype(BF16)

    lh_tile = _lane_head(TM)
    qf = dq_ref[0].astype(F32)
    km = km_ref[0]
    vm = vm_ref[0]
    yd = jnp.zeros((TM, GROUP_W), F32)
    for h in range(N_SUB):
        qm = jnp.where(lh_tile == h, qf, 0.0).astype(BF16)
        s = lax.dot_general(qm, km, _NT, preferred_element_type=F32)
        yd = jnp.where(lh_tile == h, _softmax_pv(s, vm), yd)
    y_ref[:, 3 * GROUP_W:4 * GROUP_W] = (yd * _silu(dg_ref[0].astype(F32))).astype(BF16)

    lh_row = _lane_head(GRID_W)

    def row_body(i, carry):
        r = t * ROWS_PER_TILE + i
        rs = jnp.clip(r - WIN_H // 2, 0, GRID_H - WIN_H)
        q0 = pl.multiple_of(i * GRID_W, GRID_W)
        k0 = pl.multiple_of(rs * GRID_W, GRID_W)
        qs = _stack_heads(bq_ref[0, pl.ds(q0, GRID_W), :], lh_row)
        kw = bk_ref[0, pl.ds(k0, WIN_H * GRID_W), :]
        vw = bv_ref[0, pl.ds(k0, WIN_H * GRID_W), :]
        s = lax.dot_general(qs, kw, _NT, preferred_element_type=F32) + bias_ref[r - rs]
        o = _softmax_pv(s, vw)
        yb_ref[pl.ds(q0, GRID_W), :] = _pick_heads(o, lh_row, GRID_W)
        return carry

    lax.fori_loop(0, ROWS_PER_TILE, row_body, 0)
    y_ref[:, GROUP_W:2 * GROUP_W] = (yb_ref[...] * _silu(bg_ref[0].astype(F32))).astype(BF16)

    y_ref[:, 2 * GROUP_W:3 * GROUP_W] = (zc_ref[0] * _silu(cg_ref[0].astype(F32))).astype(BF16)

    o_ref[0] = x_ref[0] + jnp.dot(y_ref[...], wout_ref[...], preferred_element_type=F32)


def _const_spec(shape):
    nd = len(shape)
    return pl.BlockSpec(shape, lambda *_: (0,) * nd)


def _mem_kv(mem, g, w_kv, kg, ones_bd):
    return pl.pallas_call(
        _mem_kv_kernel,
        grid=(BATCH,),
        in_specs=[
            pl.BlockSpec((1, N_MEM, D_MODEL), lambda b: (b, 0, 0)),
            _const_spec((1, D_MODEL)),
            _const_spec((D_MODEL, 2 * GROUP_W)),
            _const_spec((1, GROUP_W)),
            _const_spec((GROUP_W, GROUP_W)),
        ],
        out_specs=[pl.BlockSpec((1, N_MEM, GROUP_W), lambda b: (b, 0, 0))] * 2,
        out_shape=[jax.ShapeDtypeStruct((BATCH, N_MEM, GROUP_W), BF16)] * 2,
        compiler_params=pltpu.CompilerParams(dimension_semantics=("arbitrary",), vmem_limit_bytes=VMEM_LIMIT),
        name="mem_kv",
    )(mem, g, w_kv, kg, ones_bd)


def _in_proj(x2d, g, w_in, head_gains, ones_bd):
    n_tok = x2d.shape[0]
    return pl.pallas_call(
        _in_proj_kernel,
        grid=(n_tok // TM,),
        in_specs=[
            pl.BlockSpec((TM, D_MODEL), lambda i: (i, 0)),
            _const_spec((1, D_MODEL)),
            _const_spec((D_MODEL, IN_COLS)),
            _const_spec((8, GROUP_W)),
            _const_spec((GROUP_W, GROUP_W)),
        ],
        out_specs=[
            pl.BlockSpec((TM, MAIN_COLS), lambda i: (i, 0)),
            pl.BlockSpec((TM, GROUP_W), lambda i: (i, 0)),
        ],
        out_shape=[
            jax.ShapeDtypeStruct((n_tok, MAIN_COLS), BF16),
            jax.ShapeDtypeStruct((n_tok, GROUP_W), F32),
        ],
        compiler_params=pltpu.CompilerParams(dimension_semantics=("arbitrary",), vmem_limit_bytes=VMEM_LIMIT),
        name="in_proj",
    )(x2d, g, w_in, head_gains, ones_bd)


def _fnet(c_in, g1, m2, m3, wf_bd):
    n_groups = GRID_W // FFT_G
    x4 = c_in.reshape(BATCH, GRID_W, GRID_W, GROUP_W)
    t5 = pl.pallas_call(
        _fnet_stage1_kernel,
        grid=(BATCH, n_groups),
        in_specs=[
            pl.BlockSpec((1, GRID_W, FFT_G, GROUP_W), lambda b, g: (b, 0, g, 0)),
            pl.BlockSpec((1,) + g1.shape[1:], lambda b, g: (g, 0, 0)),
        ],
        out_specs=pl.BlockSpec((1, 2, FFT_G, GRID_W, GROUP_W), lambda b, g: (b, 0, g, 0, 0)),
        out_shape=jax.ShapeDtypeStruct((BATCH, 2, GRID_W, GRID_W, GROUP_W), F32),
        compiler_params=pltpu.CompilerParams(dimension_semantics=("arbitrary", "arbitrary"),
                                             vmem_limit_bytes=VMEM_LIMIT),
        name="fnet_s1",
    )(x4, g1)
    z4 = pl.pallas_call(
        _fnet_stage2_kernel,
        grid=(BATCH, n_groups),
        in_specs=[
            pl.BlockSpec((1, 2, GRID_W, FFT_G, GROUP_W), lambda b, g: (b, 0, 0, g, 0)),
            _const_spec(m2.shape),
            _const_spec(m3.shape),
            _const_spec(wf_bd.shape),
        ],
        out_specs=pl.BlockSpec((1, GRID_W, FFT_G, GROUP_W), lambda b, g: (b, 0, g, 0)),
        out_shape=jax.ShapeDtypeStruct((BATCH, GRID_W, GRID_W, GROUP_W), F32),
        compiler_params=pltpu.CompilerParams(dimension_semantics=("arbitrary", "arbitrary"),
                                             vmem_limit_bytes=VMEM_LIMIT),
        name="fnet_s2",
    )(t5, m2, m3, wf_bd)
    return z4.reshape(BATCH, SEQ, GROUP_W)


def _mixer(x, proj, zc, km, vm, w_out, ln_g, ln_b, ws_stack, bs_tab, bias_tab):
    def tile_slot(j):
        return pl.BlockSpec((1, TM, GROUP_W), lambda b, t, j=j: (b, t, j))

    def full_slot(j):
        return pl.BlockSpec((1, SEQ, GROUP_W), lambda b, t, j=j: (b, 0, j))

    in_specs = [
        pl.BlockSpec((1, TM, D_MODEL), lambda b, t: (b, t, 0)),
        tile_slot(M_AU), tile_slot(M_AV), tile_slot(M_AG),
        tile_slot(M_BQ), full_slot(M_BK), full_slot(M_BV), tile_slot(M_BG),
        tile_slot(M_CG), tile_slot(M_DQ), tile_slot(M_DG),
        pl.BlockSpec((1, TM, GROUP_W), lambda b, t: (b, t, 0)),
        pl.BlockSpec((1, N_MEM, GROUP_W), lambda b, t: (b, 0, 0)),
        pl.BlockSpec((1, N_MEM, GROUP_W), lambda b, t: (b, 0, 0)),
        _const_spec((D_MODEL, D_MODEL)),
        _const_spec((1, GROUP_W)), _const_spec((1, GROUP_W)),
        _const_spec((N_SUB * CHUNK, CHUNK)),
        _const_spec((CHUNK, GROUP_W)),
        _const_spec(bias_tab.shape),
    ]
    return pl.pallas_call(
        _mixer_kernel,
        grid=(BATCH, SEQ // TM),
        in_specs=in_specs,
        out_specs=pl.BlockSpec((1, TM, D_MODEL), lambda b, t: (b, t, 0)),
        out_shape=jax.ShapeDtypeStruct((BATCH, SEQ, D_MODEL), F32),
        scratch_shapes=[pltpu.VMEM((TM, D_MODEL), BF16), pltpu.VMEM((TM, GROUP_W), F32)],
        compiler_params=pltpu.CompilerParams(dimension_semantics=("arbitrary", "arbitrary"),
                                             vmem_limit_bytes=VMEM_LIMIT),
        name="mixer",
    )(x, *([proj] * 10), zc, km, vm, w_out, ln_g, ln_b, ws_stack, bs_tab, bias_tab)


def _tile_heads(g):
    return jnp.tile(g.astype(F32), N_SUB).reshape(1, GROUP_W)


def kernel(x, mem, norm_g, w_in, w_out, gm_ln_g, gm_ln_b, gm_w_s, gm_b_s, na_qn_g, na_kn_g, na_rpb, fn_w,
           mem_norm_g, mem_w_kv, mem_qn_g, mem_kn_g):
    g1_np, m2_np, m3_np = _dft_tables()
    g1 = jnp.asarray(g1_np).astype(BF16)
    m2 = jnp.asarray(m2_np).astype(BF16)
    m3 = jnp.asarray(m3_np).astype(BF16)
    ones_bd = jnp.asarray(_head_mean_matrix()).astype(BF16)
    qk_scale = HEAD_DIM ** -0.5

    for l in range(DEPTH):
        km, vm = _mem_kv(mem, mem_norm_g[l].reshape(1, D_MODEL), mem_w_kv[l].astype(BF16),
                         _tile_heads(mem_kn_g[l]), ones_bd)
        head_gains = jnp.concatenate([
            _tile_heads(na_qn_g[l]) * qk_scale,
            _tile_heads(na_kn_g[l]),
            _tile_heads(mem_qn_g[l]) * qk_scale,
            jnp.zeros((5, GROUP_W), F32)], axis=0)
        proj, c_in = _in_proj(x.reshape(BATCH * SEQ, D_MODEL), norm_g[l].reshape(1, D_MODEL),
                              w_in[l].astype(BF16), head_gains, ones_bd)
        wf_bd = jax.scipy.linalg.block_diag(*[fn_w[l, h] for h in range(N_SUB)]).astype(BF16)
        zc = _fnet(c_in.reshape(BATCH, SEQ, GROUP_W), g1, m2, m3, wf_bd)
        ws_stack = gm_w_s[l].reshape(N_SUB * CHUNK, CHUNK).astype(BF16)
        bs_tab = jnp.repeat(gm_b_s[l].T, HEAD_DIM, axis=1)
        x = _mixer(x, proj.reshape(BATCH, SEQ, MAIN_COLS), zc, km, vm, w_out[l].astype(BF16),
                   gm_ln_g[l].reshape(1, GROUP_W), gm_ln_b[l].reshape(1, GROUP_W),
                   ws_stack, bs_tab, _natten_bias_table(na_rpb[l]))
    return x
```

```python
import functools

import numpy as np
import jax
import jax.numpy as jnp
from jax import lax
from jax.experimental import pallas as pl
from jax.experimental.pallas import tpu as pltpu

D_MODEL = 1024
BATCH = 8
SEQ = 4096
DEPTH = 2
N_MEM = 256
GROUP_W = 256
N_SUB = 4
HEAD_DIM = 64
CHUNK = 128
GRID_W = 64
GRID_H = SEQ // GRID_W
WIN_H = 8
WIN_W = 16
N_IN_SLOTS = 11
IN_COLS = N_IN_SLOTS * GROUP_W
EPS = 1e-6
NEG = -1e30

SLOT_C_IN = 7
N_MAIN_SLOTS = N_IN_SLOTS - 1
MAIN_COLS = N_MAIN_SLOTS * GROUP_W
M_AU, M_AV, M_AG, M_BQ, M_BK, M_BV, M_BG, M_CG, M_DQ, M_DG = range(10)

TM = 512
ROWS_PER_TILE = TM // GRID_W
NATTEN_UNROLL = 8
FFT_G = 8

F32 = jnp.float32
BF16 = jnp.bfloat16
VMEM_LIMIT = 56 * 1024 * 1024


@functools.lru_cache(maxsize=None)
def _dft_tables():
    g = np.arange(GRID_W // FFT_G)
    j = np.arange(FFT_G)
    k = np.arange(GRID_W)
    n = np.arange(GRID_W)
    s = GRID_W * n[None, None, None, :] + FFT_G * g[:, None, None, None] + j[None, :, None, None]
    phase = (k[None, None, :, None] * s) % SEQ
    ang = 2.0 * np.pi * phase.astype(np.float64) / SEQ
    g1 = np.zeros((len(g), 2, FFT_G, GRID_W, GRID_W, FFT_G), np.float64)
    for jj in range(FFT_G):
        g1[:, 0, jj, :, :, jj] = np.cos(ang[:, jj])
        g1[:, 1, jj, :, :, jj] = np.sin(ang[:, jj])
    g1 = g1.reshape(len(g), 2 * FFT_G * GRID_W, GRID_W * FFT_G)
    ang2 = 2.0 * np.pi * ((k[:, None] * n[None, :]) % GRID_W).astype(np.float64) / GRID_W
    c2, s2 = np.cos(ang2), np.sin(ang2)
    m2 = np.zeros((2, GRID_W, FFT_G, 2, GRID_W, FFT_G), np.float64)
    for jj in range(FFT_G):
        m2[0, :, jj, 0, :, jj] = c2
        m2[0, :, jj, 1, :, jj] = -s2
        m2[1, :, jj, 0, :, jj] = s2
        m2[1, :, jj, 1, :, jj] = c2
    m2 = m2.reshape(2 * GRID_W * FFT_G, 2 * GRID_W * FFT_G)
    d = np.arange(HEAD_DIM)
    ang3 = 2.0 * np.pi * ((d[:, None] * d[None, :]) % HEAD_DIM).astype(np.float64) / HEAD_DIM
    scale = 1.0 / np.sqrt(float(SEQ * HEAD_DIM))
    cd = np.kron(np.eye(N_SUB), np.cos(ang3)) * scale
    sd = np.kron(np.eye(N_SUB), np.sin(ang3)) * scale
    m3 = np.concatenate([cd, -sd], axis=0)
    return g1.astype(np.float32), m2.astype(np.float32), m3.astype(np.float32)


@functools.lru_cache(maxsize=None)
def _head_mean_matrix():
    return np.kron(np.eye(N_SUB), np.full((HEAD_DIM, HEAD_DIM), 1.0 / HEAD_DIM)).astype(np.float32)


def _natten_bias_table(rpb):
    c = np.arange(GRID_W)
    col_start = np.clip(c - WIN_W // 2, 0, GRID_W - WIN_W)
    col_ok = (c[None, :] >= col_start[:, None]) & (c[None, :] < col_start[:, None] + WIN_W)
    lo = GRID_W - WIN_W
    v = jnp.pad(rpb.astype(F32), ((0, 0), (0, 0), (lo, 2 * GRID_W - lo - (2 * WIN_W - 1))))
    flat = jnp.tile(v, (1, 1, GRID_W))[..., :GRID_W * (2 * GRID_W - 1)]
    toe = flat.reshape(N_SUB, 2 * WIN_H - 1, GRID_W, 2 * GRID_W - 1)[..., GRID_W - 1:]
    toe = toe.transpose(0, 2, 1, 3)
    bias = jnp.stack([toe[:, :, WIN_H - 1 - p:2 * WIN_H - 1 - p, :] for p in range(WIN_H)], axis=0)
    bias = jnp.where(col_ok[None, None, :, None, :], bias, NEG)
    return bias.reshape(WIN_H, N_SUB * GRID_W, WIN_H * GRID_W)


def _head_rmsnorm(y, gain, ones_bd):
    y2 = y * y
    hi = y2.astype(BF16)
    lo = (y2 - hi.astype(F32)).astype(BF16)
    ms = (jnp.dot(hi, ones_bd, preferred_element_type=F32)
          + jnp.dot(lo, ones_bd, preferred_element_type=F32))
    return y * lax.rsqrt(ms + EPS) * gain


def _silu(g):
    return g * jax.nn.sigmoid(g)


def _lane_head(rows):
    return lax.broadcasted_iota(jnp.int32, (rows, GROUP_W), 1) // HEAD_DIM


def _stack_heads(q, lane_head):
    qf = q.astype(F32)
    return jnp.concatenate(
        [jnp.where(lane_head == h, qf, 0.0) for h in range(N_SUB)], axis=0).astype(BF16)


def _pick_heads(o, lane_head, rows):
    out = o[0:rows]
    for h in range(1, N_SUB):
        out = jnp.where(lane_head == h, o[h * rows:(h + 1) * rows], out)
    return out


def _softmax_pv(s, v):
    m = jnp.max(s, axis=-1, keepdims=True)
    e = jnp.exp(s - m)
    l = jnp.sum(e, axis=-1, keepdims=True)
    return jnp.dot(e.astype(BF16), v, preferred_element_type=F32) / l


_NT = (((1,), (1,)), ((), ()))


def _mem_kv_kernel(mem_ref, g_ref, w_ref, kg_ref, ones_ref, k_ref, v_ref):
    m = mem_ref[0]
    ms = jnp.mean(m * m, axis=-1, keepdims=True)
    mn = (m * lax.rsqrt(ms + EPS) * g_ref[...]).astype(BF16)
    kv = jnp.dot(mn, w_ref[...], preferred_element_type=F32)
    k = _head_rmsnorm(kv[:, :GROUP_W], kg_ref[...], ones_ref[...])
    k_ref[0] = k.astype(BF16)
    v_ref[0] = kv[:, GROUP_W:].astype(BF16)


def _in_proj_kernel(x_ref, g_ref, w_ref, hg_ref, ones_ref, proj_ref, cin_ref):
    x = x_ref[...]
    ms = jnp.mean(x * x, axis=-1, keepdims=True)
    h = (x * lax.rsqrt(ms + EPS) * g_ref[...]).astype(BF16)
    head_gain_row = {3: 0, 4: 1, 9: 2}
    col = 0
    for j in range(N_IN_SLOTS):
        y = jnp.dot(h, w_ref[:, j * GROUP_W:(j + 1) * GROUP_W], preferred_element_type=F32)
        if j in head_gain_row:
            r = head_gain_row[j]
            y = _head_rmsnorm(y, hg_ref[r:r + 1, :], ones_ref[...])
        if j == SLOT_C_IN:
            cin_ref[...] = y
        else:
            proj_ref[:, col:col + GROUP_W] = y.astype(BF16)
            col += GROUP_W


def _fnet_stage1_kernel(x_ref, g1_ref, t_ref):
    x = x_ref[0].reshape(GRID_W * FFT_G, GROUP_W).astype(BF16)
    y = jnp.dot(g1_ref[0], x, preferred_element_type=F32)
    t_ref[0] = y.reshape(2, FFT_G, GRID_W, GROUP_W)


def _fnet_stage2_kernel(t_ref, m2_ref, m3_ref, wf_ref, z_ref):
    half = GRID_W * FFT_G
    t = t_ref[0].reshape(2 * half, GROUP_W).astype(BF16)
    ab = jnp.dot(m2_ref[...], t, preferred_element_type=F32)
    ab = jnp.concatenate([ab[:half], ab[half:]], axis=-1).astype(BF16)
    z = jnp.dot(ab, m3_ref[...], preferred_element_type=F32)
    zc = jnp.dot(z.astype(BF16), wf_ref[...], preferred_element_type=F32)
    z_ref[0] = zc.reshape(GRID_W, FFT_G, GROUP_W)


def _mixer_kernel(x_ref, au_ref, av_ref, ag_ref, bq_ref, bk_ref, bv_ref, bg_ref, cg_ref, dq_ref, dg_ref,
                  zc_ref, km_ref, vm_ref, wout_ref, lng_ref, lnb_ref, ws_ref, bs_ref, bias_ref,
                  o_ref, y_ref, yb_ref):
    t = pl.program_id(1)

    lh_chunk = _lane_head(CHUNK)
    for c in range(TM // CHUNK):
        rows = slice(c * CHUNK, (c + 1) * CHUNK)
        v = av_ref[0, rows, :].astype(F32)
        mu = jnp.mean(v, axis=-1, keepdims=True)
        var = jnp.mean(jnp.square(v - mu), axis=-1, keepdims=True)
        vn = ((v - mu) * lax.rsqrt(var + EPS) * lng_ref[...] + lnb_ref[...]).astype(BF16)
        sp = jnp.dot(ws_ref[...], vn, preferred_element_type=F32)
        s = _pick_heads(sp, lh_chunk, CHUNK) + bs_ref[...]
        ya = au_ref[0, rows, :].astype(F32) * s * _silu(ag_ref[0, rows, :].astype(F32))
        y_ref[rows, 0:GROUP_W] = ya.astype(BF16)

    lh_tile = _lane_head(TM)
    qf = dq_ref[0].astype(F32)
    km = km_ref[0]
    vm = vm_ref[0]
    yd = jnp.zeros((TM, GROUP_W), F32)
    for h in range(N_SUB):
        qm = jnp.where(lh_tile == h, qf, 0.0).astype(BF16)
        s = lax.dot_general(qm, km, _NT, preferred_element_type=F32)
        yd = jnp.where(lh_tile == h, _softmax_pv(s, vm), yd)
    y_ref[:, 3 * GROUP_W:4 * GROUP_W] = (yd * _silu(dg_ref[0].astype(F32))).astype(BF16)

    lh_row = _lane_head(GRID_W)

    def row_body(i, carry):
        r = t * ROWS_PER_TILE + i
        rs = jnp.clip(r - WIN_H // 2, 0, GRID_H - WIN_H)
        q0 = pl.multiple_of(i * GRID_W, GRID_W)
        k0 = pl.multiple_of(rs * GRID_W, GRID_W)
        qs = _stack_heads(bq_ref[0, pl.ds(q0, GRID_W), :], lh_row)
        kw = bk_ref[0, pl.ds(k0, WIN_H * GRID_W), :]
        vw = bv_ref[0, pl.ds(k0, WIN_H * GRID_W), :]
        s = lax.dot_general(qs, kw, _NT, preferred_element_type=F32) + bias_ref[r - rs]
        o = _softmax_pv(s, vw)
        yb_ref[pl.ds(q0, GRID_W), :] = _pick_heads(o, lh_row, GRID_W)
        return carry

    lax.fori_loop(0, ROWS_PER_TILE, row_body, 0, unroll=NATTEN_UNROLL)
    y_ref[:, GROUP_W:2 * GROUP_W] = (yb_ref[...] * _silu(bg_ref[0].astype(F32))).astype(BF16)

    y_ref[:, 2 * GROUP_W:3 * GROUP_W] = (zc_ref[0] * _silu(cg_ref[0].astype(F32))).astype(BF16)

    o_ref[0] = x_ref[0] + jnp.dot(y_ref[...], wout_ref[...], preferred_element_type=F32)


def _const_spec(shape):
    nd = len(shape)
    return pl.BlockSpec(shape, lambda *_: (0,) * nd)


def _mem_kv(mem, g, w_kv, kg, ones_bd):
    return pl.pallas_call(
        _mem_kv_kernel,
        grid=(BATCH,),
        in_specs=[
            pl.BlockSpec((1, N_MEM, D_MODEL), lambda b: (b, 0, 0)),
            _const_spec((1, D_MODEL)),
            _const_spec((D_MODEL, 2 * GROUP_W)),
            _const_spec((1, GROUP_W)),
            _const_spec((GROUP_W, GROUP_W)),
        ],
        out_specs=[pl.BlockSpec((1, N_MEM, GROUP_W), lambda b: (b, 0, 0))] * 2,
        out_shape=[jax.ShapeDtypeStruct((BATCH, N_MEM, GROUP_W), BF16)] * 2,
        compiler_params=pltpu.CompilerParams(dimension_semantics=("arbitrary",), vmem_limit_bytes=VMEM_LIMIT),
        name="mem_kv",
    )(mem, g, w_kv, kg, ones_bd)


def _in_proj(x2d, g, w_in, head_gains, ones_bd):
    n_tok = x2d.shape[0]
    return pl.pallas_call(
        _in_proj_kernel,
        grid=(n_tok // TM,),
        in_specs=[
            pl.BlockSpec((TM, D_MODEL), lambda i: (i, 0)),
            _const_spec((1, D_MODEL)),
            _const_spec((D_MODEL, IN_COLS)),
            _const_spec((8, GROUP_W)),
            _const_spec((GROUP_W, GROUP_W)),
        ],
        out_specs=[
            pl.BlockSpec((TM, MAIN_COLS), lambda i: (i, 0)),
            pl.BlockSpec((TM, GROUP_W), lambda i: (i, 0)),
        ],
        out_shape=[
            jax.ShapeDtypeStruct((n_tok, MAIN_COLS), BF16),
            jax.ShapeDtypeStruct((n_tok, GROUP_W), F32),
        ],
        compiler_params=pltpu.CompilerParams(dimension_semantics=("arbitrary",), vmem_limit_bytes=VMEM_LIMIT),
        name="in_proj",
    )(x2d, g, w_in, head_gains, ones_bd)


def _fnet(c_in, g1, m2, m3, wf_bd):
    n_groups = GRID_W // FFT_G
    x4 = c_in.reshape(BATCH, GRID_W, GRID_W, GROUP_W)
    t5 = pl.pallas_call(
        _fnet_stage1_kernel,
        grid=(BATCH, n_groups),
        in_specs=[
            pl.BlockSpec((1, GRID_W, FFT_G, GROUP_W), lambda b, g: (b, 0, g, 0)),
            pl.BlockSpec((1,) + g1.shape[1:], lambda b, g: (g, 0, 0)),
        ],
        out_specs=pl.BlockSpec((1, 2, FFT_G, GRID_W, GROUP_W), lambda b, g: (b, 0, g, 0, 0)),
        out_shape=jax.ShapeDtypeStruct((BATCH, 2, GRID_W, GRID_W, GROUP_W), F32),
        compiler_params=pltpu.CompilerParams(dimension_semantics=("arbitrary", "arbitrary"),
                                             vmem_limit_bytes=VMEM_LIMIT),
        name="fnet_s1",
    )(x4, g1)
    z4 = pl.pallas_call(
        _fnet_stage2_kernel,
        grid=(BATCH, n_groups),
        in_specs=[
            pl.BlockSpec((1, 2, GRID_W, FFT_G, GROUP_W), lambda b, g: (b, 0, 0, g, 0)),
            _const_spec(m2.shape),
            _const_spec(m3.shape),
            _const_spec(wf_bd.shape),
        ],
        out_specs=pl.BlockSpec((1, GRID_W, FFT_G, GROUP_W), lambda b, g: (b, 0, g, 0)),
        out_shape=jax.ShapeDtypeStruct((BATCH, GRID_W, GRID_W, GROUP_W), F32),
        compiler_params=pltpu.CompilerParams(dimension_semantics=("arbitrary", "arbitrary"),
                                             vmem_limit_bytes=VMEM_LIMIT),
        name="fnet_s2",
    )(t5, m2, m3, wf_bd)
    return z4.reshape(BATCH, SEQ, GROUP_W)


def _mixer(x, proj, zc, km, vm, w_out, ln_g, ln_b, ws_stack, bs_tab, bias_tab):
    def tile_slot(j):
        return pl.BlockSpec((1, TM, GROUP_W), lambda b, t, j=j: (b, t, j))

    def full_slot(j):
        return pl.BlockSpec((1, SEQ, GROUP_W), lambda b, t, j=j: (b, 0, j))

    in_specs = [
        pl.BlockSpec((1, TM, D_MODEL), lambda b, t: (b, t, 0)),
        tile_slot(M_AU), tile_slot(M_AV), tile_slot(M_AG),
        tile_slot(M_BQ), full_slot(M_BK), full_slot(M_BV), tile_slot(M_BG),
        tile_slot(M_CG), tile_slot(M_DQ), tile_slot(M_DG),
        pl.BlockSpec((1, TM, GROUP_W), lambda b, t: (b, t, 0)),
        pl.BlockSpec((1, N_MEM, GROUP_W), lambda b, t: (b, 0, 0)),
        pl.BlockSpec((1, N_MEM, GROUP_W), lambda b, t: (b, 0, 0)),
        _const_spec((D_MODEL, D_MODEL)),
        _const_spec((1, GROUP_W)), _const_spec((1, GROUP_W)),
        _const_spec((N_SUB * CHUNK, CHUNK)),
        _const_spec((CHUNK, GROUP_W)),
        _const_spec(bias_tab.shape),
    ]
    return pl.pallas_call(
        _mixer_kernel,
        grid=(BATCH, SEQ // TM),
        in_specs=in_specs,
        out_specs=pl.BlockSpec((1, TM, D_MODEL), lambda b, t: (b, t, 0)),
        out_shape=jax.ShapeDtypeStruct((BATCH, SEQ, D_MODEL), F32),
        scratch_shapes=[pltpu.VMEM((TM, D_MODEL), BF16), pltpu.VMEM((TM, GROUP_W), F32)],
        compiler_params=pltpu.CompilerParams(dimension_semantics=("arbitrary", "arbitrary"),
                                             vmem_limit_bytes=VMEM_LIMIT),
        name="mixer",
    )(x, *([proj] * 10), zc, km, vm, w_out, ln_g, ln_b, ws_stack, bs_tab, bias_tab)


def _tile_heads(g):
    return jnp.tile(g.astype(F32), N_SUB).reshape(1, GROUP_W)


def kernel(x, mem, norm_g, w_in, w_out, gm_ln_g, gm_ln_b, gm_w_s, gm_b_s, na_qn_g, na_kn_g, na_rpb, fn_w,
           mem_norm_g, mem_w_kv, mem_qn_g, mem_kn_g):
    g1_np, m2_np, m3_np = _dft_tables()
    g1 = jnp.asarray(g1_np).astype(BF16)
    m2 = jnp.asarray(m2_np).astype(BF16)
    m3 = jnp.asarray(m3_np).astype(BF16)
    ones_bd = jnp.asarray(_head_mean_matrix()).astype(BF16)
    qk_scale = HEAD_DIM ** -0.5

    for l in range(DEPTH):
        km, vm = _mem_kv(mem, mem_norm_g[l].reshape(1, D_MODEL), mem_w_kv[l].astype(BF16),
                         _tile_heads(mem_kn_g[l]), ones_bd)
        head_gains = jnp.concatenate([
            _tile_heads(na_qn_g[l]) * qk_scale,
            _tile_heads(na_kn_g[l]),
            _tile_heads(mem_qn_g[l]) * qk_scale,
            jnp.zeros((5, GROUP_W), F32)], axis=0)
        proj, c_in = _in_proj(x.reshape(BATCH * SEQ, D_MODEL), norm_g[l].reshape(1, D_MODEL),
                              w_in[l].astype(BF16), head_gains, ones_bd)
        wf_bd = jax.scipy.linalg.block_diag(*[fn_w[l, h] for h in range(N_SUB)]).astype(BF16)
        zc = _fnet(c_in.reshape(BATCH, SEQ, GROUP_W), g1, m2, m3, wf_bd)
        ws_stack = gm_w_s[l].reshape(N_SUB * CHUNK, CHUNK).astype(BF16)
        bs_tab = jnp.repeat(gm_b_s[l].T, HEAD_DIM, axis=1)
        x = _mixer(x, proj.reshape(BATCH, SEQ, MAIN_COLS), zc, km, vm, w_out[l].astype(BF16),
                   gm_ln_g[l].reshape(1, GROUP_W), gm_ln_b[l].reshape(1, GROUP_W),
                   ws_stack, bs_tab, _natten_bias_table(na_rpb[l]))
    return x
```

```python
import functools

import numpy as np
import jax
import jax.numpy as jnp
from jax import lax
from jax.experimental import pallas as pl
from jax.experimental.pallas import tpu as pltpu

D_MODEL = 1024
BATCH = 8
SEQ = 4096
DEPTH = 2
N_MEM = 256
GROUP_W = 256
N_SUB = 4
HEAD_DIM = 64
CHUNK = 128
GRID_W = 64
GRID_H = SEQ // GRID_W
WIN_H = 8
WIN_W = 16
N_IN_SLOTS = 11
IN_COLS = N_IN_SLOTS * GROUP_W
EPS = 1e-6
NEG = -1e30

SLOT_C_IN = 7
N_MAIN_SLOTS = N_IN_SLOTS - 1
MAIN_COLS = N_MAIN_SLOTS * GROUP_W
M_AU, M_AV, M_AG, M_BQ, M_BK, M_BV, M_BG, M_CG, M_DQ, M_DG = range(10)

TM = 512
ROWS_PER_TILE = TM // GRID_W
TM_IN = 1024
IN_PROJ_SPLIT = 1
FFT_G = 8

F32 = jnp.float32
BF16 = jnp.bfloat16
VMEM_LIMIT = 56 * 1024 * 1024


@functools.lru_cache(maxsize=None)
def _dft_tables():
    g = np.arange(GRID_W // FFT_G)
    j = np.arange(FFT_G)
    k = np.arange(GRID_W)
    n = np.arange(GRID_W)
    s = GRID_W * n[None, None, None, :] + FFT_G * g[:, None, None, None] + j[None, :, None, None]
    phase = (k[None, None, :, None] * s) % SEQ
    ang = 2.0 * np.pi * phase.astype(np.float64) / SEQ
    g1 = np.zeros((len(g), 2, FFT_G, GRID_W, GRID_W, FFT_G), np.float64)
    for jj in range(FFT_G):
        g1[:, 0, jj, :, :, jj] = np.cos(ang[:, jj])
        g1[:, 1, jj, :, :, jj] = np.sin(ang[:, jj])
    g1 = g1.reshape(len(g), 2 * FFT_G * GRID_W, GRID_W * FFT_G)
    ang2 = 2.0 * np.pi * ((k[:, None] * n[None, :]) % GRID_W).astype(np.float64) / GRID_W
    c2, s2 = np.cos(ang2), np.sin(ang2)
    m2 = np.zeros((2, GRID_W, FFT_G, 2, GRID_W, FFT_G), np.float64)
    for jj in range(FFT_G):
        m2[0, :, jj, 0, :, jj] = c2
        m2[0, :, jj, 1, :, jj] = -s2
        m2[1, :, jj, 0, :, jj] = s2
        m2[1, :, jj, 1, :, jj] = c2
    m2 = m2.reshape(2 * GRID_W * FFT_G, 2 * GRID_W * FFT_G)
    d = np.arange(HEAD_DIM)
    ang3 = 2.0 * np.pi * ((d[:, None] * d[None, :]) % HEAD_DIM).astype(np.float64) / HEAD_DIM
    scale = 1.0 / np.sqrt(float(SEQ * HEAD_DIM))
    cd = np.kron(np.eye(N_SUB), np.cos(ang3)) * scale
    sd = np.kron(np.eye(N_SUB), np.sin(ang3)) * scale
    m3 = np.concatenate([cd, -sd], axis=0)
    return g1.astype(np.float32), m2.astype(np.float32), m3.astype(np.float32)


@functools.lru_cache(maxsize=None)
def _head_mean_matrix():
    return np.kron(np.eye(N_SUB), np.full((HEAD_DIM, HEAD_DIM), 1.0 / HEAD_DIM)).astype(np.float32)


def _natten_bias_table(rpb):
    c = np.arange(GRID_W)
    col_start = np.clip(c - WIN_W // 2, 0, GRID_W - WIN_W)
    col_ok = (c[None, :] >= col_start[:, None]) & (c[None, :] < col_start[:, None] + WIN_W)
    lo = GRID_W - WIN_W
    v = jnp.pad(rpb.astype(F32), ((0, 0), (0, 0), (lo, 2 * GRID_W - lo - (2 * WIN_W - 1))))
    flat = jnp.tile(v, (1, 1, GRID_W))[..., :GRID_W * (2 * GRID_W - 1)]
    toe = flat.reshape(N_SUB, 2 * WIN_H - 1, GRID_W, 2 * GRID_W - 1)[..., GRID_W - 1:]
    toe = toe.transpose(0, 2, 1, 3)
    bias = jnp.stack([toe[:, :, WIN_H - 1 - p:2 * WIN_H - 1 - p, :] for p in range(WIN_H)], axis=0)
    bias = jnp.where(col_ok[None, None, :, None, :], bias, NEG)
    return bias.reshape(WIN_H, N_SUB * GRID_W, WIN_H * GRID_W)


def _head_rmsnorm(y, gain, ones_bd):
    ms = jnp.dot((y * y).astype(BF16), ones_bd, preferred_element_type=F32)
    return y * lax.rsqrt(ms + EPS) * gain


def _silu(g):
    return g * jax.nn.sigmoid(g)


def _lane_head(rows):
    return lax.broadcasted_iota(jnp.int32, (rows, GROUP_W), 1) // HEAD_DIM


def _stack_heads(q, lane_head):
    qf = q.astype(F32)
    return jnp.concatenate(
        [jnp.where(lane_head == h, qf, 0.0) for h in range(N_SUB)], axis=0).astype(BF16)


def _pick_heads(o, lane_head, rows):
    out = o[0:rows]
    for h in range(1, N_SUB):
        out = jnp.where(lane_head == h, o[h * rows:(h + 1) * rows], out)
    return out


def _softmax_pv(s, v):
    m = jnp.max(s, axis=-1, keepdims=True)
    e = jnp.exp(s - m)
    l = jnp.sum(e, axis=-1, keepdims=True)
    return jnp.dot(e.astype(BF16), v, preferred_element_type=F32) / l


_NT = (((1,), (1,)), ((), ()))


def _mem_kv_kernel(mem_ref, g_ref, w_ref, kg_ref, ones_ref, k_ref, v_ref):
    m = mem_ref[0]
    ms = jnp.mean(m * m, axis=-1, keepdims=True)
    mn = (m * lax.rsqrt(ms + EPS) * g_ref[...]).astype(BF16)
    kv = jnp.dot(mn, w_ref[...], preferred_element_type=F32)
    k = _head_rmsnorm(kv[:, :GROUP_W], kg_ref[...], ones_ref[...])
    k_ref[0] = k.astype(BF16)
    v_ref[0] = kv[:, GROUP_W:].astype(BF16)


def _in_proj_kernel(x_ref, g_ref, w_ref, hg_ref, ones_ref, proj_ref, cin_ref):
    head_gain_row = {3: 0, 4: 1, 9: 2}
    for piece in range(IN_PROJ_SPLIT):
        rows = slice(piece * (TM_IN // IN_PROJ_SPLIT), (piece + 1) * (TM_IN // IN_PROJ_SPLIT))
        x = x_ref[rows, :]
        inv = lax.rsqrt(jnp.mean(x * x, axis=-1, keepdims=True) + EPS)
        h = (x * g_ref[...]).astype(BF16)
        col = 0
        for j in range(N_IN_SLOTS):
            y = jnp.dot(h, w_ref[:, j * GROUP_W:(j + 1) * GROUP_W], preferred_element_type=F32) * inv
            if j in head_gain_row:
                r = head_gain_row[j]
                y = _head_rmsnorm(y, hg_ref[r:r + 1, :], ones_ref[...])
            if j == SLOT_C_IN:
                cin_ref[rows, :] = y
            else:
                proj_ref[rows, col:col + GROUP_W] = y.astype(BF16)
                col += GROUP_W


def _fnet_kernel(x_ref, g1_ref, m2_ref, m3_ref, wf_ref, z_ref, t_ref):
    half = GRID_W * FFT_G

    def stage1(g, carry):
        n0 = pl.multiple_of(g * FFT_G, FFT_G)
        x = x_ref[0, :, pl.ds(n0, FFT_G), :].reshape(half, GROUP_W).astype(BF16)
        y = jnp.dot(g1_ref[g], x, preferred_element_type=F32)
        t_ref[:, pl.ds(n0, FFT_G), :, :] = y.reshape(2, FFT_G, GRID_W, GROUP_W)
        return carry

    def stage2(g, carry):
        k0 = pl.multiple_of(g * FFT_G, FFT_G)
        t = t_ref[:, :, pl.ds(k0, FFT_G), :].reshape(2 * half, GROUP_W).astype(BF16)
        ab = jnp.dot(m2_ref[...], t, preferred_element_type=F32)
        ab = jnp.concatenate([ab[:half], ab[half:]], axis=-1).astype(BF16)
        z = jnp.dot(ab, m3_ref[...], preferred_element_type=F32)
        zc = jnp.dot(z.astype(BF16), wf_ref[...], preferred_element_type=F32)
        z_ref[0, :, pl.ds(k0, FFT_G), :] = zc.reshape(GRID_W, FFT_G, GROUP_W)
        return carry

    n_groups = GRID_W // FFT_G
    lax.fori_loop(0, n_groups, stage1, 0, unroll=2)
    lax.fori_loop(0, n_groups, stage2, 0, unroll=2)


def _mixer_kernel(x_ref, au_ref, av_ref, ag_ref, bq_ref, bk_ref, bv_ref, bg_ref, cg_ref, dq_ref, dg_ref,
                  zc_ref, km_ref, vm_ref, wout_ref, lng_ref, lnb_ref, ws_ref, bs_ref, bias_ref,
                  o_ref, y_ref, yb_ref):
    t = pl.program_id(1)

    lh_chunk = _lane_head(CHUNK)
    for c in range(TM // CHUNK):
        rows = slice(c * CHUNK, (c + 1) * CHUNK)
        v = av_ref[0, rows, :].astype(F32)
        mu = jnp.mean(v, axis=-1, keepdims=True)
        var = jnp.mean(jnp.square(v - mu), axis=-1, keepdims=True)
        vn = ((v - mu) * lax.rsqrt(var + EPS) * lng_ref[...] + lnb_ref[...]).astype(BF16)
        sp = jnp.dot(ws_ref[...], vn, preferred_element_type=F32)
        s = _pick_heads(sp, lh_chunk, CHUNK) + bs_ref[...]
        ya = au_ref[0, rows, :].astype(F32) * s * _silu(ag_ref[0, rows, :].astype(F32))
        y_ref[rows, 0:GROUP_W] = ya.astype(BF16)

    lh_tile = _lane_head(TM)
    qf = dq_ref[0].astype(F32)
    km = km_ref[0]
    vm = vm_ref[0]
    yd = jnp.zeros((TM, GROUP_W), F32)
    for h in range(N_SUB):
        qm = jnp.where(lh_tile == h, qf, 0.0).astype(BF16)
        s = lax.dot_general(qm, km, _NT, preferred_element_type=F32)
        yd = jnp.where(lh_tile == h, _softmax_pv(s, vm), yd)
    y_ref[:, 3 * GROUP_W:4 * GROUP_W] = (yd * _silu(dg_ref[0].astype(F32))).astype(BF16)

    lh_row = _lane_head(GRID_W)
    for i in range(ROWS_PER_TILE):
        r = t * ROWS_PER_TILE + i
        rs = jnp.clip(r - WIN_H // 2, 0, GRID_H - WIN_H)
        k0 = pl.multiple_of(rs * GRID_W, GRID_W)
        qs = _stack_heads(bq_ref[0, i * GRID_W:(i + 1) * GRID_W, :], lh_row)
        kw = bk_ref[0, pl.ds(k0, WIN_H * GRID_W), :]
        vw = bv_ref[0, pl.ds(k0, WIN_H * GRID_W), :]
        s = lax.dot_general(qs, kw, _NT, preferred_element_type=F32) + bias_ref[r - rs]
        o = _softmax_pv(s, vw)
        yb_ref[i * GRID_W:(i + 1) * GRID_W, :] = _pick_heads(o, lh_row, GRID_W)
    y_ref[:, GROUP_W:2 * GROUP_W] = (yb_ref[...] * _silu(bg_ref[0].astype(F32))).astype(BF16)

    y_ref[:, 2 * GROUP_W:3 * GROUP_W] = (zc_ref[0] * _silu(cg_ref[0].astype(F32))).astype(BF16)

    o_ref[0] = x_ref[0] + jnp.dot(y_ref[...], wout_ref[...], preferred_element_type=F32)


def _const_spec(shape):
    nd = len(shape)
    return pl.BlockSpec(shape, lambda *_: (0,) * nd)


def _mem_kv(mem, g, w_kv, kg, ones_bd):
    return pl.pallas_call(
        _mem_kv_kernel,
        grid=(BATCH,),
        in_specs=[
            pl.BlockSpec((1, N_MEM, D_MODEL), lambda b: (b, 0, 0)),
            _const_spec((1, D_MODEL)),
            _const_spec((D_MODEL, 2 * GROUP_W)),
            _const_spec((1, GROUP_W)),
            _const_spec((GROUP_W, GROUP_W)),
        ],
        out_specs=[pl.BlockSpec((1, N_MEM, GROUP_W), lambda b: (b, 0, 0))] * 2,
        out_shape=[jax.ShapeDtypeStruct((BATCH, N_MEM, GROUP_W), BF16)] * 2,
        compiler_params=pltpu.CompilerParams(dimension_semantics=("arbitrary",), vmem_limit_bytes=VMEM_LIMIT),
        name="mem_kv",
    )(mem, g, w_kv, kg, ones_bd)


def _in_proj(x2d, g, w_in, head_gains, ones_bd):
    n_tok = x2d.shape[0]
    return pl.pallas_call(
        _in_proj_kernel,
        grid=(n_tok // TM_IN,),
        in_specs=[
            pl.BlockSpec((TM_IN, D_MODEL), lambda i: (i, 0)),
            _const_spec((1, D_MODEL)),
            _const_spec((D_MODEL, IN_COLS)),
            _const_spec((8, GROUP_W)),
            _const_spec((GROUP_W, GROUP_W)),
        ],
        out_specs=[
            pl.BlockSpec((TM_IN, MAIN_COLS), lambda i: (i, 0)),
            pl.BlockSpec((TM_IN, GROUP_W), lambda i: (i, 0)),
        ],
        out_shape=[
            jax.ShapeDtypeStruct((n_tok, MAIN_COLS), BF16),
            jax.ShapeDtypeStruct((n_tok, GROUP_W), F32),
        ],
        compiler_params=pltpu.CompilerParams(dimension_semantics=("arbitrary",), vmem_limit_bytes=VMEM_LIMIT),
        name="in_proj",
    )(x2d, g, w_in, head_gains, ones_bd)


def _fnet(c_in, g1, m2, m3, wf_bd):
    x4 = c_in.reshape(BATCH, GRID_W, GRID_W, GROUP_W)
    batch_block = pl.BlockSpec((1, GRID_W, GRID_W, GROUP_W), lambda b: (b, 0, 0, 0))
    z4 = pl.pallas_call(
        _fnet_kernel,
        grid=(BATCH,),
        in_specs=[batch_block, _const_spec(g1.shape), _const_spec(m2.shape), _const_spec(m3.shape),
                  _const_spec(wf_bd.shape)],
        out_specs=batch_block,
        out_shape=jax.ShapeDtypeStruct((BATCH, GRID_W, GRID_W, GROUP_W), F32),
        scratch_shapes=[pltpu.VMEM((2, GRID_W, GRID_W, GROUP_W), F32)],
        compiler_params=pltpu.CompilerParams(dimension_semantics=("arbitrary",), vmem_limit_bytes=VMEM_LIMIT),
        name="fnet",
    )(x4, g1, m2, m3, wf_bd)
    return z4.reshape(BATCH, SEQ, GROUP_W)


def _mixer(x, proj, zc, km, vm, w_out, ln_g, ln_b, ws_stack, bs_tab, bias_tab):
    def tile_slot(j):
        return pl.BlockSpec((1, TM, GROUP_W), lambda b, t, j=j: (b, t, j))

    def full_slot(j):
        return pl.BlockSpec((1, SEQ, GROUP_W), lambda b, t, j=j: (b, 0, j))

    in_specs = [
        pl.BlockSpec((1, TM, D_MODEL), lambda b, t: (b, t, 0)),
        tile_slot(M_AU), tile_slot(M_AV), tile_slot(M_AG),
        tile_slot(M_BQ), full_slot(M_BK), full_slot(M_BV), tile_slot(M_BG),
        tile_slot(M_CG), tile_slot(M_DQ), tile_slot(M_DG),
        pl.BlockSpec((1, TM, GROUP_W), lambda b, t: (b, t, 0)),
        pl.BlockSpec((1, N_MEM, GROUP_W), lambda b, t: (b, 0, 0)),
        pl.BlockSpec((1, N_MEM, GROUP_W), lambda b, t: (b, 0, 0)),
        _const_spec((D_MODEL, D_MODEL)),
        _const_spec((1, GROUP_W)), _const_spec((1, GROUP_W)),
        _const_spec((N_SUB * CHUNK, CHUNK)),
        _const_spec((CHUNK, GROUP_W)),
        _const_spec(bias_tab.shape),
    ]
    return pl.pallas_call(
        _mixer_kernel,
        grid=(BATCH, SEQ // TM),
        in_specs=in_specs,
        out_specs=pl.BlockSpec((1, TM, D_MODEL), lambda b, t: (b, t, 0)),
        out_shape=jax.ShapeDtypeStruct((BATCH, SEQ, D_MODEL), F32),
        scratch_shapes=[pltpu.VMEM((TM, D_MODEL), BF16), pltpu.VMEM((TM, GROUP_W), F32)],
        compiler_params=pltpu.CompilerParams(dimension_semantics=("arbitrary", "arbitrary"),
                                             vmem_limit_bytes=VMEM_LIMIT),
        name="mixer",
    )(x, *([proj] * 10), zc, km, vm, w_out, ln_g, ln_b, ws_stack, bs_tab, bias_tab)


def _tile_heads(g):
    return jnp.tile(g.astype(F32), N_SUB).reshape(1, GROUP_W)


def kernel(x, mem, norm_g, w_in, w_out, gm_ln_g, gm_ln_b, gm_w_s, gm_b_s, na_qn_g, na_kn_g, na_rpb, fn_w,
           mem_norm_g, mem_w_kv, mem_qn_g, mem_kn_g):
    g1_np, m2_np, m3_np = _dft_tables()
    g1 = jnp.asarray(g1_np).astype(BF16)
    m2 = jnp.asarray(m2_np).astype(BF16)
    m3 = jnp.asarray(m3_np).astype(BF16)
    ones_bd = jnp.asarray(_head_mean_matrix()).astype(BF16)
    qk_scale = HEAD_DIM ** -0.5

    for l in range(DEPTH):
        km, vm = _mem_kv(mem, mem_norm_g[l].reshape(1, D_MODEL), mem_w_kv[l].astype(BF16),
                         _tile_heads(mem_kn_g[l]), ones_bd)
        head_gains = jnp.concatenate([
            _tile_heads(na_qn_g[l]) * qk_scale,
            _tile_heads(na_kn_g[l]),
            _tile_heads(mem_qn_g[l]) * qk_scale,
            jnp.zeros((5, GROUP_W), F32)], axis=0)
        proj, c_in = _in_proj(x.reshape(BATCH * SEQ, D_MODEL), norm_g[l].reshape(1, D_MODEL),
                              w_in[l].astype(BF16), head_gains, ones_bd)
        wf_bd = jax.scipy.linalg.block_diag(*[fn_w[l, h] for h in range(N_SUB)]).astype(BF16)
        zc = _fnet(c_in.reshape(BATCH, SEQ, GROUP_W), g1, m2, m3, wf_bd)
        ws_stack = gm_w_s[l].reshape(N_SUB * CHUNK, CHUNK).astype(BF16)
        bs_tab = jnp.repeat(gm_b_s[l].T, HEAD_DIM, axis=1)
        x = _mixer(x, proj.reshape(BATCH, SEQ, MAIN_COLS), zc, km, vm, w_out[l].astype(BF16),
                   gm_ln_g[l].reshape(1, GROUP_W), gm_ln_b[l].reshape(1, GROUP_W),
                   ws_stack, bs_tab, _natten_bias_table(na_rpb[l]))
    return x
```

```python
import functools

import numpy as np
import jax
import jax.numpy as jnp
from jax import lax
from jax.experimental import pallas as pl
from jax.experimental.pallas import tpu as pltpu

D_MODEL = 1024
BATCH = 8
SEQ = 4096
DEPTH = 2
N_MEM = 256
GROUP_W = 256
N_SUB = 4
HEAD_DIM = 64
CHUNK = 128
GRID_W = 64
GRID_H = SEQ // GRID_W
WIN_H = 8
WIN_W = 16
N_IN_SLOTS = 11
IN_COLS = N_IN_SLOTS * GROUP_W
EPS = 1e-6
NEG = -1e30

SLOT_C_IN = 7
N_MAIN_SLOTS = N_IN_SLOTS - 1
MAIN_COLS = N_MAIN_SLOTS * GROUP_W
M_AU, M_AV, M_AG, M_BQ, M_BK, M_BV, M_BG, M_CG, M_DQ, M_DG = range(10)

TM = 512
ROWS_PER_TILE = TM // GRID_W
TM_IN = 1024
IN_PROJ_SPLIT = 1
FFT_G = 8

LANES = 128
F32 = jnp.float32
BF16 = jnp.bfloat16
VMEM_LIMIT = 56 * 1024 * 1024


@functools.lru_cache(maxsize=None)
def _dft_tables():
    g = np.arange(GRID_W // FFT_G)
    j = np.arange(FFT_G)
    k = np.arange(GRID_W)
    n = np.arange(GRID_W)
    s = GRID_W * n[None, None, None, :] + FFT_G * g[:, None, None, None] + j[None, :, None, None]
    phase = (k[None, None, :, None] * s) % SEQ
    ang = 2.0 * np.pi * phase.astype(np.float64) / SEQ
    g1 = np.zeros((len(g), 2, GRID_W, FFT_G, GRID_W, FFT_G), np.float64)
    for jj in range(FFT_G):
        g1[:, 0, :, jj, :, jj] = np.cos(ang[:, jj])
        g1[:, 1, :, jj, :, jj] = np.sin(ang[:, jj])
    g1 = g1.reshape(len(g), 2 * GRID_W * FFT_G, GRID_W * FFT_G)
    ang2 = 2.0 * np.pi * ((k[:, None] * n[None, :]) % GRID_W).astype(np.float64) / GRID_W
    c2, s2 = np.cos(ang2), np.sin(ang2)
    m2 = np.block([[c2, -s2], [s2, c2]])
    d = np.arange(HEAD_DIM)
    ang3 = 2.0 * np.pi * ((d[:, None] * d[None, :]) % HEAD_DIM).astype(np.float64) / HEAD_DIM
    scale = 1.0 / np.sqrt(float(SEQ * HEAD_DIM))
    cd = np.kron(np.eye(N_SUB), np.cos(ang3)) * scale
    sd = np.kron(np.eye(N_SUB), np.sin(ang3)) * scale
    m3 = np.concatenate([cd, -sd], axis=0)
    return g1.astype(np.float32), m2.astype(np.float32), m3.astype(np.float32)


@functools.lru_cache(maxsize=None)
def _head_mean_matrix():
    return np.kron(np.eye(N_SUB), np.full((HEAD_DIM, HEAD_DIM), 1.0 / HEAD_DIM)).astype(np.float32)


def _natten_bias_table(rpb):
    n_lh = rpb.shape[0] * rpb.shape[1]
    rpb = rpb.reshape(n_lh, 2 * WIN_H - 1, 2 * WIN_W - 1)
    c = np.arange(GRID_W)
    col_start = np.clip(c - WIN_W // 2, 0, GRID_W - WIN_W)
    col_ok = (c[None, :] >= col_start[:, None]) & (c[None, :] < col_start[:, None] + WIN_W)
    lo = GRID_W - WIN_W
    v = jnp.pad(rpb.astype(F32), ((0, 0), (0, 0), (lo, 2 * GRID_W - lo - (2 * WIN_W - 1))))
    flat = jnp.tile(v, (1, 1, GRID_W))[..., :GRID_W * (2 * GRID_W - 1)]
    toe = flat.reshape(n_lh, 2 * WIN_H - 1, GRID_W, 2 * GRID_W - 1)[..., GRID_W - 1:]
    toe = toe.transpose(0, 2, 1, 3)
    bias = jnp.stack([toe[:, :, WIN_H - 1 - p:2 * WIN_H - 1 - p, :] for p in range(WIN_H)], axis=0)
    bias = jnp.where(col_ok[None, None, :, None, :], bias, NEG)
    return bias.reshape(WIN_H, n_lh * GRID_W, WIN_H * GRID_W)


def _head_rmsnorm(y, gain, ones_bd):
    ms = jnp.dot((y * y).astype(BF16), ones_bd, preferred_element_type=F32)
    return y * lax.rsqrt(ms + EPS) * gain


def _silu(g):
    return g * jax.nn.sigmoid(g)


def _lane_head(rows):
    return lax.broadcasted_iota(jnp.int32, (rows, GROUP_W), 1) // HEAD_DIM


def _stack_heads(q, lane_head):
    qf = q.astype(F32)
    return jnp.concatenate(
        [jnp.where(lane_head == h, qf, 0.0) for h in range(N_SUB)], axis=0).astype(BF16)


def _pick_heads(o, lane_head, rows):
    out = o[0:rows]
    for h in range(1, N_SUB):
        out = jnp.where(lane_head == h, o[h * rows:(h + 1) * rows], out)
    return out


def _softmax_pv(s, v):
    m = jnp.max(s, axis=-1, keepdims=True)
    e = jnp.exp(s - m)
    l = jnp.sum(e, axis=-1, keepdims=True)
    return jnp.dot(e.astype(BF16), v, preferred_element_type=F32) / l


_NT = (((1,), (1,)), ((), ()))


def _mem_kv_kernel(mem_ref, g_ref, w_ref, kg_ref, ones_ref, k_ref, v_ref):
    m = mem_ref[0]
    ms = jnp.mean(m * m, axis=-1, keepdims=True)
    mn = (m * lax.rsqrt(ms + EPS) * g_ref[...]).astype(BF16)
    kv = jnp.dot(mn, w_ref[...], preferred_element_type=F32)
    k = _head_rmsnorm(kv[:, :GROUP_W], kg_ref[...], ones_ref[...])
    k_ref[0] = k.astype(BF16)
    v_ref[0] = kv[:, GROUP_W:].astype(BF16)


def _in_proj_kernel(x_ref, g_ref, w_ref, hg_ref, ones_ref, proj_ref, cin_ref):
    head_gain_row = {3: 0, 4: 1, 9: 2}
    for piece in range(IN_PROJ_SPLIT):
        rows = slice(piece * (TM_IN // IN_PROJ_SPLIT), (piece + 1) * (TM_IN // IN_PROJ_SPLIT))
        x = x_ref[rows, :]
        inv = lax.rsqrt(jnp.mean(x * x, axis=-1, keepdims=True) + EPS)
        h = (x * g_ref[...]).astype(BF16)
        col = 0
        for j in range(N_IN_SLOTS):
            y = jnp.dot(h, w_ref[:, j * GROUP_W:(j + 1) * GROUP_W], preferred_element_type=F32) * inv
            if j in head_gain_row:
                r = head_gain_row[j]
                y = _head_rmsnorm(y, hg_ref[r:r + 1, :], ones_ref[...])
            if j == SLOT_C_IN:
                cin_ref[rows, :] = y
            else:
                proj_ref[rows, col:col + GROUP_W] = y.astype(BF16)
                col += GROUP_W


def _fnet_kernel(x_ref, g1_ref, m2_ref, m3_ref, wf_ref, z_ref, t_ref):
    half = GRID_W * FFT_G

    def stage1(g, carry):
        n0 = pl.multiple_of(g * FFT_G, FFT_G)
        x = x_ref[0, :, pl.ds(n0, FFT_G), :].reshape(half, GROUP_W).astype(BF16)
        y = jnp.dot(g1_ref[g], x, preferred_element_type=F32)
        t_ref[:, :, pl.ds(n0, FFT_G), :] = y.reshape(2, GRID_W, FFT_G, GROUP_W)
        return carry

    def stage2(g, carry):
        k0 = g * FFT_G
        ab = []
        for j in range(FFT_G):
            tj = jnp.concatenate([t_ref[0, k0 + j], t_ref[1, k0 + j]], axis=0).astype(BF16)
            abj = jnp.dot(m2_ref[...], tj, preferred_element_type=F32)
            ab.append(jnp.concatenate([abj[:GRID_W], abj[GRID_W:]], axis=-1))
        ab = jnp.concatenate(ab, axis=0).astype(BF16)
        z = jnp.dot(ab, m3_ref[...], preferred_element_type=F32)
        zc = jnp.dot(z.astype(BF16), wf_ref[...], preferred_element_type=F32)
        zc = zc.reshape(FFT_G, GRID_W, GROUP_W)
        for hf in range(GROUP_W // LANES):
            z_ref[0, hf, pl.ds(k0, FFT_G), :, :] = zc[:, :, hf * LANES:(hf + 1) * LANES]
        return carry

    n_groups = GRID_W // FFT_G
    lax.fori_loop(0, n_groups, stage1, 0, unroll=2)
    lax.fori_loop(0, n_groups, stage2, 0, unroll=2)


def _mixer_kernel(x_ref, au_ref, av_ref, ag_ref, bq_ref, bk_ref, bv_ref, bg_ref, cg_ref, dq_ref, dg_ref,
                  zc_ref, km_ref, vm_ref, wout_ref, lng_ref, lnb_ref, ws_ref, bs_ref, bias_ref,
                  o_ref, y_ref, yb_ref):
    t = pl.program_id(1)

    lh_chunk = _lane_head(CHUNK)
    for c in range(TM // CHUNK):
        rows = slice(c * CHUNK, (c + 1) * CHUNK)
        v = av_ref[0, rows, :].astype(F32)
        mu = jnp.mean(v, axis=-1, keepdims=True)
        var = jnp.mean(jnp.square(v - mu), axis=-1, keepdims=True)
        vn = ((v - mu) * lax.rsqrt(var + EPS) * lng_ref[...] + lnb_ref[...]).astype(BF16)
        sp = jnp.dot(ws_ref[...], vn, preferred_element_type=F32)
        s = _pick_heads(sp, lh_chunk, CHUNK) + bs_ref[...]
        ya = au_ref[0, rows, :].astype(F32) * s * _silu(ag_ref[0, rows, :].astype(F32))
        y_ref[rows, 0:GROUP_W] = ya.astype(BF16)

    lh_tile = _lane_head(TM)
    qf = dq_ref[0].astype(F32)
    km = km_ref[0]
    vm = vm_ref[0]
    yd = jnp.zeros((TM, GROUP_W), F32)
    for h in range(N_SUB):
        qm = jnp.where(lh_tile == h, qf, 0.0).astype(BF16)
        s = lax.dot_general(qm, km, _NT, preferred_element_type=F32)
        yd = jnp.where(lh_tile == h, _softmax_pv(s, vm), yd)
    y_ref[:, 3 * GROUP_W:4 * GROUP_W] = (yd * _silu(dg_ref[0].astype(F32))).astype(BF16)

    lh_row = _lane_head(GRID_W)
    for i in range(ROWS_PER_TILE):
        r = t * ROWS_PER_TILE + i
        rs = jnp.clip(r - WIN_H // 2, 0, GRID_H - WIN_H)
        k0 = pl.multiple_of(rs * GRID_W, GRID_W)
        qs = _stack_heads(bq_ref[0, i * GRID_W:(i + 1) * GRID_W, :], lh_row)
        kw = bk_ref[0, pl.ds(k0, WIN_H * GRID_W), :]
        vw = bv_ref[0, pl.ds(k0, WIN_H * GRID_W), :]
        s = lax.dot_general(qs, kw, _NT, preferred_element_type=F32) + bias_ref[r - rs]
        o = _softmax_pv(s, vw)
        yb_ref[i * GRID_W:(i + 1) * GRID_W, :] = _pick_heads(o, lh_row, GRID_W)
    y_ref[:, GROUP_W:2 * GROUP_W] = (yb_ref[...] * _silu(bg_ref[0].astype(F32))).astype(BF16)

    zc_halves = [zc_ref.at[0, hf].reshape(GRID_W * ROWS_PER_TILE, LANES) for hf in range(GROUP_W // LANES)]
    zc = jnp.concatenate(
        [jnp.concatenate([zh[pl.ds(i, GRID_W, stride=ROWS_PER_TILE), :] for zh in zc_halves], axis=-1)
         for i in range(ROWS_PER_TILE)], axis=0)
    y_ref[:, 2 * GROUP_W:3 * GROUP_W] = (zc * _silu(cg_ref[0].astype(F32))).astype(BF16)

    o_ref[0] = x_ref[0] + jnp.dot(y_ref[...], wout_ref[...], preferred_element_type=F32)


def _const_spec(shape):
    nd = len(shape)
    return pl.BlockSpec(shape, lambda *_: (0,) * nd)


def _layer_spec(shape, layer):
    nd = len(shape)
    return pl.BlockSpec((None,) + tuple(shape), lambda *_: (layer,) + (0,) * nd)


def _params(dims):
    return pltpu.CompilerParams(dimension_semantics=("arbitrary",) * dims, vmem_limit_bytes=VMEM_LIMIT)


def _mem_kv(mem, g, w_kv, kg, ones_bd):
    return pl.pallas_call(
        _mem_kv_kernel,
        grid=(DEPTH, BATCH),
        in_specs=[
            pl.BlockSpec((1, N_MEM, D_MODEL), lambda l, b: (b, 0, 0)),
            pl.BlockSpec((None, 1, D_MODEL), lambda l, b: (l, 0, 0)),
            pl.BlockSpec((None, D_MODEL, 2 * GROUP_W), lambda l, b: (l, 0, 0)),
            pl.BlockSpec((None, 1, GROUP_W), lambda l, b: (l, 0, 0)),
            _const_spec((GROUP_W, GROUP_W)),
        ],
        out_specs=[pl.BlockSpec((None, 1, N_MEM, GROUP_W), lambda l, b: (l, b, 0, 0))] * 2,
        out_shape=[jax.ShapeDtypeStruct((DEPTH, BATCH, N_MEM, GROUP_W), BF16)] * 2,
        compiler_params=_params(2),
        name="mem_kv",
    )(mem, g, w_kv, kg, ones_bd)


def _in_proj(layer, x2d, g, w_in, head_gains, ones_bd):
    n_tok = x2d.shape[0]
    return pl.pallas_call(
        _in_proj_kernel,
        grid=(n_tok // TM_IN,),
        in_specs=[
            pl.BlockSpec((TM_IN, D_MODEL), lambda i: (i, 0)),
            _layer_spec((1, D_MODEL), layer),
            _layer_spec((D_MODEL, IN_COLS), layer),
            _layer_spec((8, GROUP_W), layer),
            _const_spec((GROUP_W, GROUP_W)),
        ],
        out_specs=[
            pl.BlockSpec((TM_IN, MAIN_COLS), lambda i: (i, 0)),
            pl.BlockSpec((TM_IN, GROUP_W), lambda i: (i, 0)),
        ],
        out_shape=[
            jax.ShapeDtypeStruct((n_tok, MAIN_COLS), BF16),
            jax.ShapeDtypeStruct((n_tok, GROUP_W), F32),
        ],
        compiler_params=_params(1),
        name="in_proj",
    )(x2d, g, w_in, head_gains, ones_bd)


def _fnet(layer, c_in, g1, m2, m3, wf_bd):
    x4 = c_in.reshape(BATCH, GRID_W, GRID_W, GROUP_W)
    n_half = GROUP_W // LANES
    return pl.pallas_call(
        _fnet_kernel,
        grid=(BATCH,),
        in_specs=[pl.BlockSpec((1, GRID_W, GRID_W, GROUP_W), lambda b: (b, 0, 0, 0)),
                  _const_spec(g1.shape), _const_spec(m2.shape), _const_spec(m3.shape),
                  _layer_spec((GROUP_W, GROUP_W), layer)],
        out_specs=pl.BlockSpec((1, n_half, GRID_W, GRID_W, LANES), lambda b: (b, 0, 0, 0, 0)),
        out_shape=jax.ShapeDtypeStruct((BATCH, n_half, GRID_W, GRID_W, LANES), F32),
        scratch_shapes=[pltpu.VMEM((2, GRID_W, GRID_W, GROUP_W), F32)],
        compiler_params=_params(1),
        name="fnet",
    )(x4, g1, m2, m3, wf_bd)


def _mixer(layer, x, proj, zc, km, vm, w_out, ln_g, ln_b, ws_stack, bs_tab, bias_tab):
    def tile_slot(j):
        return pl.BlockSpec((1, TM, GROUP_W), lambda b, t, j=j: (b, t, j))

    def full_slot(j):
        return pl.BlockSpec((1, SEQ, GROUP_W), lambda b, t, j=j: (b, 0, j))

    mem_spec = pl.BlockSpec((None, 1, N_MEM, GROUP_W), lambda b, t: (layer, b, 0, 0))
    in_specs = [
        pl.BlockSpec((1, TM, D_MODEL), lambda b, t: (b, t, 0)),
        tile_slot(M_AU), tile_slot(M_AV), tile_slot(M_AG),
        tile_slot(M_BQ), full_slot(M_BK), full_slot(M_BV), tile_slot(M_BG),
        tile_slot(M_CG), tile_slot(M_DQ), tile_slot(M_DG),
        pl.BlockSpec((1, GROUP_W // LANES, GRID_W, ROWS_PER_TILE, LANES), lambda b, t: (b, 0, 0, t, 0)),
        mem_spec, mem_spec,
        _layer_spec((D_MODEL, D_MODEL), layer),
        _layer_spec((1, GROUP_W), layer), _layer_spec((1, GROUP_W), layer),
        _layer_spec((N_SUB * CHUNK, CHUNK), layer),
        _layer_spec((CHUNK, GROUP_W), layer),
        pl.BlockSpec((WIN_H, N_SUB * GRID_W, WIN_H * GRID_W), lambda b, t: (0, layer, 0)),
    ]
    return pl.pallas_call(
        _mixer_kernel,
        grid=(BATCH, SEQ // TM),
        in_specs=in_specs,
        out_specs=pl.BlockSpec((1, TM, D_MODEL), lambda b, t: (b, t, 0)),
        out_shape=jax.ShapeDtypeStruct((BATCH, SEQ, D_MODEL), F32),
        scratch_shapes=[pltpu.VMEM((TM, D_MODEL), BF16), pltpu.VMEM((TM, GROUP_W), F32)],
        compiler_params=_params(2),
        name="mixer",
    )(x, *([proj] * 10), zc, km, vm, w_out, ln_g, ln_b, ws_stack, bs_tab, bias_tab)


def _tile_heads(g):
    return jnp.tile(g.astype(F32), (1, N_SUB))


def kernel(x, mem, norm_g, w_in, w_out, gm_ln_g, gm_ln_b, gm_w_s, gm_b_s, na_qn_g, na_kn_g, na_rpb, fn_w,
           mem_norm_g, mem_w_kv, mem_qn_g, mem_kn_g):
    g1_np, m2_np, m3_np = _dft_tables()
    g1 = jnp.asarray(g1_np).astype(BF16)
    m2 = jnp.asarray(m2_np).astype(BF16)
    m3 = jnp.asarray(m3_np).astype(BF16)
    ones_bd = jnp.asarray(_head_mean_matrix()).astype(BF16)
    qk_scale = HEAD_DIM ** -0.5

    w_in_b = w_in.astype(BF16)
    w_out_b = w_out.astype(BF16)
    norm_g3 = norm_g.reshape(DEPTH, 1, D_MODEL)
    head_gains = jnp.stack([_tile_heads(na_qn_g) * qk_scale, _tile_heads(na_kn_g),
                            _tile_heads(mem_qn_g) * qk_scale], axis=1)
    head_gains = jnp.pad(head_gains, ((0, 0), (0, 5), (0, 0)))
    eye = jnp.eye(N_SUB, dtype=fn_w.dtype)
    wf_bd = (fn_w[:, :, :, None, :] * eye[None, :, None, :, None]).reshape(DEPTH, GROUP_W, GROUP_W).astype(BF16)
    ws_stack = gm_w_s.reshape(DEPTH, N_SUB * CHUNK, CHUNK).astype(BF16)
    bs_tab = jnp.repeat(gm_b_s.transpose(0, 2, 1), HEAD_DIM, axis=2)
    ln_g3 = gm_ln_g.reshape(DEPTH, 1, GROUP_W)
    ln_b3 = gm_ln_b.reshape(DEPTH, 1, GROUP_W)
    bias_tab = _natten_bias_table(na_rpb)

    km, vm = _mem_kv(mem, mem_norm_g.reshape(DEPTH, 1, D_MODEL), mem_w_kv.astype(BF16),
                     _tile_heads(mem_kn_g).reshape(DEPTH, 1, GROUP_W), ones_bd)
    for l in range(DEPTH):
        proj, c_in = _in_proj(l, x.reshape(BATCH * SEQ, D_MODEL), norm_g3, w_in_b, head_gains, ones_bd)
        zc = _fnet(l, c_in, g1, m2, m3, wf_bd)
        x = _mixer(l, x, proj.reshape(BATCH, SEQ, MAIN_COLS), zc, km, vm, w_out_b,
                   ln_g3, ln_b3, ws_stack, bs_tab, bias_tab)
    return x
```

```python
import functools

import numpy as np
import jax
import jax.numpy as jnp
from jax import lax
from jax.experimental import pallas as pl
from jax.experimental.pallas import tpu as pltpu

D_MODEL = 1024
BATCH = 8
SEQ = 4096
DEPTH = 2
N_MEM = 256
GROUP_W = 256
N_SUB = 4
HEAD_DIM = 64
CHUNK = 128
GRID_W = 64
GRID_H = SEQ // GRID_W
WIN_H = 8
WIN_W = 16
N_IN_SLOTS = 11
IN_COLS = N_IN_SLOTS * GROUP_W
EPS = 1e-6
NEG = -1e30
LOG2E = 1.4426950408889634

SLOT_C_IN = 7
N_MAIN_SLOTS = N_IN_SLOTS - 1
MAIN_COLS = N_MAIN_SLOTS * GROUP_W
M_AU, M_AV, M_AG, M_BQ, M_BK, M_BV, M_BG, M_CG, M_DQ, M_DG = range(10)

TM = 512
ROWS_PER_TILE = TM // GRID_W
TM_IN = 1024
IN_PROJ_SPLIT = 1
FFT_G = 8

LANES = 128
F32 = jnp.float32
BF16 = jnp.bfloat16
VMEM_LIMIT = 56 * 1024 * 1024


@functools.lru_cache(maxsize=None)
def _dft_tables():
    g = np.arange(GRID_W // FFT_G)
    j = np.arange(FFT_G)
    k = np.arange(GRID_W)
    n = np.arange(GRID_W)
    s = GRID_W * n[None, None, None, :] + FFT_G * g[:, None, None, None] + j[None, :, None, None]
    phase = (k[None, None, :, None] * s) % SEQ
    ang = 2.0 * np.pi * phase.astype(np.float64) / SEQ
    g1 = np.zeros((len(g), 2, GRID_W, FFT_G, GRID_W, FFT_G), np.float64)
    for jj in range(FFT_G):
        g1[:, 0, :, jj, :, jj] = np.cos(ang[:, jj])
        g1[:, 1, :, jj, :, jj] = np.sin(ang[:, jj])
    g1 = g1.reshape(len(g), 2 * GRID_W * FFT_G, GRID_W * FFT_G)
    ang2 = 2.0 * np.pi * ((k[:, None] * n[None, :]) % GRID_W).astype(np.float64) / GRID_W
    c2, s2 = np.cos(ang2), np.sin(ang2)
    m2 = np.block([[c2, -s2], [s2, c2]])
    d = np.arange(HEAD_DIM)
    ang3 = 2.0 * np.pi * ((d[:, None] * d[None, :]) % HEAD_DIM).astype(np.float64) / HEAD_DIM
    scale = 1.0 / np.sqrt(float(SEQ * HEAD_DIM))
    cd = np.kron(np.eye(N_SUB), np.cos(ang3)) * scale
    sd = np.kron(np.eye(N_SUB), np.sin(ang3)) * scale
    m3 = np.concatenate([cd, -sd], axis=0)
    return g1.astype(np.float32), m2.astype(np.float32), m3.astype(np.float32)


@functools.lru_cache(maxsize=None)
def _head_mean_matrix():
    return np.kron(np.eye(N_SUB), np.full((HEAD_DIM, HEAD_DIM), 1.0 / HEAD_DIM)).astype(np.float32)


def _natten_bias_table(rpb):
    n_lh = rpb.shape[0] * rpb.shape[1]
    rpb = rpb.reshape(n_lh, 2 * WIN_H - 1, 2 * WIN_W - 1)
    c = np.arange(GRID_W)
    col_start = np.clip(c - WIN_W // 2, 0, GRID_W - WIN_W)
    col_ok = (c[None, :] >= col_start[:, None]) & (c[None, :] < col_start[:, None] + WIN_W)
    lo = GRID_W - WIN_W
    v = jnp.pad(rpb.astype(F32), ((0, 0), (0, 0), (lo, 2 * GRID_W - lo - (2 * WIN_W - 1))))
    flat = jnp.tile(v, (1, 1, GRID_W))[..., :GRID_W * (2 * GRID_W - 1)]
    toe = flat.reshape(n_lh, 2 * WIN_H - 1, GRID_W, 2 * GRID_W - 1)[..., GRID_W - 1:]
    toe = toe.transpose(0, 2, 1, 3)
    bias = jnp.stack([toe[:, :, WIN_H - 1 - p:2 * WIN_H - 1 - p, :] for p in range(WIN_H)], axis=0)
    bias = jnp.where(col_ok[None, None, :, None, :], bias, NEG)
    return bias.reshape(WIN_H, n_lh * GRID_W, WIN_H * GRID_W)


def _head_rmsnorm(y, gain, ones_bd):
    ms = jnp.dot((y * y).astype(BF16), ones_bd, preferred_element_type=F32)
    return y * lax.rsqrt(ms + EPS) * gain


def _silu(g):
    h = 0.5 * g
    return h + h * jnp.tanh(h)


def _lane_head(rows):
    return lax.broadcasted_iota(jnp.int32, (rows, GROUP_W), 1) // HEAD_DIM


def _stack_heads(q, lane_head):
    qf = q.astype(F32)
    return jnp.concatenate(
        [jnp.where(lane_head == h, qf, 0.0) for h in range(N_SUB)], axis=0).astype(BF16)


def _pick_heads(o, lane_head, rows):
    out = o[0:rows]
    for h in range(1, N_SUB):
        out = jnp.where(lane_head == h, o[h * rows:(h + 1) * rows], out)
    return out


def _softmax_pv(s, v):
    m = jnp.max(s, axis=-1, keepdims=True)
    e = jnp.exp2(s - m)
    l = jnp.sum(e, axis=-1, keepdims=True)
    return jnp.dot(e.astype(BF16), v, preferred_element_type=F32) / l


_NT = (((1,), (1,)), ((), ()))


def _mem_kv_kernel(mem_ref, g_ref, w_ref, kg_ref, ones_ref, k_ref, v_ref):
    m = mem_ref[0]
    ms = jnp.mean(m * m, axis=-1, keepdims=True)
    mn = (m * lax.rsqrt(ms + EPS) * g_ref[...]).astype(BF16)
    kv = jnp.dot(mn, w_ref[...], preferred_element_type=F32)
    k = _head_rmsnorm(kv[:, :GROUP_W], kg_ref[...], ones_ref[...])
    k_ref[0] = k.astype(BF16)
    v_ref[0] = kv[:, GROUP_W:].astype(BF16)


def _in_proj_kernel(x_ref, g_ref, w_ref, hg_ref, ones_ref, proj_ref, cin_ref):
    head_gain_row = {3: 0, 4: 1, 9: 2}
    for piece in range(IN_PROJ_SPLIT):
        rows = slice(piece * (TM_IN // IN_PROJ_SPLIT), (piece + 1) * (TM_IN // IN_PROJ_SPLIT))
        x = x_ref[rows, :]
        inv = lax.rsqrt(jnp.mean(x * x, axis=-1, keepdims=True) + EPS)
        h = (x * g_ref[...]).astype(BF16)
        y_all = jnp.dot(h, w_ref[...], preferred_element_type=F32)
        col = 0
        for j in range(N_IN_SLOTS):
            y = y_all[:, j * GROUP_W:(j + 1) * GROUP_W] * inv
            if j in head_gain_row:
                r = head_gain_row[j]
                y = _head_rmsnorm(y, hg_ref[r:r + 1, :], ones_ref[...])
            if j == SLOT_C_IN:
                cin_ref[rows, :] = y
            else:
                proj_ref[rows, col:col + GROUP_W] = y.astype(BF16)
                col += GROUP_W


def _fnet_kernel(x_ref, g1_ref, m2_ref, m3_ref, wf_ref, z_ref, t_ref):
    half = GRID_W * FFT_G

    def stage1(g, carry):
        n0 = pl.multiple_of(g * FFT_G, FFT_G)
        x = x_ref[0, :, pl.ds(n0, FFT_G), :].reshape(half, GROUP_W).astype(BF16)
        y = jnp.dot(g1_ref[g], x, preferred_element_type=F32)
        t_ref[:, :, pl.ds(n0, FFT_G), :] = y.reshape(2, GRID_W, FFT_G, GROUP_W)
        return carry

    def stage2(g, carry):
        k0 = g * FFT_G
        ab = []
        for j in range(FFT_G):
            tj = jnp.concatenate([t_ref[0, k0 + j], t_ref[1, k0 + j]], axis=0).astype(BF16)
            abj = jnp.dot(m2_ref[...], tj, preferred_element_type=F32)
            ab.append(jnp.concatenate([abj[:GRID_W], abj[GRID_W:]], axis=-1))
        ab = jnp.concatenate(ab, axis=0).astype(BF16)
        z = jnp.dot(ab, m3_ref[...], preferred_element_type=F32)
        zc = jnp.dot(z.astype(BF16), wf_ref[...], preferred_element_type=F32)
        zc = zc.reshape(FFT_G, GRID_W, GROUP_W)
        for hf in range(GROUP_W // LANES):
            z_ref[0, hf, pl.ds(k0, FFT_G), :, :] = zc[:, :, hf * LANES:(hf + 1) * LANES]
        return carry

    n_groups = GRID_W // FFT_G
    lax.fori_loop(0, n_groups, stage1, 0, unroll=2)
    lax.fori_loop(0, n_groups, stage2, 0, unroll=2)


def _mixer_kernel(x_ref, au_ref, av_ref, ag_ref, bq_ref, bk_ref, bv_ref, bg_ref, cg_ref, dq_ref, dg_ref,
                  zc_ref, km_ref, vm_ref, wout_ref, lng_ref, lnb_ref, ws_ref, bs_ref, bias_ref,
                  o_ref, y_ref, yb_ref):
    t = pl.program_id(1)

    lh_tile = _lane_head(TM)
    qs = _stack_heads(dq_ref[0], lh_tile)
    s = lax.dot_general(qs, km_ref[0], _NT, preferred_element_type=F32)
    yd = _pick_heads(_softmax_pv(s, vm_ref[0]), lh_tile, TM)
    y_ref[:, 3 * GROUP_W:4 * GROUP_W] = (yd * _silu(dg_ref[0].astype(F32))).astype(BF16)

    lh_row = _lane_head(GRID_W)
    for i in range(ROWS_PER_TILE):
        r = t * ROWS_PER_TILE + i
        rs = jnp.clip(r - WIN_H // 2, 0, GRID_H - WIN_H)
        k0 = pl.multiple_of(rs * GRID_W, GRID_W)
        qs = _stack_heads(bq_ref[0, i * GRID_W:(i + 1) * GRID_W, :], lh_row)
        kw = bk_ref[0, pl.ds(k0, WIN_H * GRID_W), :]
        vw = bv_ref[0, pl.ds(k0, WIN_H * GRID_W), :]
        s = lax.dot_general(qs, kw, _NT, preferred_element_type=F32) + bias_ref[r - rs]
        o = _softmax_pv(s, vw)
        yb_ref[i * GRID_W:(i + 1) * GRID_W, :] = _pick_heads(o, lh_row, GRID_W)
    y_ref[:, GROUP_W:2 * GROUP_W] = (yb_ref[...] * _silu(bg_ref[0].astype(F32))).astype(BF16)

    lh_chunk = _lane_head(CHUNK)
    for c in range(TM // CHUNK):
        rows = slice(c * CHUNK, (c + 1) * CHUNK)
        v = av_ref[0, rows, :].astype(F32)
        mu = jnp.mean(v, axis=-1, keepdims=True)
        var = jnp.mean(jnp.square(v - mu), axis=-1, keepdims=True)
        vn = ((v - mu) * lax.rsqrt(var + EPS) * lng_ref[...] + lnb_ref[...]).astype(BF16)
        sp = jnp.dot(ws_ref[...], vn, preferred_element_type=F32)
        s = _pick_heads(sp, lh_chunk, CHUNK) + bs_ref[...]
        ya = au_ref[0, rows, :].astype(F32) * s * _silu(ag_ref[0, rows, :].astype(F32))
        y_ref[rows, 0:GROUP_W] = ya.astype(BF16)

    zc_halves = [zc_ref.at[0, hf].reshape(GRID_W * ROWS_PER_TILE, LANES) for hf in range(GROUP_W // LANES)]
    zc = jnp.concatenate(
        [jnp.concatenate([zh[pl.ds(i, GRID_W, stride=ROWS_PER_TILE), :] for zh in zc_halves], axis=-1)
         for i in range(ROWS_PER_TILE)], axis=0)
    y_ref[:, 2 * GROUP_W:3 * GROUP_W] = (zc * _silu(cg_ref[0].astype(F32))).astype(BF16)

    o_ref[0] = x_ref[0] + jnp.dot(y_ref[...], wout_ref[...], preferred_element_type=F32)


def _const_spec(shape):
    nd = len(shape)
    return pl.BlockSpec(shape, lambda *_: (0,) * nd)


def _layer_spec(shape, layer):
    nd = len(shape)
    return pl.BlockSpec((None,) + tuple(shape), lambda *_: (layer,) + (0,) * nd)


def _params(dims):
    return pltpu.CompilerParams(dimension_semantics=("arbitrary",) * dims, vmem_limit_bytes=VMEM_LIMIT)


def _mem_kv(mem, g, w_kv, kg, ones_bd):
    return pl.pallas_call(
        _mem_kv_kernel,
        grid=(DEPTH, BATCH),
        in_specs=[
            pl.BlockSpec((1, N_MEM, D_MODEL), lambda l, b: (b, 0, 0)),
            pl.BlockSpec((None, 1, D_MODEL), lambda l, b: (l, 0, 0)),
            pl.BlockSpec((None, D_MODEL, 2 * GROUP_W), lambda l, b: (l, 0, 0)),
            pl.BlockSpec((None, 1, GROUP_W), lambda l, b: (l, 0, 0)),
            _const_spec((GROUP_W, GROUP_W)),
        ],
        out_specs=[pl.BlockSpec((None, 1, N_MEM, GROUP_W), lambda l, b: (l, b, 0, 0))] * 2,
        out_shape=[jax.ShapeDtypeStruct((DEPTH, BATCH, N_MEM, GROUP_W), BF16)] * 2,
        compiler_params=_params(2),
        name="mem_kv",
    )(mem, g, w_kv, kg, ones_bd)


def _in_proj(layer, x2d, g, w_in, head_gains, ones_bd):
    n_tok = x2d.shape[0]
    return pl.pallas_call(
        _in_proj_kernel,
        grid=(n_tok // TM_IN,),
        in_specs=[
            pl.BlockSpec((TM_IN, D_MODEL), lambda i: (i, 0)),
            _layer_spec((1, D_MODEL), layer),
            _layer_spec((D_MODEL, IN_COLS), layer),
            _layer_spec((8, GROUP_W), layer),
            _const_spec((GROUP_W, GROUP_W)),
        ],
        out_specs=[
            pl.BlockSpec((TM_IN, MAIN_COLS), lambda i: (i, 0)),
            pl.BlockSpec((TM_IN, GROUP_W), lambda i: (i, 0)),
        ],
        out_shape=[
            jax.ShapeDtypeStruct((n_tok, MAIN_COLS), BF16),
            jax.ShapeDtypeStruct((n_tok, GROUP_W), F32),
        ],
        compiler_params=_params(1),
        name="in_proj",
    )(x2d, g, w_in, head_gains, ones_bd)


def _fnet(layer, c_in, g1, m2, m3, wf_bd):
    x4 = c_in.reshape(BATCH, GRID_W, GRID_W, GROUP_W)
    n_half = GROUP_W // LANES
    return pl.pallas_call(
        _fnet_kernel,
        grid=(BATCH,),
        in_specs=[pl.BlockSpec((1, GRID_W, GRID_W, GROUP_W), lambda b: (b, 0, 0, 0)),
                  _const_spec(g1.shape), _const_spec(m2.shape), _const_spec(m3.shape),
                  _layer_spec((GROUP_W, GROUP_W), layer)],
        out_specs=pl.BlockSpec((1, n_half, GRID_W, GRID_W, LANES), lambda b: (b, 0, 0, 0, 0)),
        out_shape=jax.ShapeDtypeStruct((BATCH, n_half, GRID_W, GRID_W, LANES), F32),
        scratch_shapes=[pltpu.VMEM((2, GRID_W, GRID_W, GROUP_W), F32)],
        compiler_params=_params(1),
        name="fnet",
    )(x4, g1, m2, m3, wf_bd)


def _mixer(layer, x, proj, zc, km, vm, w_out, ln_g, ln_b, ws_stack, bs_tab, bias_tab):
    def tile_slot(j):
        return pl.BlockSpec((1, TM, GROUP_W), lambda b, t, j=j: (b, t, j))

    def full_slot(j):
        return pl.BlockSpec((1, SEQ, GROUP_W), lambda b, t, j=j: (b, 0, j))

    mem_spec = pl.BlockSpec((None, 1, N_MEM, GROUP_W), lambda b, t: (layer, b, 0, 0))
    in_specs = [
        pl.BlockSpec((1, TM, D_MODEL), lambda b, t: (b, t, 0)),
        tile_slot(M_AU), tile_slot(M_AV), tile_slot(M_AG),
        tile_slot(M_BQ), full_slot(M_BK), full_slot(M_BV), tile_slot(M_BG),
        tile_slot(M_CG), tile_slot(M_DQ), tile_slot(M_DG),
        pl.BlockSpec((1, GROUP_W // LANES, GRID_W, ROWS_PER_TILE, LANES), lambda b, t: (b, 0, 0, t, 0)),
        mem_spec, mem_spec,
        _layer_spec((D_MODEL, D_MODEL), layer),
        _layer_spec((1, GROUP_W), layer), _layer_spec((1, GROUP_W), layer),
        _layer_spec((N_SUB * CHUNK, CHUNK), layer),
        _layer_spec((CHUNK, GROUP_W), layer),
        pl.BlockSpec((WIN_H, N_SUB * GRID_W, WIN_H * GRID_W), lambda b, t: (0, layer, 0)),
    ]
    return pl.pallas_call(
        _mixer_kernel,
        grid=(BATCH, SEQ // TM),
        in_specs=in_specs,
        out_specs=pl.BlockSpec((1, TM, D_MODEL), lambda b, t: (b, t, 0)),
        out_shape=jax.ShapeDtypeStruct((BATCH, SEQ, D_MODEL), F32),
        scratch_shapes=[pltpu.VMEM((TM, D_MODEL), BF16), pltpu.VMEM((TM, GROUP_W), F32)],
        compiler_params=_params(2),
        name="mixer",
    )(x, *([proj] * 10), zc, km, vm, w_out, ln_g, ln_b, ws_stack, bs_tab, bias_tab)


def _tile_heads(g):
    return jnp.tile(g.astype(F32), (1, N_SUB))


def kernel(x, mem, norm_g, w_in, w_out, gm_ln_g, gm_ln_b, gm_w_s, gm_b_s, na_qn_g, na_kn_g, na_rpb, fn_w,
           mem_norm_g, mem_w_kv, mem_qn_g, mem_kn_g):
    g1_np, m2_np, m3_np = _dft_tables()
    g1 = jnp.asarray(g1_np).astype(BF16)
    m2 = jnp.asarray(m2_np).astype(BF16)
    m3 = jnp.asarray(m3_np).astype(BF16)
    ones_bd = jnp.asarray(_head_mean_matrix()).astype(BF16)
    qk_scale = HEAD_DIM ** -0.5 * LOG2E

    w_in_b = w_in.astype(BF16)
    w_out_b = w_out.astype(BF16)
    norm_g3 = norm_g.reshape(DEPTH, 1, D_MODEL)
    head_gains = jnp.stack([_tile_heads(na_qn_g) * qk_scale, _tile_heads(na_kn_g),
                            _tile_heads(mem_qn_g) * qk_scale], axis=1)
    head_gains = jnp.pad(head_gains, ((0, 0), (0, 5), (0, 0)))
    eye = jnp.eye(N_SUB, dtype=fn_w.dtype)
    wf_bd = (fn_w[:, :, :, None, :] * eye[None, :, None, :, None]).reshape(DEPTH, GROUP_W, GROUP_W).astype(BF16)
    ws_stack = gm_w_s.reshape(DEPTH, N_SUB * CHUNK, CHUNK).astype(BF16)
    bs_tab = jnp.repeat(gm_b_s.transpose(0, 2, 1), HEAD_DIM, axis=2)
    ln_g3 = gm_ln_g.reshape(DEPTH, 1, GROUP_W)
    ln_b3 = gm_ln_b.reshape(DEPTH, 1, GROUP_W)
    bias_tab = _natten_bias_table(na_rpb * LOG2E)

    km, vm = _mem_kv(mem, mem_norm_g.reshape(DEPTH, 1, D_MODEL), mem_w_kv.astype(BF16),
                     _tile_heads(mem_kn_g).reshape(DEPTH, 1, GROUP_W), ones_bd)
    for l in range(DEPTH):
        proj, c_in = _in_proj(l, x.reshape(BATCH * SEQ, D_MODEL), norm_g3, w_in_b, head_gains, ones_bd)
        zc = _fnet(l, c_in, g1, m2, m3, wf_bd)
        x = _mixer(l, x, proj.reshape(BATCH, SEQ, MAIN_COLS), zc, km, vm, w_out_b,
                   ln_g3, ln_b3, ws_stack, bs_tab, bias_tab)
    return x
```

```python
import functools

import numpy as np
import jax
import jax.numpy as jnp
from jax import lax
from jax.experimental import pallas as pl
from jax.experimental.pallas import tpu as pltpu

D_MODEL = 1024
BATCH = 8
SEQ = 4096
DEPTH = 2
N_MEM = 256
GROUP_W = 256
N_SUB = 4
HEAD_DIM = 64
CHUNK = 128
GRID_W = 64
GRID_H = SEQ // GRID_W
WIN_H = 8
WIN_W = 16
N_IN_SLOTS = 11
IN_COLS = N_IN_SLOTS * GROUP_W
EPS = 1e-6
NEG = -1e30
LOG2E = 1.4426950408889634

SLOT_C_IN = 7
N_MAIN_SLOTS = N_IN_SLOTS - 1
MAIN_COLS = N_MAIN_SLOTS * GROUP_W
M_AU, M_AV, M_AG, M_BQ, M_BK, M_BV, M_BG, M_CG, M_DQ, M_DG = range(10)

TM = 512
ROWS_PER_TILE = TM // GRID_W
TM_IN = 1024
IN_PROJ_SPLIT = 1
FFT_G = 8

LANES = 128
F32 = jnp.float32
BF16 = jnp.bfloat16
VMEM_LIMIT = 56 * 1024 * 1024


@functools.lru_cache(maxsize=None)
def _dft_tables():
    g = np.arange(GRID_W // FFT_G)
    j = np.arange(FFT_G)
    k = np.arange(GRID_W)
    n = np.arange(GRID_W)
    s = GRID_W * n[None, None, None, :] + FFT_G * g[:, None, None, None] + j[None, :, None, None]
    phase = (k[None, None, :, None] * s) % SEQ
    ang = 2.0 * np.pi * phase.astype(np.float64) / SEQ
    g1 = np.zeros((len(g), 2, GRID_W, FFT_G, GRID_W, FFT_G), np.float64)
    for jj in range(FFT_G):
        g1[:, 0, :, jj, :, jj] = np.cos(ang[:, jj])
        g1[:, 1, :, jj, :, jj] = np.sin(ang[:, jj])
    g1 = g1.reshape(len(g), 2 * GRID_W * FFT_G, GRID_W * FFT_G)
    ang2 = 2.0 * np.pi * ((k[:, None] * n[None, :]) % GRID_W).astype(np.float64) / GRID_W
    c2, s2 = np.cos(ang2), np.sin(ang2)
    m2 = np.block([[c2, -s2], [s2, c2]])
    d = np.arange(HEAD_DIM)
    ang3 = 2.0 * np.pi * ((d[:, None] * d[None, :]) % HEAD_DIM).astype(np.float64) / HEAD_DIM
    scale = 1.0 / np.sqrt(float(SEQ * HEAD_DIM))
    cd = np.kron(np.eye(N_SUB), np.cos(ang3)) * scale
    sd = np.kron(np.eye(N_SUB), np.sin(ang3)) * scale
    m3 = np.concatenate([cd, -sd], axis=0)
    return g1.astype(np.float32), m2.astype(np.float32), m3.astype(np.float32)


@functools.lru_cache(maxsize=None)
def _head_mean_matrix():
    return np.kron(np.eye(N_SUB), np.full((HEAD_DIM, HEAD_DIM), 1.0 / HEAD_DIM)).astype(np.float32)


def _natten_bias_table(rpb):
    n_lh = rpb.shape[0] * rpb.shape[1]
    rpb = rpb.reshape(n_lh, 2 * WIN_H - 1, 2 * WIN_W - 1)
    c = np.arange(GRID_W)
    col_start = np.clip(c - WIN_W // 2, 0, GRID_W - WIN_W)
    col_ok = (c[None, :] >= col_start[:, None]) & (c[None, :] < col_start[:, None] + WIN_W)
    lo = GRID_W - WIN_W
    v = jnp.pad(rpb.astype(F32), ((0, 0), (0, 0), (lo, 2 * GRID_W - lo - (2 * WIN_W - 1))))
    flat = jnp.tile(v, (1, 1, GRID_W))[..., :GRID_W * (2 * GRID_W - 1)]
    toe = flat.reshape(n_lh, 2 * WIN_H - 1, GRID_W, 2 * GRID_W - 1)[..., GRID_W - 1:]
    toe = jnp.where(col_ok[None, None, :, :], toe, NEG)
    toe = toe.transpose(1, 0, 2, 3).reshape(2 * WIN_H - 1, n_lh * GRID_W, GRID_W)
    return jnp.concatenate([toe[:-1], toe[1:]], axis=-1)


def _head_rmsnorm(y, gain, ones_bd):
    ms = jnp.dot((y * y).astype(BF16), ones_bd, preferred_element_type=F32)
    return y * lax.rsqrt(ms + EPS) * gain


def _silu(g):
    h = 0.5 * g
    return h + h * jnp.tanh(h)


def _lane_head(rows):
    return lax.broadcasted_iota(jnp.int32, (rows, GROUP_W), 1) // HEAD_DIM


def _stack_heads(q, lane_head):
    qf = q.astype(F32)
    return jnp.concatenate(
        [jnp.where(lane_head == h, qf, 0.0) for h in range(N_SUB)], axis=0).astype(BF16)


def _pick_heads(o, lane_head, rows):
    out = o[0:rows]
    for h in range(1, N_SUB):
        out = jnp.where(lane_head == h, o[h * rows:(h + 1) * rows], out)
    return out


def _softmax_pv(s, v):
    m = jnp.max(s, axis=-1, keepdims=True)
    e = jnp.exp2(s - m)
    l = jnp.sum(e, axis=-1, keepdims=True)
    return jnp.dot(e.astype(BF16), v, preferred_element_type=F32) / l


_NT = (((1,), (1,)), ((), ()))


def _mem_kv_kernel(mem_ref, g_ref, w_ref, kg_ref, ones_ref, k_ref, v_ref):
    m = mem_ref[0]
    ms = jnp.mean(m * m, axis=-1, keepdims=True)
    mn = (m * lax.rsqrt(ms + EPS) * g_ref[...]).astype(BF16)
    kv = jnp.dot(mn, w_ref[...], preferred_element_type=F32)
    k = _head_rmsnorm(kv[:, :GROUP_W], kg_ref[...], ones_ref[...])
    k_ref[0] = k.astype(BF16)
    v_ref[0] = kv[:, GROUP_W:].astype(BF16)


def _in_proj_kernel(x_ref, g_ref, w_ref, hg_ref, ones_ref, proj_ref, cin_ref):
    head_gain_row = {3: 0, 4: 1, 9: 2}
    for piece in range(IN_PROJ_SPLIT):
        rows = slice(piece * (TM_IN // IN_PROJ_SPLIT), (piece + 1) * (TM_IN // IN_PROJ_SPLIT))
        x = x_ref[rows, :]
        inv = lax.rsqrt(jnp.mean(x * x, axis=-1, keepdims=True) + EPS)
        h = (x * g_ref[...]).astype(BF16)
        y_all = jnp.dot(h, w_ref[...], preferred_element_type=F32)
        col = 0
        for j in range(N_IN_SLOTS):
            y = y_all[:, j * GROUP_W:(j + 1) * GROUP_W] * inv
            if j in head_gain_row:
                r = head_gain_row[j]
                y = _head_rmsnorm(y, hg_ref[r:r + 1, :], ones_ref[...])
            if j == SLOT_C_IN:
                cin_ref[rows, :] = y
            else:
                proj_ref[rows, col:col + GROUP_W] = y.astype(BF16)
                col += GROUP_W


def _fnet_kernel(x_ref, g1_ref, m2_ref, m3_ref, wf_ref, z_ref, t_ref):
    half = GRID_W * FFT_G

    def stage1(g, carry):
        n0 = pl.multiple_of(g * FFT_G, FFT_G)
        x = x_ref[0, :, pl.ds(n0, FFT_G), :].reshape(half, GROUP_W).astype(BF16)
        y = jnp.dot(g1_ref[g], x, preferred_element_type=F32)
        t_ref[:, :, pl.ds(n0, FFT_G), :] = y.reshape(2, GRID_W, FFT_G, GROUP_W)
        return carry

    m3w = jnp.dot(m3_ref[...], wf_ref[...], preferred_element_type=F32).astype(BF16)

    def stage2(g, carry):
        k0 = g * FFT_G
        t = jnp.concatenate(
            [jnp.concatenate([t_ref[0, k0 + j], t_ref[1, k0 + j]], axis=0) for j in range(FFT_G)],
            axis=-1).astype(BF16)
        ab = jnp.dot(m2_ref[...], t, preferred_element_type=F32)
        ab = jnp.concatenate(
            [jnp.concatenate([ab[:GRID_W, j * GROUP_W:(j + 1) * GROUP_W],
                              ab[GRID_W:, j * GROUP_W:(j + 1) * GROUP_W]], axis=-1) for j in range(FFT_G)],
            axis=0).astype(BF16)
        zc = jnp.dot(ab, m3w, preferred_element_type=F32)
        zc = zc.reshape(FFT_G, GRID_W, GROUP_W)
        for hf in range(GROUP_W // LANES):
            z_ref[0, hf, pl.ds(k0, FFT_G), :, :] = zc[:, :, hf * LANES:(hf + 1) * LANES]
        return carry

    n_groups = GRID_W // FFT_G
    lax.fori_loop(0, n_groups, stage1, 0, unroll=2)
    lax.fori_loop(0, n_groups, stage2, 0, unroll=2)


def _mixer_kernel(x_ref, au_ref, av_ref, ag_ref, bq_ref, bk_ref, bv_ref, bg_ref, cg_ref, dq_ref, dg_ref,
                  zc_ref, km_ref, vm_ref, wout_ref, lng_ref, lnb_ref, ws_ref, bs_ref, bias_ref,
                  o_ref, y_ref, yb_ref):
    t = pl.program_id(1)

    lh_tile = _lane_head(TM)
    qs = _stack_heads(dq_ref[0], lh_tile)
    s = lax.dot_general(qs, km_ref[0], _NT, preferred_element_type=F32)
    yd = _pick_heads(_softmax_pv(s, vm_ref[0]), lh_tile, TM)
    y_ref[:, 3 * GROUP_W:4 * GROUP_W] = (yd * _silu(dg_ref[0].astype(F32))).astype(BF16)

    lh_row = _lane_head(GRID_W)
    for i in range(ROWS_PER_TILE):
        r = t * ROWS_PER_TILE + i
        rs = jnp.clip(r - WIN_H // 2, 0, GRID_H - WIN_H)
        k0 = pl.multiple_of(rs * GRID_W, GRID_W)
        qs = _stack_heads(bq_ref[0, i * GRID_W:(i + 1) * GRID_W, :], lh_row)
        kw = bk_ref[0, pl.ds(k0, WIN_H * GRID_W), :]
        vw = bv_ref[0, pl.ds(k0, WIN_H * GRID_W), :]
        d0 = (WIN_H - 1) - (r - rs)
        bias = jnp.concatenate([bias_ref[d0 + 2 * m] for m in range(WIN_H // 2)], axis=-1)
        s = lax.dot_general(qs, kw, _NT, preferred_element_type=F32) + bias
        o = _softmax_pv(s, vw)
        yb_ref[i * GRID_W:(i + 1) * GRID_W, :] = _pick_heads(o, lh_row, GRID_W)
    y_ref[:, GROUP_W:2 * GROUP_W] = (yb_ref[...] * _silu(bg_ref[0].astype(F32))).astype(BF16)

    lh_chunk = _lane_head(CHUNK)
    for c in range(TM // CHUNK):
        rows = slice(c * CHUNK, (c + 1) * CHUNK)
        v = av_ref[0, rows, :].astype(F32)
        mu = jnp.mean(v, axis=-1, keepdims=True)
        var = jnp.mean(jnp.square(v - mu), axis=-1, keepdims=True)
        vn = ((v - mu) * lax.rsqrt(var + EPS) * lng_ref[...] + lnb_ref[...]).astype(BF16)
        sp = jnp.dot(ws_ref[...], vn, preferred_element_type=F32)
        s = _pick_heads(sp, lh_chunk, CHUNK) + bs_ref[...]
        ya = au_ref[0, rows, :].astype(F32) * s * _silu(ag_ref[0, rows, :].astype(F32))
        y_ref[rows, 0:GROUP_W] = ya.astype(BF16)

    zc_halves = [zc_ref.at[0, hf].reshape(GRID_W * ROWS_PER_TILE, LANES) for hf in range(GROUP_W // LANES)]
    zc = jnp.concatenate(
        [jnp.concatenate([zh[pl.ds(i, GRID_W, stride=ROWS_PER_TILE), :] for zh in zc_halves], axis=-1)
         for i in range(ROWS_PER_TILE)], axis=0)
    y_ref[:, 2 * GROUP_W:3 * GROUP_W] = (zc * _silu(cg_ref[0].astype(F32))).astype(BF16)

    o_ref[0] = x_ref[0] + jnp.dot(y_ref[...], wout_ref[...], preferred_element_type=F32)


def _const_spec(shape):
    nd = len(shape)
    return pl.BlockSpec(shape, lambda *_: (0,) * nd)


def _layer_spec(shape, layer):
    nd = len(shape)
    return pl.BlockSpec((None,) + tuple(shape), lambda *_: (layer,) + (0,) * nd)


def _params(dims):
    return pltpu.CompilerParams(dimension_semantics=("arbitrary",) * dims, vmem_limit_bytes=VMEM_LIMIT)


def _mem_kv(mem, g, w_kv, kg, ones_bd):
    return pl.pallas_call(
        _mem_kv_kernel,
        grid=(DEPTH, BATCH),
        in_specs=[
            pl.BlockSpec((1, N_MEM, D_MODEL), lambda l, b: (b, 0, 0)),
            pl.BlockSpec((None, 1, D_MODEL), lambda l, b: (l, 0, 0)),
            pl.BlockSpec((None, D_MODEL, 2 * GROUP_W), lambda l, b: (l, 0, 0)),
            pl.BlockSpec((None, 1, GROUP_W), lambda l, b: (l, 0, 0)),
            _const_spec((GROUP_W, GROUP_W)),
        ],
        out_specs=[pl.BlockSpec((None, 1, N_MEM, GROUP_W), lambda l, b: (l, b, 0, 0))] * 2,
        out_shape=[jax.ShapeDtypeStruct((DEPTH, BATCH, N_MEM, GROUP_W), BF16)] * 2,
        compiler_params=_params(2),
        name="mem_kv",
    )(mem, g, w_kv, kg, ones_bd)


def _in_proj(layer, x2d, g, w_in, head_gains, ones_bd):
    n_tok = x2d.shape[0]
    return pl.pallas_call(
        _in_proj_kernel,
        grid=(n_tok // TM_IN,),
        in_specs=[
            pl.BlockSpec((TM_IN, D_MODEL), lambda i: (i, 0)),
            _layer_spec((1, D_MODEL), layer),
            _layer_spec((D_MODEL, IN_COLS), layer),
            _layer_spec((8, GROUP_W), layer),
            _const_spec((GROUP_W, GROUP_W)),
        ],
        out_specs=[
            pl.BlockSpec((TM_IN, MAIN_COLS), lambda i: (i, 0)),
            pl.BlockSpec((TM_IN, GROUP_W), lambda i: (i, 0)),
        ],
        out_shape=[
            jax.ShapeDtypeStruct((n_tok, MAIN_COLS), BF16),
            jax.ShapeDtypeStruct((n_tok, GROUP_W), F32),
        ],
        compiler_params=_params(1),
        name="in_proj",
    )(x2d, g, w_in, head_gains, ones_bd)


def _fnet(layer, c_in, g1, m2, m3, wf_bd):
    x4 = c_in.reshape(BATCH, GRID_W, GRID_W, GROUP_W)
    n_half = GROUP_W // LANES
    return pl.pallas_call(
        _fnet_kernel,
        grid=(BATCH,),
        in_specs=[pl.BlockSpec((1, GRID_W, GRID_W, GROUP_W), lambda b: (b, 0, 0, 0)),
                  _const_spec(g1.shape), _const_spec(m2.shape), _const_spec(m3.shape),
                  _layer_spec((GROUP_W, GROUP_W), layer)],
        out_specs=pl.BlockSpec((1, n_half, GRID_W, GRID_W, LANES), lambda b: (b, 0, 0, 0, 0)),
        out_shape=jax.ShapeDtypeStruct((BATCH, n_half, GRID_W, GRID_W, LANES), F32),
        scratch_shapes=[pltpu.VMEM((2, GRID_W, GRID_W, GROUP_W), F32)],
        compiler_params=_params(1),
        name="fnet",
    )(x4, g1, m2, m3, wf_bd)


def _mixer(layer, x, proj, zc, km, vm, w_out, ln_g, ln_b, ws_stack, bs_tab, bias_tab):
    def tile_slot(j):
        return pl.BlockSpec((1, TM, GROUP_W), lambda b, t, j=j: (b, t, j))

    def full_slot(j):
        return pl.BlockSpec((1, SEQ, GROUP_W), lambda b, t, j=j: (b, 0, j))

    mem_spec = pl.BlockSpec((None, 1, N_MEM, GROUP_W), lambda b, t: (layer, b, 0, 0))
    in_specs = [
        pl.BlockSpec((1, TM, D_MODEL), lambda b, t: (b, t, 0)),
        tile_slot(M_AU), tile_slot(M_AV), tile_slot(M_AG),
        tile_slot(M_BQ), full_slot(M_BK), full_slot(M_BV), tile_slot(M_BG),
        tile_slot(M_CG), tile_slot(M_DQ), tile_slot(M_DG),
        pl.BlockSpec((1, GROUP_W // LANES, GRID_W, ROWS_PER_TILE, LANES), lambda b, t: (b, 0, 0, t, 0)),
        mem_spec, mem_spec,
        _layer_spec((D_MODEL, D_MODEL), layer),
        _layer_spec((1, GROUP_W), layer), _layer_spec((1, GROUP_W), layer),
        _layer_spec((N_SUB * CHUNK, CHUNK), layer),
        _layer_spec((CHUNK, GROUP_W), layer),
        pl.BlockSpec((2 * WIN_H - 2, N_SUB * GRID_W, 2 * GRID_W), lambda b, t: (0, layer, 0)),
    ]
    return pl.pallas_call(
        _mixer_kernel,
        grid=(BATCH, SEQ // TM),
        in_specs=in_specs,
        out_specs=pl.BlockSpec((1, TM, D_MODEL), lambda b, t: (b, t, 0)),
        out_shape=jax.ShapeDtypeStruct((BATCH, SEQ, D_MODEL), F32),
        scratch_shapes=[pltpu.VMEM((TM, D_MODEL), BF16), pltpu.VMEM((TM, GROUP_W), F32)],
        compiler_params=_params(2),
        name="mixer",
    )(x, *([proj] * 10), zc, km, vm, w_out, ln_g, ln_b, ws_stack, bs_tab, bias_tab)


def _tile_heads(g):
    return jnp.tile(g.astype(F32), (1, N_SUB))


def kernel(x, mem, norm_g, w_in, w_out, gm_ln_g, gm_ln_b, gm_w_s, gm_b_s, na_qn_g, na_kn_g, na_rpb, fn_w,
           mem_norm_g, mem_w_kv, mem_qn_g, mem_kn_g):
    g1_np, m2_np, m3_np = _dft_tables()
    g1 = jnp.asarray(g1_np).astype(BF16)
    m2 = jnp.asarray(m2_np).astype(BF16)
    m3 = jnp.asarray(m3_np).astype(BF16)
    ones_bd = jnp.asarray(_head_mean_matrix()).astype(BF16)
    qk_scale = HEAD_DIM ** -0.5 * LOG2E

    w_in_b = w_in.astype(BF16)
    w_out_b = w_out.astype(BF16)
    norm_g3 = norm_g.reshape(DEPTH, 1, D_MODEL)
    head_gains = jnp.stack([_tile_heads(na_qn_g) * qk_scale, _tile_heads(na_kn_g),
                            _tile_heads(mem_qn_g) * qk_scale], axis=1)
    head_gains = jnp.pad(head_gains, ((0, 0), (0, 5), (0, 0)))
    eye = jnp.eye(N_SUB, dtype=fn_w.dtype)
    wf_bd = (fn_w[:, :, :, None, :] * eye[None, :, None, :, None]).reshape(DEPTH, GROUP_W, GROUP_W).astype(BF16)
    ws_stack = gm_w_s.reshape(DEPTH, N_SUB * CHUNK, CHUNK).astype(BF16)
    bs_tab = jnp.repeat(gm_b_s.transpose(0, 2, 1), HEAD_DIM, axis=2)
    ln_g3 = gm_ln_g.reshape(DEPTH, 1, GROUP_W)
    ln_b3 = gm_ln_b.reshape(DEPTH, 1, GROUP_W)
    bias_tab = _natten_bias_table(na_rpb * LOG2E)

    km, vm = _mem_kv(mem, mem_norm_g.reshape(DEPTH, 1, D_MODEL), mem_w_kv.astype(BF16),
                     _tile_heads(mem_kn_g).reshape(DEPTH, 1, GROUP_W), ones_bd)
    for l in range(DEPTH):
        proj, c_in = _in_proj(l, x.reshape(BATCH * SEQ, D_MODEL), norm_g3, w_in_b, head_gains, ones_bd)
        zc = _fnet(l, c_in, g1, m2, m3, wf_bd)
        x = _mixer(l, x, proj.reshape(BATCH, SEQ, MAIN_COLS), zc, km, vm, w_out_b,
                   ln_g3, ln_b3, ws_stack, bs_tab, bias_tab)
    return x
```

```python
import functools

import numpy as np
import jax
import jax.numpy as jnp
from jax import lax
from jax.experimental import pallas as pl
from jax.experimental.pallas import tpu as pltpu

D_MODEL = 1024
BATCH = 8
SEQ = 4096
DEPTH = 2
N_MEM = 256
GROUP_W = 256
N_SUB = 4
HEAD_DIM = 64
CHUNK = 128
GRID_W = 64
GRID_H = SEQ // GRID_W
WIN_H = 8
WIN_W = 16
N_IN_SLOTS = 11
IN_COLS = N_IN_SLOTS * GROUP_W
EPS = 1e-6
NEG = -1e30
LOG2E = 1.4426950408889634

SLOT_C_IN = 7
N_MAIN_SLOTS = N_IN_SLOTS - 1
MAIN_COLS = N_MAIN_SLOTS * GROUP_W
M_AU, M_AV, M_AG, M_BQ, M_BK, M_BV, M_BG, M_CG, M_DQ, M_DG = range(10)

MEM_KV_BATCH = 4
TM = 1024
ROWS_PER_TILE = TM // GRID_W
TM_IN = 1024
IN_PROJ_SPLIT = 1
FFT_G = 8

LANES = 128
F32 = jnp.float32
BF16 = jnp.bfloat16
VMEM_LIMIT = 56 * 1024 * 1024


@functools.lru_cache(maxsize=None)
def _dft_tables():
    g = np.arange(GRID_W // FFT_G)
    j = np.arange(FFT_G)
    k = np.arange(GRID_W)
    n = np.arange(GRID_W)
    s = GRID_W * n[None, None, None, :] + FFT_G * g[:, None, None, None] + j[None, :, None, None]
    phase = (k[None, None, :, None] * s) % SEQ
    ang = 2.0 * np.pi * phase.astype(np.float64) / SEQ
    g1 = np.zeros((len(g), 2, GRID_W, FFT_G, GRID_W, FFT_G), np.float64)
    for jj in range(FFT_G):
        g1[:, 0, :, jj, :, jj] = np.cos(ang[:, jj])
        g1[:, 1, :, jj, :, jj] = np.sin(ang[:, jj])
    g1 = g1.reshape(len(g), 2 * GRID_W * FFT_G, GRID_W * FFT_G)
    ang2 = 2.0 * np.pi * ((k[:, None] * n[None, :]) % GRID_W).astype(np.float64) / GRID_W
    c2, s2 = np.cos(ang2), np.sin(ang2)
    m2 = np.block([[c2, -s2], [s2, c2]])
    d = np.arange(HEAD_DIM)
    ang3 = 2.0 * np.pi * ((d[:, None] * d[None, :]) % HEAD_DIM).astype(np.float64) / HEAD_DIM
    scale = 1.0 / np.sqrt(float(SEQ * HEAD_DIM))
    cd = np.kron(np.eye(N_SUB), np.cos(ang3)) * scale
    sd = np.kron(np.eye(N_SUB), np.sin(ang3)) * scale
    m3 = np.concatenate([cd, -sd], axis=0)
    return g1.astype(np.float32), m2.astype(np.float32), m3.astype(np.float32)


@functools.lru_cache(maxsize=None)
def _head_mean_matrix():
    return np.kron(np.eye(N_SUB), np.full((HEAD_DIM, HEAD_DIM), 1.0 / HEAD_DIM)).astype(np.float32)


def _natten_bias_table(rpb):
    n_lh = rpb.shape[0] * rpb.shape[1]
    rpb = rpb.reshape(n_lh, 2 * WIN_H - 1, 2 * WIN_W - 1)
    c = np.arange(GRID_W)
    col_start = np.clip(c - WIN_W // 2, 0, GRID_W - WIN_W)
    col_ok = (c[None, :] >= col_start[:, None]) & (c[None, :] < col_start[:, None] + WIN_W)
    lo = GRID_W - WIN_W
    v = jnp.pad(rpb.astype(F32), ((0, 0), (0, 0), (lo, 2 * GRID_W - lo - (2 * WIN_W - 1))))
    flat = jnp.tile(v, (1, 1, GRID_W))[..., :GRID_W * (2 * GRID_W - 1)]
    toe = flat.reshape(n_lh, 2 * WIN_H - 1, GRID_W, 2 * GRID_W - 1)[..., GRID_W - 1:]
    toe = jnp.where(col_ok[None, None, :, :], toe, NEG)
    toe = toe.transpose(1, 0, 2, 3).reshape(2 * WIN_H - 1, n_lh * GRID_W, GRID_W)
    return jnp.concatenate([toe[:-1], toe[1:]], axis=-1)


def _head_rmsnorm(y, gain, ones_bd):
    ms = jnp.dot((y * y).astype(BF16), ones_bd, preferred_element_type=F32)
    return y * lax.rsqrt(ms + EPS) * gain


def _silu(g):
    h = 0.5 * g
    return h + h * jnp.tanh(h)


def _lane_head(rows):
    return lax.broadcasted_iota(jnp.int32, (rows, GROUP_W), 1) // HEAD_DIM


def _stack_heads(q, lane_head):
    qf = q.astype(F32)
    return jnp.concatenate(
        [jnp.where(lane_head == h, qf, 0.0) for h in range(N_SUB)], axis=0).astype(BF16)


def _pick_heads(o, lane_head, rows):
    out = o[0:rows]
    for h in range(1, N_SUB):
        out = jnp.where(lane_head == h, o[h * rows:(h + 1) * rows], out)
    return out


def _softmax_pv(s, v):
    m = jnp.max(s, axis=-1, keepdims=True)
    e = jnp.exp2(s - m)
    l = jnp.sum(e, axis=-1, keepdims=True)
    return jnp.dot(e.astype(BF16), v, preferred_element_type=F32) / l


_NT = (((1,), (1,)), ((), ()))


def _mem_kv_kernel(mem_ref, g_ref, w_ref, kg_ref, ones_ref, k_ref, v_ref):
    m = mem_ref[...].reshape(MEM_KV_BATCH * N_MEM, D_MODEL)
    ms = jnp.mean(m * m, axis=-1, keepdims=True)
    mn = (m * lax.rsqrt(ms + EPS) * g_ref[...]).astype(BF16)
    kv = jnp.dot(mn, w_ref[...], preferred_element_type=F32)
    k = _head_rmsnorm(kv[:, :GROUP_W], kg_ref[...], ones_ref[...])
    k_ref[...] = k.astype(BF16).reshape(MEM_KV_BATCH, N_MEM, GROUP_W)
    v_ref[...] = kv[:, GROUP_W:].astype(BF16).reshape(MEM_KV_BATCH, N_MEM, GROUP_W)


def _in_proj_kernel(x_ref, g_ref, w_ref, hg_ref, ones_ref, proj_ref, cin_ref):
    head_gain_row = {3: 0, 4: 1, 9: 2}
    for piece in range(IN_PROJ_SPLIT):
        rows = slice(piece * (TM_IN // IN_PROJ_SPLIT), (piece + 1) * (TM_IN // IN_PROJ_SPLIT))
        x = x_ref[rows, :]
        inv = lax.rsqrt(jnp.mean(x * x, axis=-1, keepdims=True) + EPS)
        h = (x * g_ref[...]).astype(BF16)
        y_all = jnp.dot(h, w_ref[...], preferred_element_type=F32)
        col = 0
        for j in range(N_IN_SLOTS):
            y = y_all[:, j * GROUP_W:(j + 1) * GROUP_W] * inv
            if j in head_gain_row:
                r = head_gain_row[j]
                y = _head_rmsnorm(y, hg_ref[r:r + 1, :], ones_ref[...])
            if j == SLOT_C_IN:
                cin_ref[rows, :] = y
            else:
                proj_ref[rows, col:col + GROUP_W] = y.astype(BF16)
                col += GROUP_W


def _fnet_kernel(x_ref, g1_ref, m2_ref, m3_ref, wf_ref, z_ref, t_ref):
    half = GRID_W * FFT_G

    def stage1(g, carry):
        n0 = pl.multiple_of(g * FFT_G, FFT_G)
        x = x_ref[0, :, pl.ds(n0, FFT_G), :].reshape(half, GROUP_W).astype(BF16)
        y = jnp.dot(g1_ref[g], x, preferred_element_type=F32)
        t_ref[:, :, pl.ds(n0, FFT_G), :] = y.reshape(2, GRID_W, FFT_G, GROUP_W)
        return carry

    m3w = jnp.dot(m3_ref[...], wf_ref[...], preferred_element_type=F32).astype(BF16)

    def stage2(g, carry):
        k0 = g * FFT_G
        t = jnp.concatenate(
            [jnp.concatenate([t_ref[0, k0 + j], t_ref[1, k0 + j]], axis=0) for j in range(FFT_G)],
            axis=-1).astype(BF16)
        ab = jnp.dot(m2_ref[...], t, preferred_element_type=F32)
        ab = jnp.concatenate(
            [jnp.concatenate([ab[:GRID_W, j * GROUP_W:(j + 1) * GROUP_W],
                              ab[GRID_W:, j * GROUP_W:(j + 1) * GROUP_W]], axis=-1) for j in range(FFT_G)],
            axis=0).astype(BF16)
        zc = jnp.dot(ab, m3w, preferred_element_type=F32)
        zc = zc.reshape(FFT_G, GRID_W, GROUP_W)
        for hf in range(GROUP_W // LANES):
            z_ref[0, hf, pl.ds(k0, FFT_G), :, :] = zc[:, :, hf * LANES:(hf + 1) * LANES]
        return carry

    n_groups = GRID_W // FFT_G
    lax.fori_loop(0, n_groups, stage1, 0, unroll=2)
    lax.fori_loop(0, n_groups, stage2, 0, unroll=2)


def _mixer_kernel(x_ref, au_ref, av_ref, ag_ref, bq_ref, bk_ref, bv_ref, bg_ref, cg_ref, dq_ref, dg_ref,
                  zc_ref, km_ref, vm_ref, wout_ref, lng_ref, lnb_ref, ws_ref, bs_ref, bias_ref,
                  o_ref, y_ref, yb_ref):
    t = pl.program_id(1)

    lh_tile = _lane_head(TM)
    qs = _stack_heads(dq_ref[0], lh_tile)
    s = lax.dot_general(qs, km_ref[0], _NT, preferred_element_type=F32)
    yd = _pick_heads(_softmax_pv(s, vm_ref[0]), lh_tile, TM)
    y_ref[:, 3 * GROUP_W:4 * GROUP_W] = (yd * _silu(dg_ref[0].astype(F32))).astype(BF16)

    lh_row = _lane_head(GRID_W)
    for i in range(ROWS_PER_TILE):
        r = t * ROWS_PER_TILE + i
        rs = jnp.clip(r - WIN_H // 2, 0, GRID_H - WIN_H)
        k0 = pl.multiple_of(rs * GRID_W, GRID_W)
        qs = _stack_heads(bq_ref[0, i * GRID_W:(i + 1) * GRID_W, :], lh_row)
        kw = bk_ref[0, pl.ds(k0, WIN_H * GRID_W), :]
        vw = bv_ref[0, pl.ds(k0, WIN_H * GRID_W), :]
        d0 = (WIN_H - 1) - (r - rs)
        bias = jnp.concatenate([bias_ref[d0 + 2 * m] for m in range(WIN_H // 2)], axis=-1)
        s = lax.dot_general(qs, kw, _NT, preferred_element_type=F32) + bias
        o = _softmax_pv(s, vw)
        yb_ref[i * GRID_W:(i + 1) * GRID_W, :] = _pick_heads(o, lh_row, GRID_W)
    y_ref[:, GROUP_W:2 * GROUP_W] = (yb_ref[...] * _silu(bg_ref[0].astype(F32))).astype(BF16)

    lh_chunk = _lane_head(CHUNK)
    for c in range(TM // CHUNK):
        rows = slice(c * CHUNK, (c + 1) * CHUNK)
        v = av_ref[0, rows, :].astype(F32)
        mu = jnp.mean(v, axis=-1, keepdims=True)
        var = jnp.mean(jnp.square(v - mu), axis=-1, keepdims=True)
        vn = ((v - mu) * lax.rsqrt(var + EPS) * lng_ref[...] + lnb_ref[...]).astype(BF16)
        sp = jnp.dot(ws_ref[...], vn, preferred_element_type=F32)
        s = _pick_heads(sp, lh_chunk, CHUNK) + bs_ref[...]
        ya = au_ref[0, rows, :].astype(F32) * s * _silu(ag_ref[0, rows, :].astype(F32))
        y_ref[rows, 0:GROUP_W] = ya.astype(BF16)

    zc_halves = [zc_ref.at[0, hf].reshape(GRID_W * ROWS_PER_TILE, LANES) for hf in range(GROUP_W // LANES)]
    zc = jnp.concatenate(
        [jnp.concatenate([zh[pl.ds(i, GRID_W, stride=ROWS_PER_TILE), :] for zh in zc_halves], axis=-1)
         for i in range(ROWS_PER_TILE)], axis=0)
    y_ref[:, 2 * GROUP_W:3 * GROUP_W] = (zc * _silu(cg_ref[0].astype(F32))).astype(BF16)

    o_ref[0] = x_ref[0] + jnp.dot(y_ref[...], wout_ref[...], preferred_element_type=F32)


def _const_spec(shape):
    nd = len(shape)
    return pl.BlockSpec(shape, lambda *_: (0,) * nd)


def _layer_spec(shape, layer):
    nd = len(shape)
    return pl.BlockSpec((None,) + tuple(shape), lambda *_: (layer,) + (0,) * nd)


def _params(dims, flags=None):
    return pltpu.CompilerParams(dimension_semantics=("arbitrary",) * dims, vmem_limit_bytes=VMEM_LIMIT,
                                flags=flags)


def _mem_kv(mem, g, w_kv, kg, ones_bd):
    return pl.pallas_call(
        _mem_kv_kernel,
        grid=(DEPTH, BATCH // MEM_KV_BATCH),
        in_specs=[
            pl.BlockSpec((MEM_KV_BATCH, N_MEM, D_MODEL), lambda l, b: (b, 0, 0)),
            pl.BlockSpec((None, 1, D_MODEL), lambda l, b: (l, 0, 0)),
            pl.BlockSpec((None, D_MODEL, 2 * GROUP_W), lambda l, b: (l, 0, 0)),
            pl.BlockSpec((None, 1, GROUP_W), lambda l, b: (l, 0, 0)),
            _const_spec((GROUP_W, GROUP_W)),
        ],
        out_specs=[pl.BlockSpec((None, MEM_KV_BATCH, N_MEM, GROUP_W), lambda l, b: (l, b, 0, 0))] * 2,
        out_shape=[jax.ShapeDtypeStruct((DEPTH, BATCH, N_MEM, GROUP_W), BF16)] * 2,
        compiler_params=_params(2),
        name="mem_kv",
    )(mem, g, w_kv, kg, ones_bd)


def _in_proj(layer, x2d, g, w_in, head_gains, ones_bd):
    n_tok = x2d.shape[0]
    return pl.pallas_call(
        _in_proj_kernel,
        grid=(n_tok // TM_IN,),
        in_specs=[
            pl.BlockSpec((TM_IN, D_MODEL), lambda i: (i, 0)),
            _layer_spec((1, D_MODEL), layer),
            _layer_spec((D_MODEL, IN_COLS), layer),
            _layer_spec((8, GROUP_W), layer),
            _const_spec((GROUP_W, GROUP_W)),
        ],
        out_specs=[
            pl.BlockSpec((TM_IN, MAIN_COLS), lambda i: (i, 0)),
            pl.BlockSpec((TM_IN, GROUP_W), lambda i: (i, 0)),
        ],
        out_shape=[
            jax.ShapeDtypeStruct((n_tok, MAIN_COLS), BF16),
            jax.ShapeDtypeStruct((n_tok, GROUP_W), F32),
        ],
        compiler_params=_params(1),
        name="in_proj",
    )(x2d, g, w_in, head_gains, ones_bd)


def _fnet(layer, c_in, g1, m2, m3, wf_bd):
    x4 = c_in.reshape(BATCH, GRID_W, GRID_W, GROUP_W)
    n_half = GROUP_W // LANES
    return pl.pallas_call(
        _fnet_kernel,
        grid=(BATCH,),
        in_specs=[pl.BlockSpec((1, GRID_W, GRID_W, GROUP_W), lambda b: (b, 0, 0, 0)),
                  _const_spec(g1.shape), _const_spec(m2.shape), _const_spec(m3.shape),
                  _layer_spec((GROUP_W, GROUP_W), layer)],
        out_specs=pl.BlockSpec((1, n_half, GRID_W, GRID_W, LANES), lambda b: (b, 0, 0, 0, 0)),
        out_shape=jax.ShapeDtypeStruct((BATCH, n_half, GRID_W, GRID_W, LANES), F32),
        scratch_shapes=[pltpu.VMEM((2, GRID_W, GRID_W, GROUP_W), F32)],
        compiler_params=_params(1),
        name="fnet",
    )(x4, g1, m2, m3, wf_bd)


def _mixer(layer, x, proj, zc, km, vm, w_out, ln_g, ln_b, ws_stack, bs_tab, bias_tab):
    def tile_slot(j):
        return pl.BlockSpec((1, TM, GROUP_W), lambda b, t, j=j: (b, t, j))

    def full_slot(j):
        return pl.BlockSpec((1, SEQ, GROUP_W), lambda b, t, j=j: (b, 0, j))

    mem_spec = pl.BlockSpec((None, 1, N_MEM, GROUP_W), lambda b, t: (layer, b, 0, 0))
    in_specs = [
        pl.BlockSpec((1, TM, D_MODEL), lambda b, t: (b, t, 0)),
        tile_slot(M_AU), tile_slot(M_AV), tile_slot(M_AG),
        tile_slot(M_BQ), full_slot(M_BK), full_slot(M_BV), tile_slot(M_BG),
        tile_slot(M_CG), tile_slot(M_DQ), tile_slot(M_DG),
        pl.BlockSpec((1, GROUP_W // LANES, GRID_W, ROWS_PER_TILE, LANES), lambda b, t: (b, 0, 0, t, 0)),
        mem_spec, mem_spec,
        _layer_spec((D_MODEL, D_MODEL), layer),
        _layer_spec((1, GROUP_W), layer), _layer_spec((1, GROUP_W), layer),
        _layer_spec((N_SUB * CHUNK, CHUNK), layer),
        _layer_spec((CHUNK, GROUP_W), layer),
        pl.BlockSpec((2 * WIN_H - 2, N_SUB * GRID_W, 2 * GRID_W), lambda b, t: (0, layer, 0)),
    ]
    return pl.pallas_call(
        _mixer_kernel,
        grid=(BATCH, SEQ // TM),
        in_specs=in_specs,
        out_specs=pl.BlockSpec((1, TM, D_MODEL), lambda b, t: (b, t, 0)),
        out_shape=jax.ShapeDtypeStruct((BATCH, SEQ, D_MODEL), F32),
        scratch_shapes=[pltpu.VMEM((TM, D_MODEL), BF16), pltpu.VMEM((TM, GROUP_W), F32)],
        compiler_params=_params(2),
        name="mixer",
    )(x, *([proj] * 10), zc, km, vm, w_out, ln_g, ln_b, ws_stack, bs_tab, bias_tab)


def _tile_heads(g):
    return jnp.tile(g.astype(F32), (1, N_SUB))


def kernel(x, mem, norm_g, w_in, w_out, gm_ln_g, gm_ln_b, gm_w_s, gm_b_s, na_qn_g, na_kn_g, na_rpb, fn_w,
           mem_norm_g, mem_w_kv, mem_qn_g, mem_kn_g):
    g1_np, m2_np, m3_np = _dft_tables()
    g1 = jnp.asarray(g1_np).astype(BF16)
    m2 = jnp.asarray(m2_np).astype(BF16)
    m3 = jnp.asarray(m3_np).astype(BF16)
    ones_bd = jnp.asarray(_head_mean_matrix()).astype(BF16)
    qk_scale = HEAD_DIM ** -0.5 * LOG2E

    w_in_b = w_in.astype(BF16)
    w_out_b = w_out.astype(BF16)
    norm_g3 = norm_g.reshape(DEPTH, 1, D_MODEL)
    head_gains = jnp.stack([_tile_heads(na_qn_g) * qk_scale, _tile_heads(na_kn_g),
                            _tile_heads(mem_qn_g) * qk_scale], axis=1)
    head_gains = jnp.pad(head_gains, ((0, 0), (0, 5), (0, 0)))
    eye = jnp.eye(N_SUB, dtype=fn_w.dtype)
    wf_bd = (fn_w[:, :, :, None, :] * eye[None, :, None, :, None]).reshape(DEPTH, GROUP_W, GROUP_W).astype(BF16)
    ws_stack = gm_w_s.reshape(DEPTH, N_SUB * CHUNK, CHUNK).astype(BF16)
    bs_tab = jnp.repeat(gm_b_s.transpose(0, 2, 1), HEAD_DIM, axis=2)
    ln_g3 = gm_ln_g.reshape(DEPTH, 1, GROUP_W)
    ln_b3 = gm_ln_b.reshape(DEPTH, 1, GROUP_W)
    bias_tab = _natten_bias_table(na_rpb * LOG2E)

    km, vm = _mem_kv(mem, mem_norm_g.reshape(DEPTH, 1, D_MODEL), mem_w_kv.astype(BF16),
                     _tile_heads(mem_kn_g).reshape(DEPTH, 1, GROUP_W), ones_bd)
    for l in range(DEPTH):
        proj, c_in = _in_proj(l, x.reshape(BATCH * SEQ, D_MODEL), norm_g3, w_in_b, head_gains, ones_bd)
        zc = _fnet(l, c_in, g1, m2, m3, wf_bd)
        x = _mixer(l, x, proj.reshape(BATCH, SEQ, MAIN_COLS), zc, km, vm, w_out_b,
                   ln_g3, ln_b3, ws_stack, bs_tab, bias_tab)
    return x
```

```python
import functools

import numpy as np
import jax
import jax.numpy as jnp
from jax import lax
from jax.experimental import pallas as pl
from jax.experimental.pallas import tpu as pltpu

D_MODEL = 1024
BATCH = 8
SEQ = 4096
DEPTH = 2
N_MEM = 256
GROUP_W = 256
N_SUB = 4
HEAD_DIM = 64
CHUNK = 128
GRID_W = 64
GRID_H = SEQ // GRID_W
WIN_H = 8
WIN_W = 16
N_IN_SLOTS = 11
IN_COLS = N_IN_SLOTS * GROUP_W
EPS = 1e-6
NEG = -1e30
LOG2E = 1.4426950408889634

IN_SLOTS = ("a_u", "a_v", "a_g", "b_q", "b_k", "b_v", "b_g", "c_in", "c_g", "d_q", "d_g")
LOCAL_SLOTS = ("a_u", "a_v", "a_g", "d_q", "d_g")
REST_SLOTS = ("b_q", "b_k", "b_v", "b_g", "c_in", "c_g")
P_BQ, P_BK, P_BV, P_BG, P_CG = range(5)
PROJ_COLS = 5 * GROUP_W

MEM_KV_BATCH = 4
TM = 512
ROWS_PER_TILE = TM // GRID_W
TM_IN = 1024
FFT_G = 8

LANES = 128
F32 = jnp.float32
BF16 = jnp.bfloat16
VMEM_LIMIT = 56 * 1024 * 1024


@functools.lru_cache(maxsize=None)
def _dft_tables():
    g = np.arange(GRID_W // FFT_G)
    j = np.arange(FFT_G)
    k = np.arange(GRID_W)
    n = np.arange(GRID_W)
    s = GRID_W * n[None, None, None, :] + FFT_G * g[:, None, None, None] + j[None, :, None, None]
    phase = (k[None, None, :, None] * s) % SEQ
    ang = 2.0 * np.pi * phase.astype(np.float64) / SEQ
    g1 = np.zeros((len(g), 2, GRID_W, FFT_G, GRID_W, FFT_G), np.float64)
    for jj in range(FFT_G):
        g1[:, 0, :, jj, :, jj] = np.cos(ang[:, jj])
        g1[:, 1, :, jj, :, jj] = np.sin(ang[:, jj])
    g1 = g1.reshape(len(g), 2 * GRID_W * FFT_G, GRID_W * FFT_G)
    ang2 = 2.0 * np.pi * ((k[:, None] * n[None, :]) % GRID_W).astype(np.float64) / GRID_W
    c2, s2 = np.cos(ang2), np.sin(ang2)
    m2 = np.block([[c2, -s2], [s2, c2]])
    d = np.arange(HEAD_DIM)
    ang3 = 2.0 * np.pi * ((d[:, None] * d[None, :]) % HEAD_DIM).astype(np.float64) / HEAD_DIM
    scale = 1.0 / np.sqrt(float(SEQ * HEAD_DIM))
    cd = np.kron(np.eye(N_SUB), np.cos(ang3)) * scale
    sd = np.kron(np.eye(N_SUB), np.sin(ang3)) * scale
    m3 = np.concatenate([cd, -sd], axis=0)
    return g1.astype(np.float32), m2.astype(np.float32), m3.astype(np.float32)


@functools.lru_cache(maxsize=None)
def _head_mean_matrix():
    return np.kron(np.eye(N_SUB), np.full((HEAD_DIM, HEAD_DIM), 1.0 / HEAD_DIM)).astype(np.float32)


def _natten_bias_table(rpb):
    n_lh = rpb.shape[0] * rpb.shape[1]
    rpb = rpb.reshape(n_lh, 2 * WIN_H - 1, 2 * WIN_W - 1)
    c = np.arange(GRID_W)
    col_start = np.clip(c - WIN_W // 2, 0, GRID_W - WIN_W)
    col_ok = (c[None, :] >= col_start[:, None]) & (c[None, :] < col_start[:, None] + WIN_W)
    lo = GRID_W - WIN_W
    v = jnp.pad(rpb.astype(F32), ((0, 0), (0, 0), (lo, 2 * GRID_W - lo - (2 * WIN_W - 1))))
    flat = jnp.tile(v, (1, 1, GRID_W))[..., :GRID_W * (2 * GRID_W - 1)]
    toe = flat.reshape(n_lh, 2 * WIN_H - 1, GRID_W, 2 * GRID_W - 1)[..., GRID_W - 1:]
    toe = jnp.where(col_ok[None, None, :, :], toe, NEG)
    toe = toe.transpose(1, 0, 2, 3).reshape(2 * WIN_H - 1, n_lh * GRID_W, GRID_W)
    return jnp.concatenate([toe[:-1], toe[1:]], axis=-1)


def _head_rmsnorm(y, gain, ones_bd):
    ms = jnp.dot((y * y).astype(BF16), ones_bd, preferred_element_type=F32)
    return y * lax.rsqrt(ms + EPS) * gain


def _silu(g):
    h = 0.5 * g
    return h + h * jnp.tanh(h)


def _lane_head(rows):
    return lax.broadcasted_iota(jnp.int32, (rows, GROUP_W), 1) // HEAD_DIM


def _stack_heads(q, lane_head):
    qf = q.astype(F32)
    return jnp.concatenate(
        [jnp.where(lane_head == h, qf, 0.0) for h in range(N_SUB)], axis=0).astype(BF16)


def _pick_heads(o, lane_head, rows):
    out = o[0:rows]
    for h in range(1, N_SUB):
        out = jnp.where(lane_head == h, o[h * rows:(h + 1) * rows], out)
    return out


def _softmax_pv(s, v):
    m = jnp.max(s, axis=-1, keepdims=True)
    e = jnp.exp2(s - m)
    l = jnp.sum(e, axis=-1, keepdims=True)
    return jnp.dot(e.astype(BF16), v, preferred_element_type=F32) / l


_NT = (((1,), (1,)), ((), ()))


def _mem_kv_kernel(mem_ref, g_ref, w_ref, kg_ref, ones_ref, k_ref, v_ref):
    m = mem_ref[...].reshape(MEM_KV_BATCH * N_MEM, D_MODEL)
    ms = jnp.mean(m * m, axis=-1, keepdims=True)
    mn = (m * lax.rsqrt(ms + EPS) * g_ref[...]).astype(BF16)
    kv = jnp.dot(mn, w_ref[...], preferred_element_type=F32)
    k = _head_rmsnorm(kv[:, :GROUP_W], kg_ref[...], ones_ref[...])
    k_ref[...] = k.astype(BF16).reshape(MEM_KV_BATCH, N_MEM, GROUP_W)
    v_ref[...] = kv[:, GROUP_W:].astype(BF16).reshape(MEM_KV_BATCH, N_MEM, GROUP_W)


def _in_proj_kernel(x_ref, g_ref, w_ref, hg_ref, ones_ref, km_ref, vm_ref, lng_ref, lnb_ref, ws_ref, bs_ref,
                    proj_ref, cin_ref, yad_ref):
    x = x_ref[...]
    inv = lax.rsqrt(jnp.mean(x * x, axis=-1, keepdims=True) + EPS)
    h = (x * g_ref[...]).astype(BF16)
    n_local = len(LOCAL_SLOTS) * GROUP_W
    y_loc = jnp.dot(h, w_ref[:, :n_local], preferred_element_type=F32)
    y_rest = jnp.dot(h, w_ref[:, n_local:], preferred_element_type=F32)

    def local_slot(name):
        j = LOCAL_SLOTS.index(name)
        return y_loc[:, j * GROUP_W:(j + 1) * GROUP_W] * inv

    def rest_slot(name):
        j = REST_SLOTS.index(name)
        return y_rest[:, j * GROUP_W:(j + 1) * GROUP_W] * inv

    lh_tile = _lane_head(TM_IN)
    dq = _head_rmsnorm(local_slot("d_q"), hg_ref[2:3, :], ones_ref[...])
    qs = _stack_heads(dq, lh_tile)
    s = lax.dot_general(qs, km_ref[0], _NT, preferred_element_type=F32)
    yd = _pick_heads(_softmax_pv(s, vm_ref[0]), lh_tile, TM_IN)
    yad_ref[:, GROUP_W:] = (yd * _silu(local_slot("d_g"))).astype(BF16)

    lh_chunk = _lane_head(CHUNK)
    a_u, a_v, a_g = local_slot("a_u"), local_slot("a_v"), local_slot("a_g")
    for c in range(TM_IN // CHUNK):
        rows = slice(c * CHUNK, (c + 1) * CHUNK)
        v = a_v[rows]
        mu = jnp.mean(v, axis=-1, keepdims=True)
        var = jnp.mean(jnp.square(v - mu), axis=-1, keepdims=True)
        vn = ((v - mu) * lax.rsqrt(var + EPS) * lng_ref[...] + lnb_ref[...]).astype(BF16)
        sp = jnp.dot(ws_ref[...], vn, preferred_element_type=F32)
        sg = _pick_heads(sp, lh_chunk, CHUNK) + bs_ref[...]
        yad_ref[rows, :GROUP_W] = (a_u[rows] * sg * _silu(a_g[rows])).astype(BF16)

    proj_ref[:, P_BQ * GROUP_W:(P_BQ + 1) * GROUP_W] = _head_rmsnorm(
        rest_slot("b_q"), hg_ref[0:1, :], ones_ref[...]).astype(BF16)
    proj_ref[:, P_BK * GROUP_W:(P_BK + 1) * GROUP_W] = _head_rmsnorm(
        rest_slot("b_k"), hg_ref[1:2, :], ones_ref[...]).astype(BF16)
    for name, col in (("b_v", P_BV), ("b_g", P_BG), ("c_g", P_CG)):
        proj_ref[:, col * GROUP_W:(col + 1) * GROUP_W] = rest_slot(name).astype(BF16)
    cin_ref[...] = rest_slot("c_in")


def _fnet_kernel(x_ref, g1_ref, m2_ref, m3_ref, wf_ref, z_ref, t_ref):
    half = GRID_W * FFT_G

    def stage1(g, carry):
        n0 = pl.multiple_of(g * FFT_G, FFT_G)
        x = x_ref[0, :, pl.ds(n0, FFT_G), :].reshape(half, GROUP_W).astype(BF16)
        y = jnp.dot(g1_ref[g], x, preferred_element_type=F32)
        t_ref[:, :, pl.ds(n0, FFT_G), :] = y.reshape(2, GRID_W, FFT_G, GROUP_W)
        return carry

    m3w = jnp.dot(m3_ref[...], wf_ref[...], preferred_element_type=F32).astype(BF16)

    def stage2(g, carry):
        k0 = g * FFT_G
        t = jnp.concatenate(
            [jnp.concatenate([t_ref[0, k0 + j], t_ref[1, k0 + j]], axis=0) for j in range(FFT_G)],
            axis=-1).astype(BF16)
        ab = jnp.dot(m2_ref[...], t, preferred_element_type=F32)
        ab = jnp.concatenate(
            [jnp.concatenate([ab[:GRID_W, j * GROUP_W:(j + 1) * GROUP_W],
                              ab[GRID_W:, j * GROUP_W:(j + 1) * GROUP_W]], axis=-1) for j in range(FFT_G)],
            axis=0).astype(BF16)
        zc = jnp.dot(ab, m3w, preferred_element_type=F32)
        zc = zc.reshape(FFT_G, GRID_W, GROUP_W)
        for hf in range(GROUP_W // LANES):
            z_ref[0, hf, pl.ds(k0, FFT_G), :, :] = zc[:, :, hf * LANES:(hf + 1) * LANES]
        return carry

    n_groups = GRID_W // FFT_G
    lax.fori_loop(0, n_groups, stage1, 0, unroll=2)
    lax.fori_loop(0, n_groups, stage2, 0, unroll=2)


def _mixer_kernel(x_ref, bq_ref, bk_ref, bv_ref, bg_ref, cg_ref, yad_ref, zc_ref, wout_ref, bias_ref,
                  o_ref, y_ref, yb_ref):
    t = pl.program_id(1)

    y_ref[:, 0:GROUP_W] = yad_ref[0, :, :GROUP_W]
    y_ref[:, 3 * GROUP_W:4 * GROUP_W] = yad_ref[0, :, GROUP_W:]

    lh_row = _lane_head(GRID_W)
    for i in range(ROWS_PER_TILE):
        r = t * ROWS_PER_TILE + i
        rs = jnp.clip(r - WIN_H // 2, 0, GRID_H - WIN_H)
        k0 = pl.multiple_of(rs * GRID_W, GRID_W)
        qs = _stack_heads(bq_ref[0, i * GRID_W:(i + 1) * GRID_W, :], lh_row)
        kw = bk_ref[0, pl.ds(k0, WIN_H * GRID_W), :]
        vw = bv_ref[0, pl.ds(k0, WIN_H * GRID_W), :]
        d0 = (WIN_H - 1) - (r - rs)
        bias = jnp.concatenate([bias_ref[d0 + 2 * m] for m in range(WIN_H // 2)], axis=-1)
        s = lax.dot_general(qs, kw, _NT, preferred_element_type=F32) + bias
        o = _softmax_pv(s, vw)
        yb_ref[i * GRID_W:(i + 1) * GRID_W, :] = _pick_heads(o, lh_row, GRID_W)
    y_ref[:, GROUP_W:2 * GROUP_W] = (yb_ref[...] * _silu(bg_ref[0].astype(F32))).astype(BF16)

    zc_halves = [zc_ref.at[0, hf].reshape(GRID_W * ROWS_PER_TILE, LANES) for hf in range(GROUP_W // LANES)]
    zc = jnp.concatenate(
        [jnp.concatenate([zh[pl.ds(i, GRID_W, stride=ROWS_PER_TILE), :] for zh in zc_halves], axis=-1)
         for i in range(ROWS_PER_TILE)], axis=0)
    y_ref[:, 2 * GROUP_W:3 * GROUP_W] = (zc * _silu(cg_ref[0].astype(F32))).astype(BF16)

    o_ref[0] = x_ref[0] + jnp.dot(y_ref[...], wout_ref[...], preferred_element_type=F32)


def _const_spec(shape):
    nd = len(shape)
    return pl.BlockSpec(shape, lambda *_: (0,) * nd)


def _layer_spec(shape, layer):
    nd = len(shape)
    return pl.BlockSpec((None,) + tuple(shape), lambda *_: (layer,) + (0,) * nd)


def _params(dims, flags=None):
    return pltpu.CompilerParams(dimension_semantics=("arbitrary",) * dims, vmem_limit_bytes=VMEM_LIMIT,
                                flags=flags)


def _mem_kv(mem, g, w_kv, kg, ones_bd):
    return pl.pallas_call(
        _mem_kv_kernel,
        grid=(DEPTH, BATCH // MEM_KV_BATCH),
        in_specs=[
            pl.BlockSpec((MEM_KV_BATCH, N_MEM, D_MODEL), lambda l, b: (b, 0, 0)),
            pl.BlockSpec((None, 1, D_MODEL), lambda l, b: (l, 0, 0)),
            pl.BlockSpec((None, D_MODEL, 2 * GROUP_W), lambda l, b: (l, 0, 0)),
            pl.BlockSpec((None, 1, GROUP_W), lambda l, b: (l, 0, 0)),
            _const_spec((GROUP_W, GROUP_W)),
        ],
        out_specs=[pl.BlockSpec((None, MEM_KV_BATCH, N_MEM, GROUP_W), lambda l, b: (l, b, 0, 0))] * 2,
        out_shape=[jax.ShapeDtypeStruct((DEPTH, BATCH, N_MEM, GROUP_W), BF16)] * 2,
        compiler_params=_params(2),
        name="mem_kv",
    )(mem, g, w_kv, kg, ones_bd)


def _in_proj(layer, x2d, g, w_in, head_gains, ones_bd, km, vm, ln_g, ln_b, ws_stack, bs_tab):
    n_tok = x2d.shape[0]
    tiles_per_batch = SEQ // TM_IN
    mem_spec = pl.BlockSpec((None, 1, N_MEM, GROUP_W), lambda i: (layer, i // tiles_per_batch, 0, 0))
    return pl.pallas_call(
        _in_proj_kernel,
        grid=(n_tok // TM_IN,),
        in_specs=[
            pl.BlockSpec((TM_IN, D_MODEL), lambda i: (i, 0)),
            _layer_spec((1, D_MODEL), layer),
            _layer_spec((D_MODEL, IN_COLS), layer),
            _layer_spec((8, GROUP_W), layer),
            _const_spec((GROUP_W, GROUP_W)),
            mem_spec, mem_spec,
            _layer_spec((1, GROUP_W), layer), _layer_spec((1, GROUP_W), layer),
            _layer_spec((N_SUB * CHUNK, CHUNK), layer),
            _layer_spec((CHUNK, GROUP_W), layer),
        ],
        out_specs=[
            pl.BlockSpec((TM_IN, PROJ_COLS), lambda i: (i, 0)),
            pl.BlockSpec((TM_IN, GROUP_W), lambda i: (i, 0)),
            pl.BlockSpec((TM_IN, 2 * GROUP_W), lambda i: (i, 0)),
        ],
        out_shape=[
            jax.ShapeDtypeStruct((n_tok, PROJ_COLS), BF16),
            jax.ShapeDtypeStruct((n_tok, GROUP_W), F32),
            jax.ShapeDtypeStruct((n_tok, 2 * GROUP_W), BF16),
        ],
        compiler_params=_params(1),
        name="in_proj",
    )(x2d, g, w_in, head_gains, ones_bd, km, vm, ln_g, ln_b, ws_stack, bs_tab)


def _fnet(layer, c_in, g1, m2, m3, wf_bd):
    x4 = c_in.reshape(BATCH, GRID_W, GRID_W, GROUP_W)
    n_half = GROUP_W // LANES
    return pl.pallas_call(
        _fnet_kernel,
        grid=(BATCH,),
        in_specs=[pl.BlockSpec((1, GRID_W, GRID_W, GROUP_W), lambda b: (b, 0, 0, 0)),
                  _const_spec(g1.shape), _const_spec(m2.shape), _const_spec(m3.shape),
                  _layer_spec((GROUP_W, GROUP_W), layer)],
        out_specs=pl.BlockSpec((1, n_half, GRID_W, GRID_W, LANES), lambda b: (b, 0, 0, 0, 0)),
        out_shape=jax.ShapeDtypeStruct((BATCH, n_half, GRID_W, GRID_W, LANES), F32),
        scratch_shapes=[pltpu.VMEM((2, GRID_W, GRID_W, GROUP_W), F32)],
        compiler_params=_params(1),
        name="fnet",
    )(x4, g1, m2, m3, wf_bd)


def _mixer(layer, x, proj, yad, zc, w_out, bias_tab):
    def tile_slot(j):
        return pl.BlockSpec((1, TM, GROUP_W), lambda b, t, j=j: (b, t, j))

    def full_slot(j):
        return pl.BlockSpec((1, SEQ, GROUP_W), lambda b, t, j=j: (b, 0, j))

    in_specs = [
        pl.BlockSpec((1, TM, D_MODEL), lambda b, t: (b, t, 0)),
        tile_slot(P_BQ), full_slot(P_BK), full_slot(P_BV), tile_slot(P_BG), tile_slot(P_CG),
        pl.BlockSpec((1, TM, 2 * GROUP_W), lambda b, t: (b, t, 0)),
        pl.BlockSpec((1, GROUP_W // LANES, GRID_W, ROWS_PER_TILE, LANES), lambda b, t: (b, 0, 0, t, 0)),
        _layer_spec((D_MODEL, D_MODEL), layer),
        pl.BlockSpec((2 * WIN_H - 2, N_SUB * GRID_W, 2 * GRID_W), lambda b, t: (0, layer, 0)),
    ]
    return pl.pallas_call(
        _mixer_kernel,
        grid=(BATCH, SEQ // TM),
        in_specs=in_specs,
        out_specs=pl.BlockSpec((1, TM, D_MODEL), lambda b, t: (b, t, 0)),
        out_shape=jax.ShapeDtypeStruct((BATCH, SEQ, D_MODEL), F32),
        scratch_shapes=[pltpu.VMEM((TM, D_MODEL), BF16), pltpu.VMEM((TM, GROUP_W), F32)],
        compiler_params=_params(2),
        name="mixer",
    )(x, *([proj] * 5), yad, zc, w_out, bias_tab)


def _tile_heads(g):
    return jnp.tile(g.astype(F32), (1, N_SUB))


def kernel(x, mem, norm_g, w_in, w_out, gm_ln_g, gm_ln_b, gm_w_s, gm_b_s, na_qn_g, na_kn_g, na_rpb, fn_w,
           mem_norm_g, mem_w_kv, mem_qn_g, mem_kn_g):
    g1_np, m2_np, m3_np = _dft_tables()
    g1 = jnp.asarray(g1_np).astype(BF16)
    m2 = jnp.asarray(m2_np).astype(BF16)
    m3 = jnp.asarray(m3_np).astype(BF16)
    ones_bd = jnp.asarray(_head_mean_matrix()).astype(BF16)
    qk_scale = HEAD_DIM ** -0.5 * LOG2E

    slot_order = [IN_SLOTS.index(name) for name in LOCAL_SLOTS + REST_SLOTS]
    w_in_b = jnp.concatenate([w_in[:, :, j * GROUP_W:(j + 1) * GROUP_W] for j in slot_order], axis=-1).astype(BF16)
    w_out_b = w_out.astype(BF16)
    norm_g3 = norm_g.reshape(DEPTH, 1, D_MODEL)
    head_gains = jnp.stack([_tile_heads(na_qn_g) * qk_scale, _tile_heads(na_kn_g),
                            _tile_heads(mem_qn_g) * qk_scale], axis=1)
    head_gains = jnp.pad(head_gains, ((0, 0), (0, 5), (0, 0)))
    eye = jnp.eye(N_SUB, dtype=fn_w.dtype)
    wf_bd = (fn_w[:, :, :, None, :] * eye[None, :, None, :, None]).reshape(DEPTH, GROUP_W, GROUP_W).astype(BF16)
    ws_stack = gm_w_s.reshape(DEPTH, N_SUB * CHUNK, CHUNK).astype(BF16)
    bs_tab = jnp.repeat(gm_b_s.transpose(0, 2, 1), HEAD_DIM, axis=2)
    ln_g3 = gm_ln_g.reshape(DEPTH, 1, GROUP_W)
    ln_b3 = gm_ln_b.reshape(DEPTH, 1, GROUP_W)
    bias_tab = _natten_bias_table(na_rpb * LOG2E)

    km, vm = _mem_kv(mem, mem_norm_g.reshape(DEPTH, 1, D_MODEL), mem_w_kv.astype(BF16),
                     _tile_heads(mem_kn_g).reshape(DEPTH, 1, GROUP_W), ones_bd)
    for l in range(DEPTH):
        proj, c_in, yad = _in_proj(l, x.reshape(BATCH * SEQ, D_MODEL), norm_g3, w_in_b, head_gains, ones_bd,
                                   km, vm, ln_g3, ln_b3, ws_stack, bs_tab)
        zc = _fnet(l, c_in, g1, m2, m3, wf_bd)
        x = _mixer(l, x, proj.reshape(BATCH, SEQ, PROJ_COLS), yad.reshape(BATCH, SEQ, 2 * GROUP_W), zc,
                   w_out_b, bias_tab)
    return x
```

```python
import functools

import numpy as np
import jax
import jax.numpy as jnp
from jax import lax
from jax.experimental import pallas as pl
from jax.experimental.pallas import tpu as pltpu

D_MODEL = 1024
BATCH = 8
SEQ = 4096
DEPTH = 2
N_MEM = 256
GROUP_W = 256
N_SUB = 4
HEAD_DIM = 64
CHUNK = 128
GRID_W = 64
GRID_H = SEQ // GRID_W
WIN_H = 8
WIN_W = 16
N_IN_SLOTS = 11
IN_COLS = N_IN_SLOTS * GROUP_W
EPS = 1e-6
NEG = -1e30
LOG2E = 1.4426950408889634

IN_SLOTS = ("a_u", "a_v", "a_g", "b_q", "b_k", "b_v", "b_g", "c_in", "c_g", "d_q", "d_g")
TILE_SLOTS = ("a_u", "a_v", "a_g", "b_q", "b_g", "c_g", "d_q", "d_g")
KV_SLOTS = ("b_k", "b_v")
TILE_COLS = len(TILE_SLOTS) * GROUP_W
KV_COLS = len(KV_SLOTS) * GROUP_W

MEM_KV_BATCH = 4
TM = 512
ROWS_PER_TILE = TM // GRID_W
TM_IN = 1024
FFT_G = 8

LANES = 128
F32 = jnp.float32
BF16 = jnp.bfloat16
VMEM_LIMIT = 56 * 1024 * 1024


@functools.lru_cache(maxsize=None)
def _dft_tables():
    g = np.arange(GRID_W // FFT_G)
    j = np.arange(FFT_G)
    k = np.arange(GRID_W)
    n = np.arange(GRID_W)
    s = GRID_W * n[None, None, None, :] + FFT_G * g[:, None, None, None] + j[None, :, None, None]
    phase = (k[None, None, :, None] * s) % SEQ
    ang = 2.0 * np.pi * phase.astype(np.float64) / SEQ
    g1 = np.zeros((len(g), 2, GRID_W, FFT_G, GRID_W, FFT_G), np.float64)
    for jj in range(FFT_G):
        g1[:, 0, :, jj, :, jj] = np.cos(ang[:, jj])
        g1[:, 1, :, jj, :, jj] = np.sin(ang[:, jj])
    g1 = g1.reshape(len(g), 2 * GRID_W * FFT_G, GRID_W * FFT_G)
    ang2 = 2.0 * np.pi * ((k[:, None] * n[None, :]) % GRID_W).astype(np.float64) / GRID_W
    c2, s2 = np.cos(ang2), np.sin(ang2)
    m2 = np.block([[c2, -s2], [s2, c2]])
    d = np.arange(HEAD_DIM)
    ang3 = 2.0 * np.pi * ((d[:, None] * d[None, :]) % HEAD_DIM).astype(np.float64) / HEAD_DIM
    scale = 1.0 / np.sqrt(float(SEQ * HEAD_DIM))
    cd = np.kron(np.eye(N_SUB), np.cos(ang3)) * scale
    sd = np.kron(np.eye(N_SUB), np.sin(ang3)) * scale
    m3 = np.concatenate([cd, -sd], axis=0)
    return g1.astype(np.float32), m2.astype(np.float32), m3.astype(np.float32)


@functools.lru_cache(maxsize=None)
def _head_mean_matrix():
    return np.kron(np.eye(N_SUB), np.full((HEAD_DIM, HEAD_DIM), 1.0 / HEAD_DIM)).astype(np.float32)


def _natten_bias_table(rpb):
    n_lh = rpb.shape[0] * rpb.shape[1]
    rpb = rpb.reshape(n_lh, 2 * WIN_H - 1, 2 * WIN_W - 1).astype(F32)
    c = np.arange(GRID_W)
    col_start = np.clip(c - WIN_W // 2, 0, GRID_W - WIN_W)
    col_ok = (c[None, :] >= col_start[:, None]) & (c[None, :] < col_start[:, None] + WIN_W)
    d_col = c[None, :] - c[:, None] + (WIN_W - 1)
    onehot = (d_col[None, :, :] == np.arange(2 * WIN_W - 1)[:, None, None]).astype(np.float32)
    toe = jnp.einsum("hij,jcw->ihcw", rpb, onehot, precision=lax.Precision.HIGHEST)
    toe = jnp.where(col_ok[None, None, :, :], toe, NEG)
    toe = toe.reshape(2 * WIN_H - 1, n_lh * GRID_W, GRID_W)
    return jnp.concatenate([toe[:-1], toe[1:]], axis=-1)


def _head_rmsnorm(y, gain, ones_bd):
    ms = jnp.dot((y * y).astype(BF16), ones_bd, preferred_element_type=F32)
    return y * lax.rsqrt(ms + EPS) * gain


def _silu(g):
    h = 0.5 * g
    return h + h * jnp.tanh(h)


def _lane_head(rows):
    return lax.broadcasted_iota(jnp.int32, (rows, GROUP_W), 1) // HEAD_DIM


def _stack_heads(q, lane_head):
    qf = q.astype(F32)
    return jnp.concatenate(
        [jnp.where(lane_head == h, qf, 0.0) for h in range(N_SUB)], axis=0).astype(BF16)


def _pick_heads(o, lane_head, rows):
    out = o[0:rows]
    for h in range(1, N_SUB):
        out = jnp.where(lane_head == h, o[h * rows:(h + 1) * rows], out)
    return out


def _softmax_pv(s, v):
    m = jnp.max(s, axis=-1, keepdims=True)
    e = jnp.exp2(s - m)
    l = jnp.sum(e, axis=-1, keepdims=True)
    return jnp.dot(e.astype(BF16), v, preferred_element_type=F32) / l


_NT = (((1,), (1,)), ((), ()))


def _mem_kv_kernel(mem_ref, g_ref, w_ref, kg_ref, ones_ref, k_ref, v_ref):
    m = mem_ref[...].reshape(MEM_KV_BATCH * N_MEM, D_MODEL)
    ms = jnp.mean(m * m, axis=-1, keepdims=True)
    mn = (m * lax.rsqrt(ms + EPS) * g_ref[...]).astype(BF16)
    kv = jnp.dot(mn, w_ref[...], preferred_element_type=F32)
    k = _head_rmsnorm(kv[:, :GROUP_W], kg_ref[...], ones_ref[...])
    k_ref[...] = k.astype(BF16).reshape(MEM_KV_BATCH, N_MEM, GROUP_W)
    v_ref[...] = kv[:, GROUP_W:].astype(BF16).reshape(MEM_KV_BATCH, N_MEM, GROUP_W)


def _in_proj_kernel(x_ref, g_ref, w_ref, hg_ref, ones_ref, tile_ref, kv_ref, cin_ref):
    head_gain_row = {"b_q": 0, "b_k": 1, "d_q": 2}
    x = x_ref[...]
    inv = lax.rsqrt(jnp.mean(x * x, axis=-1, keepdims=True) + EPS)
    h = (x * g_ref[...]).astype(BF16)
    y_all = jnp.dot(h, w_ref[...], preferred_element_type=F32)
    for j, name in enumerate(IN_SLOTS):
        y = y_all[:, j * GROUP_W:(j + 1) * GROUP_W] * inv
        if name in head_gain_row:
            r = head_gain_row[name]
            y = _head_rmsnorm(y, hg_ref[r:r + 1, :], ones_ref[...])
        if name == "c_in":
            cin_ref[...] = y
        elif name in KV_SLOTS:
            col = KV_SLOTS.index(name) * GROUP_W
            kv_ref[:, col:col + GROUP_W] = y.astype(BF16)
        else:
            col = TILE_SLOTS.index(name) * GROUP_W
            tile_ref[:, col:col + GROUP_W] = y.astype(BF16)


def _fnet_kernel(x_ref, g1_ref, m2_ref, m3_ref, wf_ref, z_ref, t_ref):
    half = GRID_W * FFT_G

    def stage1(g, carry):
        n0 = pl.multiple_of(g * FFT_G, FFT_G)
        x = x_ref[0, :, pl.ds(n0, FFT_G), :].reshape(half, GROUP_W).astype(BF16)
        y = jnp.dot(g1_ref[g], x, preferred_element_type=F32)
        t_ref[:, :, pl.ds(n0, FFT_G), :] = y.reshape(2, GRID_W, FFT_G, GROUP_W)
        return carry

    m3w = jnp.dot(m3_ref[...], wf_ref[...], preferred_element_type=F32).astype(BF16)

    def stage2(g, carry):
        k0 = g * FFT_G
        t = jnp.concatenate(
            [jnp.concatenate([t_ref[0, k0 + j], t_ref[1, k0 + j]], axis=0) for j in range(FFT_G)],
            axis=-1).astype(BF16)
        ab = jnp.dot(m2_ref[...], t, preferred_element_type=F32)
        ab = jnp.concatenate(
            [jnp.concatenate([ab[:GRID_W, j * GROUP_W:(j + 1) * GROUP_W],
                              ab[GRID_W:, j * GROUP_W:(j + 1) * GROUP_W]], axis=-1) for j in range(FFT_G)],
            axis=0).astype(BF16)
        zc = jnp.dot(ab, m3w, preferred_element_type=F32)
        zc = zc.reshape(FFT_G, GRID_W, GROUP_W)
        for hf in range(GROUP_W // LANES):
            z_ref[0, hf, pl.ds(k0, FFT_G), :, :] = zc[:, :, hf * LANES:(hf + 1) * LANES]
        return carry

    n_groups = GRID_W // FFT_G
    lax.fori_loop(0, n_groups, stage1, 0, unroll=True)
    lax.fori_loop(0, n_groups, stage2, 0, unroll=True)


def _mixer_kernel(x_ref, tile_ref, kv_ref, zc_ref, km_ref, vm_ref, wout_ref, lng_ref, lnb_ref, ws_ref, bs_ref,
                  bias_ref, o_ref, y_ref, yb_ref):
    t = pl.program_id(1)

    def slot(name, rows=slice(None)):
        col = TILE_SLOTS.index(name) * GROUP_W
        return tile_ref[0, rows, col:col + GROUP_W]

    k_col, v_col = (KV_SLOTS.index(name) * GROUP_W for name in ("b_k", "b_v"))

    lh_tile = _lane_head(TM)
    qs = _stack_heads(slot("d_q"), lh_tile)
    s = lax.dot_general(qs, km_ref[0], _NT, preferred_element_type=F32)
    yd = _pick_heads(_softmax_pv(s, vm_ref[0]), lh_tile, TM)
    y_ref[:, 3 * GROUP_W:4 * GROUP_W] = (yd * _silu(slot("d_g").astype(F32))).astype(BF16)

    lh_row = _lane_head(GRID_W)
    for i in range(ROWS_PER_TILE):
        r = t * ROWS_PER_TILE + i
        rs = jnp.clip(r - WIN_H // 2, 0, GRID_H - WIN_H)
        k0 = pl.multiple_of(rs * GRID_W, GRID_W)
        qs = _stack_heads(slot("b_q", slice(i * GRID_W, (i + 1) * GRID_W)), lh_row)
        kw = kv_ref[0, pl.ds(k0, WIN_H * GRID_W), k_col:k_col + GROUP_W]
        vw = kv_ref[0, pl.ds(k0, WIN_H * GRID_W), v_col:v_col + GROUP_W]
        d0 = (WIN_H - 1) - (r - rs)
        bias = jnp.concatenate([bias_ref[d0 + 2 * m] for m in range(WIN_H // 2)], axis=-1)
        s = lax.dot_general(qs, kw, _NT, preferred_element_type=F32) + bias
        o = _softmax_pv(s, vw)
        yb_ref[i * GRID_W:(i + 1) * GRID_W, :] = _pick_heads(o, lh_row, GRID_W)
    y_ref[:, GROUP_W:2 * GROUP_W] = (yb_ref[...] * _silu(slot("b_g").astype(F32))).astype(BF16)

    lh_chunk = _lane_head(CHUNK)
    for c in range(TM // CHUNK):
        rows = slice(c * CHUNK, (c + 1) * CHUNK)
        v = slot("a_v", rows).astype(F32)
        mu = jnp.mean(v, axis=-1, keepdims=True)
        var = jnp.mean(jnp.square(v - mu), axis=-1, keepdims=True)
        vn = (v - mu) * lax.rsqrt(var + EPS) * lng_ref[...] + lnb_ref[...]
        s = jnp.dot(ws_ref[...], _stack_heads(vn, lh_chunk), preferred_element_type=F32) + bs_ref[...]
        ya = slot("a_u", rows).astype(F32) * s * _silu(slot("a_g", rows).astype(F32))
        y_ref[rows, 0:GROUP_W] = ya.astype(BF16)

    zc_halves = [zc_ref.at[0, hf].reshape(GRID_W * ROWS_PER_TILE, LANES) for hf in range(GROUP_W // LANES)]
    zc = jnp.concatenate(
        [jnp.concatenate([zh[pl.ds(i, GRID_W, stride=ROWS_PER_TILE), :] for zh in zc_halves], axis=-1)
         for i in range(ROWS_PER_TILE)], axis=0)
    y_ref[:, 2 * GROUP_W:3 * GROUP_W] = (zc * _silu(slot("c_g").astype(F32))).astype(BF16)

    o_ref[0] = x_ref[0] + jnp.dot(y_ref[...], wout_ref[...], preferred_element_type=F32)


def _const_spec(shape):
    nd = len(shape)
    return pl.BlockSpec(shape, lambda *_: (0,) * nd)


def _layer_spec(shape, layer):
    nd = len(shape)
    return pl.BlockSpec((None,) + tuple(shape), lambda *_: (layer,) + (0,) * nd)


def _params(dims):
    return pltpu.CompilerParams(dimension_semantics=("arbitrary",) * dims, vmem_limit_bytes=VMEM_LIMIT)


def _mem_kv(mem, g, w_kv, kg, ones_bd):
    return pl.pallas_call(
        _mem_kv_kernel,
        grid=(DEPTH, BATCH // MEM_KV_BATCH),
        in_specs=[
            pl.BlockSpec((MEM_KV_BATCH, N_MEM, D_MODEL), lambda l, b: (b, 0, 0)),
            pl.BlockSpec((None, 1, D_MODEL), lambda l, b: (l, 0, 0)),
            pl.BlockSpec((None, D_MODEL, 2 * GROUP_W), lambda l, b: (l, 0, 0)),
            pl.BlockSpec((None, 1, GROUP_W), lambda l, b: (l, 0, 0)),
            _const_spec((GROUP_W, GROUP_W)),
        ],
        out_specs=[pl.BlockSpec((None, MEM_KV_BATCH, N_MEM, GROUP_W), lambda l, b: (l, b, 0, 0))] * 2,
        out_shape=[jax.ShapeDtypeStruct((DEPTH, BATCH, N_MEM, GROUP_W), BF16)] * 2,
        compiler_params=_params(2),
        name="mem_kv",
    )(mem, g, w_kv, kg, ones_bd)


def _in_proj(layer, x2d, g, w_in, head_gains, ones_bd):
    n_tok = x2d.shape[0]
    return pl.pallas_call(
        _in_proj_kernel,
        grid=(n_tok // TM_IN,),
        in_specs=[
            pl.BlockSpec((TM_IN, D_MODEL), lambda i: (i, 0)),
            _layer_spec((1, D_MODEL), layer),
            _layer_spec((D_MODEL, IN_COLS), layer),
            _layer_spec((8, GROUP_W), layer),
            _const_spec((GROUP_W, GROUP_W)),
        ],
        out_specs=[
            pl.BlockSpec((TM_IN, TILE_COLS), lambda i: (i, 0)),
            pl.BlockSpec((TM_IN, KV_COLS), lambda i: (i, 0)),
            pl.BlockSpec((TM_IN, GROUP_W), lambda i: (i, 0)),
        ],
        out_shape=[
            jax.ShapeDtypeStruct((n_tok, TILE_COLS), BF16),
            jax.ShapeDtypeStruct((n_tok, KV_COLS), BF16),
            jax.ShapeDtypeStruct((n_tok, GROUP_W), F32),
        ],
        compiler_params=_params(1),
        name="in_proj",
    )(x2d, g, w_in, head_gains, ones_bd)


def _fnet(layer, c_in, g1, m2, m3, wf_bd):
    x4 = c_in.reshape(BATCH, GRID_W, GRID_W, GROUP_W)
    n_half = GROUP_W // LANES
    return pl.pallas_call(
        _fnet_kernel,
        grid=(BATCH,),
        in_specs=[pl.BlockSpec((1, GRID_W, GRID_W, GROUP_W), lambda b: (b, 0, 0, 0)),
                  _const_spec(g1.shape), _const_spec(m2.shape), _const_spec(m3.shape),
                  _layer_spec((GROUP_W, GROUP_W), layer)],
        out_specs=pl.BlockSpec((1, n_half, GRID_W, GRID_W, LANES), lambda b: (b, 0, 0, 0, 0)),
        out_shape=jax.ShapeDtypeStruct((BATCH, n_half, GRID_W, GRID_W, LANES), F32),
        scratch_shapes=[pltpu.VMEM((2, GRID_W, GRID_W, GROUP_W), F32)],
        compiler_params=_params(1),
        name="fnet",
    )(x4, g1, m2, m3, wf_bd)


def _mixer(layer, x, tiles, kv, zc, km, vm, w_out, ln_g, ln_b, ws_cat, bs_tab, bias_tab):
    mem_spec = pl.BlockSpec((None, 1, N_MEM, GROUP_W), lambda b, t: (layer, b, 0, 0))
    in_specs = [
        pl.BlockSpec((1, TM, D_MODEL), lambda b, t: (b, t, 0)),
        pl.BlockSpec((1, TM, TILE_COLS), lambda b, t: (b, t, 0)),
        pl.BlockSpec((1, SEQ, KV_COLS), lambda b, t: (b, 0, 0)),
        pl.BlockSpec((1, GROUP_W // LANES, GRID_W, ROWS_PER_TILE, LANES), lambda b, t: (b, 0, 0, t, 0)),
        mem_spec, mem_spec,
        _layer_spec((D_MODEL, D_MODEL), layer),
        _layer_spec((1, GROUP_W), layer), _layer_spec((1, GROUP_W), layer),
        _layer_spec((CHUNK, N_SUB * CHUNK), layer),
        _layer_spec((CHUNK, GROUP_W), layer),
        pl.BlockSpec((2 * WIN_H - 2, N_SUB * GRID_W, 2 * GRID_W), lambda b, t: (0, layer, 0)),
    ]
    return pl.pallas_call(
        _mixer_kernel,
        grid=(BATCH, SEQ // TM),
        in_specs=in_specs,
        out_specs=pl.BlockSpec((1, TM, D_MODEL), lambda b, t: (b, t, 0)),
        out_shape=jax.ShapeDtypeStruct((BATCH, SEQ, D_MODEL), F32),
        scratch_shapes=[pltpu.VMEM((TM, D_MODEL), BF16), pltpu.VMEM((TM, GROUP_W), F32)],
        compiler_params=_params(2),
        name="mixer",
    )(x, tiles, kv, zc, km, vm, w_out, ln_g, ln_b, ws_cat, bs_tab, bias_tab)


def _tile_heads(g):
    return jnp.tile(g.astype(F32), (1, N_SUB))


def kernel(x, mem, norm_g, w_in, w_out, gm_ln_g, gm_ln_b, gm_w_s, gm_b_s, na_qn_g, na_kn_g, na_rpb, fn_w,
           mem_norm_g, mem_w_kv, mem_qn_g, mem_kn_g):
    g1_np, m2_np, m3_np = _dft_tables()
    g1 = jnp.asarray(g1_np).astype(BF16)
    m2 = jnp.asarray(m2_np).astype(BF16)
    m3 = jnp.asarray(m3_np).astype(BF16)
    ones_bd = jnp.asarray(_head_mean_matrix()).astype(BF16)
    qk_scale = HEAD_DIM ** -0.5 * LOG2E

    w_in_b = w_in.astype(BF16)
    w_out_b = w_out.astype(BF16)
    norm_g3 = norm_g.reshape(DEPTH, 1, D_MODEL)
    head_gains = jnp.stack([_tile_heads(na_qn_g) * qk_scale, _tile_heads(na_kn_g),
                            _tile_heads(mem_qn_g) * qk_scale], axis=1)
    head_gains = jnp.pad(head_gains, ((0, 0), (0, 5), (0, 0)))
    eye = jnp.eye(N_SUB, dtype=fn_w.dtype)
    wf_bd = (fn_w[:, :, :, None, :] * eye[None, :, None, :, None]).reshape(DEPTH, GROUP_W, GROUP_W).astype(BF16)
    ws_cat = gm_w_s.transpose(0, 2, 1, 3).reshape(DEPTH, CHUNK, N_SUB * CHUNK).astype(BF16)
    bs_tab = jnp.repeat(gm_b_s.transpose(0, 2, 1), HEAD_DIM, axis=2)
    ln_g3 = gm_ln_g.reshape(DEPTH, 1, GROUP_W)
    ln_b3 = gm_ln_b.reshape(DEPTH, 1, GROUP_W)
    bias_tab = _natten_bias_table(na_rpb * LOG2E)

    km, vm = _mem_kv(mem, mem_norm_g.reshape(DEPTH, 1, D_MODEL), mem_w_kv.astype(BF16),
                     _tile_heads(mem_kn_g).reshape(DEPTH, 1, GROUP_W), ones_bd)
    for l in range(DEPTH):
        tiles, kv, c_in = _in_proj(l, x.reshape(BATCH * SEQ, D_MODEL), norm_g3, w_in_b, head_gains, ones_bd)
        zc = _fnet(l, c_in, g1, m2, m3, wf_bd)
        x = _mixer(l, x, tiles.reshape(BATCH, SEQ, TILE_COLS), kv.reshape(BATCH, SEQ, KV_COLS), zc, km, vm,
                   w_out_b, ln_g3, ln_b3, ws_cat, bs_tab, bias_tab)
    return x
```

```python
import functools

import numpy as np
import jax
import jax.numpy as jnp
from jax import lax
from jax.experimental import pallas as pl
from jax.experimental.pallas import tpu as pltpu

D_MODEL = 1024
BATCH = 8
SEQ = 4096
DEPTH = 2
N_MEM = 256
GROUP_W = 256
N_SUB = 4
HEAD_DIM = 64
CHUNK = 128
GRID_W = 64
GRID_H = SEQ // GRID_W
WIN_H = 8
WIN_W = 16
N_IN_SLOTS = 11
IN_COLS = N_IN_SLOTS * GROUP_W
EPS = 1e-6
NEG = -1e30
LOG2E = 1.4426950408889634

IN_SLOTS = ("a_u", "a_v", "a_g", "b_q", "b_k", "b_v", "b_g", "c_in", "c_g", "d_q", "d_g")
TILE_SLOTS = ("a_u", "a_v", "a_g", "b_q", "b_g", "c_g", "d_q", "d_g")
KV_SLOTS = ("b_k", "b_v")
TILE_COLS = len(TILE_SLOTS) * GROUP_W
KV_COLS = len(KV_SLOTS) * GROUP_W

MEM_KV_BATCH = 4
TM = 512
ROWS_PER_TILE = TM // GRID_W
TM_IN = 1024
FFT_G = 8

LANES = 128
F32 = jnp.float32
BF16 = jnp.bfloat16
VMEM_LIMIT = 56 * 1024 * 1024


@functools.lru_cache(maxsize=None)
def _dft_tables():
    g = np.arange(GRID_W // FFT_G)
    j = np.arange(FFT_G)
    k = np.arange(GRID_W)
    n = np.arange(GRID_W)
    s = GRID_W * n[None, None, None, :] + FFT_G * g[:, None, None, None] + j[None, :, None, None]
    phase = (k[None, None, :, None] * s) % SEQ
    ang = 2.0 * np.pi * phase.astype(np.float64) / SEQ
    g1 = np.zeros((len(g), 2, GRID_W, FFT_G, GRID_W, FFT_G), np.float64)
    for jj in range(FFT_G):
        g1[:, 0, :, jj, :, jj] = np.cos(ang[:, jj])
        g1[:, 1, :, jj, :, jj] = np.sin(ang[:, jj])
    g1 = g1.reshape(len(g), 2 * GRID_W * FFT_G, GRID_W * FFT_G)
    ang2 = 2.0 * np.pi * ((k[:, None] * n[None, :]) % GRID_W).astype(np.float64) / GRID_W
    c2, s2 = np.cos(ang2), np.sin(ang2)
    m2 = np.block([[c2, -s2], [s2, c2]])
    d = np.arange(HEAD_DIM)
    ang3 = 2.0 * np.pi * ((d[:, None] * d[None, :]) % HEAD_DIM).astype(np.float64) / HEAD_DIM
    scale = 1.0 / np.sqrt(float(SEQ * HEAD_DIM))
    cd = np.kron(np.eye(N_SUB), np.cos(ang3)) * scale
    sd = np.kron(np.eye(N_SUB), np.sin(ang3)) * scale
    m3 = np.concatenate([cd, -sd], axis=0)
    return g1.astype(np.float32), m2.astype(np.float32), m3.astype(np.float32)


@functools.lru_cache(maxsize=None)
def _head_mean_matrix():
    return np.kron(np.eye(N_SUB), np.full((HEAD_DIM, HEAD_DIM), 1.0 / HEAD_DIM)).astype(np.float32)


def _natten_bias_table(rpb):
    n_lh = rpb.shape[0] * rpb.shape[1]
    rpb = rpb.reshape(n_lh, 2 * WIN_H - 1, 2 * WIN_W - 1).astype(F32)
    c = np.arange(GRID_W)
    col_start = np.clip(c - WIN_W // 2, 0, GRID_W - WIN_W)
    col_ok = (c[None, :] >= col_start[:, None]) & (c[None, :] < col_start[:, None] + WIN_W)
    d_col = c[None, :] - c[:, None] + (WIN_W - 1)
    onehot = (d_col[None, :, :] == np.arange(2 * WIN_W - 1)[:, None, None]).astype(np.float32)
    toe = jnp.einsum("hij,jcw->ihcw", rpb, onehot, precision=lax.Precision.HIGHEST)
    toe = jnp.where(col_ok[None, None, :, :], toe, NEG)
    toe = toe.reshape(2 * WIN_H - 1, n_lh * GRID_W, GRID_W)
    return jnp.concatenate([toe[:-1], toe[1:]], axis=-1)


def _head_rmsnorm(y, gain, ones_bd):
    ms = jnp.dot((y * y).astype(BF16), ones_bd, preferred_element_type=F32)
    return y * lax.rsqrt(ms + EPS) * gain


def _silu(g):
    h = 0.5 * g
    return h + h * jnp.tanh(h)


def _lane_head(rows):
    return lax.broadcasted_iota(jnp.int32, (rows, GROUP_W), 1) // HEAD_DIM


def _stack_heads(q, lane_head):
    qf = q.astype(F32)
    return jnp.concatenate(
        [jnp.where(lane_head == h, qf, 0.0) for h in range(N_SUB)], axis=0).astype(BF16)


def _pick_heads(o, lane_head, rows):
    out = o[0:rows]
    for h in range(1, N_SUB):
        out = jnp.where(lane_head == h, o[h * rows:(h + 1) * rows], out)
    return out


def _softmax_pv(s, v):
    m = jnp.max(s, axis=-1, keepdims=True)
    e = jnp.exp2(s - m)
    l = jnp.sum(e, axis=-1, keepdims=True)
    return jnp.dot(e.astype(BF16), v, preferred_element_type=F32) / l


_NT = (((1,), (1,)), ((), ()))


def _mem_kv_kernel(mem_ref, g_ref, w_ref, kg_ref, ones_ref, k_ref, v_ref):
    m = mem_ref[...].reshape(MEM_KV_BATCH * N_MEM, D_MODEL)
    ms = jnp.mean(m * m, axis=-1, keepdims=True)
    mn = (m * lax.rsqrt(ms + EPS) * g_ref[...]).astype(BF16)
    kv = jnp.dot(mn, w_ref[...], preferred_element_type=F32)
    k = _head_rmsnorm(kv[:, :GROUP_W], kg_ref[...], ones_ref[...])
    k_ref[...] = k.astype(BF16).reshape(MEM_KV_BATCH, N_MEM, GROUP_W)
    v_ref[...] = kv[:, GROUP_W:].astype(BF16).reshape(MEM_KV_BATCH, N_MEM, GROUP_W)


def _in_proj_kernel(x_ref, g_ref, w_ref, hg_ref, ones_ref, tile_ref, kv_ref, cin_ref):
    head_gain_row = {"b_q": 0, "b_k": 1, "d_q": 2}
    x = x_ref[...]
    inv = lax.rsqrt(jnp.mean(x * x, axis=-1, keepdims=True) + EPS)
    h = (x * g_ref[...]).astype(BF16)
    y_all = jnp.dot(h, w_ref[...], preferred_element_type=F32)
    for j, name in enumerate(IN_SLOTS):
        y = y_all[:, j * GROUP_W:(j + 1) * GROUP_W] * inv
        if name in head_gain_row:
            r = head_gain_row[name]
            y = _head_rmsnorm(y, hg_ref[r:r + 1, :], ones_ref[...])
        if name == "c_in":
            cin_ref[...] = y
        elif name in KV_SLOTS:
            col = KV_SLOTS.index(name) * GROUP_W
            kv_ref[:, col:col + GROUP_W] = y.astype(BF16)
        else:
            col = TILE_SLOTS.index(name) * GROUP_W
            tile_ref[:, col:col + GROUP_W] = y.astype(BF16)


def _fnet_kernel(x_ref, g1_ref, m2_ref, m3_ref, wf_ref, z_ref, t_ref):
    half = GRID_W * FFT_G

    def stage1(g, carry):
        n0 = pl.multiple_of(g * FFT_G, FFT_G)
        x = x_ref[0, :, pl.ds(n0, FFT_G), :].reshape(half, GROUP_W).astype(BF16)
        y = jnp.dot(g1_ref[g], x, preferred_element_type=F32)
        t_ref[:, :, pl.ds(n0, FFT_G), :] = y.reshape(2, GRID_W, FFT_G, GROUP_W)
        return carry

    m3w = jnp.dot(m3_ref[...], wf_ref[...], preferred_element_type=F32).astype(BF16)

    def stage2(g, carry):
        k0 = g * FFT_G
        t = jnp.concatenate(
            [jnp.concatenate([t_ref[0, k0 + j], t_ref[1, k0 + j]], axis=0) for j in range(FFT_G)],
            axis=-1).astype(BF16)
        ab = jnp.dot(m2_ref[...], t, preferred_element_type=F32)
        ab = jnp.concatenate(
            [jnp.concatenate([ab[:GRID_W, j * GROUP_W:(j + 1) * GROUP_W],
                              ab[GRID_W:, j * GROUP_W:(j + 1) * GROUP_W]], axis=-1) for j in range(FFT_G)],
            axis=0).astype(BF16)
        zc = jnp.dot(ab, m3w, preferred_element_type=F32)
        zc = zc.reshape(FFT_G, GRID_W, GROUP_W)
        for hf in range(GROUP_W // LANES):
            z_ref[0, hf, pl.ds(k0, FFT_G), :, :] = zc[:, :, hf * LANES:(hf + 1) * LANES]
        return carry

    n_groups = GRID_W // FFT_G
    lax.fori_loop(0, n_groups, stage1, 0, unroll=True)
    lax.fori_loop(0, n_groups, stage2, 0, unroll=True)


def _mixer_kernel(x_ref, tile_ref, kv_ref, zc_ref, km_ref, vm_ref, wout_ref, lng_ref, lnb_ref, ws_ref, bs_ref,
                  bias_ref, o_ref, y_ref, yb_ref):
    t = pl.program_id(1)

    def slot(name, rows=slice(None)):
        col = TILE_SLOTS.index(name) * GROUP_W
        return tile_ref[0, rows, col:col + GROUP_W]

    k_col, v_col = (KV_SLOTS.index(name) * GROUP_W for name in ("b_k", "b_v"))

    lh_tile = _lane_head(TM)
    qs = _stack_heads(slot("d_q"), lh_tile)
    s = lax.dot_general(qs, km_ref[0], _NT, preferred_element_type=F32)
    yd = _pick_heads(_softmax_pv(s, vm_ref[0]), lh_tile, TM)
    y_ref[:, 3 * GROUP_W:4 * GROUP_W] = yd.astype(BF16) * _silu(slot("d_g"))

    lh_row = _lane_head(GRID_W)
    for i in range(ROWS_PER_TILE):
        r = t * ROWS_PER_TILE + i
        rs = jnp.clip(r - WIN_H // 2, 0, GRID_H - WIN_H)
        k0 = pl.multiple_of(rs * GRID_W, GRID_W)
        qs = _stack_heads(slot("b_q", slice(i * GRID_W, (i + 1) * GRID_W)), lh_row)
        kw = kv_ref[0, pl.ds(k0, WIN_H * GRID_W), k_col:k_col + GROUP_W]
        vw = kv_ref[0, pl.ds(k0, WIN_H * GRID_W), v_col:v_col + GROUP_W]
        d0 = (WIN_H - 1) - (r - rs)
        bias = jnp.concatenate([bias_ref[d0 + 2 * m] for m in range(WIN_H // 2)], axis=-1)
        s = lax.dot_general(qs, kw, _NT, preferred_element_type=F32) + bias
        o = _softmax_pv(s, vw)
        yb_ref[i * GRID_W:(i + 1) * GRID_W, :] = _pick_heads(o, lh_row, GRID_W)
    y_ref[:, GROUP_W:2 * GROUP_W] = yb_ref[...].astype(BF16) * _silu(slot("b_g"))

    lh_chunk = _lane_head(CHUNK)
    for c in range(TM // CHUNK):
        rows = slice(c * CHUNK, (c + 1) * CHUNK)
        v = slot("a_v", rows).astype(F32)
        mu = jnp.mean(v, axis=-1, keepdims=True)
        var = jnp.mean(jnp.square(v - mu), axis=-1, keepdims=True)
        vn = (v - mu) * lax.rsqrt(var + EPS) * lng_ref[...] + lnb_ref[...]
        s = jnp.dot(ws_ref[...], _stack_heads(vn, lh_chunk), preferred_element_type=F32) + bs_ref[...]
        ya = (slot("a_u", rows).astype(F32) * s).astype(BF16) * _silu(slot("a_g", rows))
        y_ref[rows, 0:GROUP_W] = ya

    zc_halves = [zc_ref.at[0, hf].reshape(GRID_W * ROWS_PER_TILE, LANES) for hf in range(GROUP_W // LANES)]
    zc = jnp.concatenate(
        [jnp.concatenate([zh[pl.ds(i, GRID_W, stride=ROWS_PER_TILE), :] for zh in zc_halves], axis=-1)
         for i in range(ROWS_PER_TILE)], axis=0)
    y_ref[:, 2 * GROUP_W:3 * GROUP_W] = zc.astype(BF16) * _silu(slot("c_g"))

    o_ref[0] = x_ref[0] + jnp.dot(y_ref[...], wout_ref[...], preferred_element_type=F32)


def _const_spec(shape):
    nd = len(shape)
    return pl.BlockSpec(shape, lambda *_: (0,) * nd)


def _layer_spec(shape, layer):
    nd = len(shape)
    return pl.BlockSpec((None,) + tuple(shape), lambda *_: (layer,) + (0,) * nd)


def _params(dims):
    return pltpu.CompilerParams(dimension_semantics=("arbitrary",) * dims, vmem_limit_bytes=VMEM_LIMIT)


def _mem_kv(mem, g, w_kv, kg, ones_bd):
    return pl.pallas_call(
        _mem_kv_kernel,
        grid=(DEPTH, BATCH // MEM_KV_BATCH),
        in_specs=[
            pl.BlockSpec((MEM_KV_BATCH, N_MEM, D_MODEL), lambda l, b: (b, 0, 0)),
            pl.BlockSpec((None, 1, D_MODEL), lambda l, b: (l, 0, 0)),
            pl.BlockSpec((None, D_MODEL, 2 * GROUP_W), lambda l, b: (l, 0, 0)),
            pl.BlockSpec((None, 1, GROUP_W), lambda l, b: (l, 0, 0)),
            _const_spec((GROUP_W, GROUP_W)),
        ],
        out_specs=[pl.BlockSpec((None, MEM_KV_BATCH, N_MEM, GROUP_W), lambda l, b: (l, b, 0, 0))] * 2,
        out_shape=[jax.ShapeDtypeStruct((DEPTH, BATCH, N_MEM, GROUP_W), BF16)] * 2,
        compiler_params=_params(2),
        name="mem_kv",
    )(mem, g, w_kv, kg, ones_bd)


def _in_proj(layer, x2d, g, w_in, head_gains, ones_bd):
    n_tok = x2d.shape[0]
    return pl.pallas_call(
        _in_proj_kernel,
        grid=(n_tok // TM_IN,),
        in_specs=[
            pl.BlockSpec((TM_IN, D_MODEL), lambda i: (i, 0)),
            _layer_spec((1, D_MODEL), layer),
            _layer_spec((D_MODEL, IN_COLS), layer),
            _layer_spec((8, GROUP_W), layer),
            _const_spec((GROUP_W, GROUP_W)),
        ],
        out_specs=[
            pl.BlockSpec((TM_IN, TILE_COLS), lambda i: (i, 0)),
            pl.BlockSpec((TM_IN, KV_COLS), lambda i: (i, 0)),
            pl.BlockSpec((TM_IN, GROUP_W), lambda i: (i, 0)),
        ],
        out_shape=[
            jax.ShapeDtypeStruct((n_tok, TILE_COLS), BF16),
            jax.ShapeDtypeStruct((n_tok, KV_COLS), BF16),
            jax.ShapeDtypeStruct((n_tok, GROUP_W), F32),
        ],
        compiler_params=_params(1),
        name="in_proj",
    )(x2d, g, w_in, head_gains, ones_bd)


def _fnet(layer, c_in, g1, m2, m3, wf_bd):
    x4 = c_in.reshape(BATCH, GRID_W, GRID_W, GROUP_W)
    n_half = GROUP_W // LANES
    return pl.pallas_call(
        _fnet_kernel,
        grid=(BATCH,),
        in_specs=[pl.BlockSpec((1, GRID_W, GRID_W, GROUP_W), lambda b: (b, 0, 0, 0)),
                  _const_spec(g1.shape), _const_spec(m2.shape), _const_spec(m3.shape),
                  _layer_spec((GROUP_W, GROUP_W), layer)],
        out_specs=pl.BlockSpec((1, n_half, GRID_W, GRID_W, LANES), lambda b: (b, 0, 0, 0, 0)),
        out_shape=jax.ShapeDtypeStruct((BATCH, n_half, GRID_W, GRID_W, LANES), F32),
        scratch_shapes=[pltpu.VMEM((2, GRID_W, GRID_W, GROUP_W), F32)],
        compiler_params=_params(1),
        name="fnet",
    )(x4, g1, m2, m3, wf_bd)


def _mixer(layer, x, tiles, kv, zc, km, vm, w_out, ln_g, ln_b, ws_cat, bs_tab, bias_tab):
    mem_spec = pl.BlockSpec((None, 1, N_MEM, GROUP_W), lambda b, t: (layer, b, 0, 0))
    in_specs = [
        pl.BlockSpec((1, TM, D_MODEL), lambda b, t: (b, t, 0)),
        pl.BlockSpec((1, TM, TILE_COLS), lambda b, t: (b, t, 0)),
        pl.BlockSpec((1, SEQ, KV_COLS), lambda b, t: (b, 0, 0)),
        pl.BlockSpec((1, GROUP_W // LANES, GRID_W, ROWS_PER_TILE, LANES), lambda b, t: (b, 0, 0, t, 0)),
        mem_spec, mem_spec,
        _layer_spec((D_MODEL, D_MODEL), layer),
        _layer_spec((1, GROUP_W), layer), _layer_spec((1, GROUP_W), layer),
        _layer_spec((CHUNK, N_SUB * CHUNK), layer),
        _layer_spec((CHUNK, GROUP_W), layer),
        pl.BlockSpec((2 * WIN_H - 2, N_SUB * GRID_W, 2 * GRID_W), lambda b, t: (0, layer, 0)),
    ]
    return pl.pallas_call(
        _mixer_kernel,
        grid=(BATCH, SEQ // TM),
        in_specs=in_specs,
        out_specs=pl.BlockSpec((1, TM, D_MODEL), lambda b, t: (b, t, 0)),
        out_shape=jax.ShapeDtypeStruct((BATCH, SEQ, D_MODEL), F32),
        scratch_shapes=[pltpu.VMEM((TM, D_MODEL), BF16), pltpu.VMEM((TM, GROUP_W), F32)],
        compiler_params=_params(2),
        name="mixer",
    )(x, tiles, kv, zc, km, vm, w_out, ln_g, ln_b, ws_cat, bs_tab, bias_tab)


def _tile_heads(g):
    return jnp.tile(g.astype(F32), (1, N_SUB))


def kernel(x, mem, norm_g, w_in, w_out, gm_ln_g, gm_ln_b, gm_w_s, gm_b_s, na_qn_g, na_kn_g, na_rpb, fn_w,
           mem_norm_g, mem_w_kv, mem_qn_g, mem_kn_g):
    g1_np, m2_np, m3_np = _dft_tables()
    g1 = jnp.asarray(g1_np).astype(BF16)
    m2 = jnp.asarray(m2_np).astype(BF16)
    m3 = jnp.asarray(m3_np).astype(BF16)
    ones_bd = jnp.asarray(_head_mean_matrix()).astype(BF16)
    qk_scale = HEAD_DIM ** -0.5 * LOG2E

    w_in_b = w_in.astype(BF16)
    w_out_b = w_out.astype(BF16)
    norm_g3 = norm_g.reshape(DEPTH, 1, D_MODEL)
    head_gains = jnp.stack([_tile_heads(na_qn_g) * qk_scale, _tile_heads(na_kn_g),
                            _tile_heads(mem_qn_g) * qk_scale], axis=1)
    head_gains = jnp.pad(head_gains, ((0, 0), (0, 5), (0, 0)))
    eye = jnp.eye(N_SUB, dtype=fn_w.dtype)
    wf_bd = (fn_w[:, :, :, None, :] * eye[None, :, None, :, None]).reshape(DEPTH, GROUP_W, GROUP_W).astype(BF16)
    ws_cat = gm_w_s.transpose(0, 2, 1, 3).reshape(DEPTH, CHUNK, N_SUB * CHUNK).astype(BF16)
    bs_tab = jnp.repeat(gm_b_s.transpose(0, 2, 1), HEAD_DIM, axis=2)
    ln_g3 = gm_ln_g.reshape(DEPTH, 1, GROUP_W)
    ln_b3 = gm_ln_b.reshape(DEPTH, 1, GROUP_W)
    bias_tab = _natten_bias_table(na_rpb * LOG2E)

    km, vm = _mem_kv(mem, mem_norm_g.reshape(DEPTH, 1, D_MODEL), mem_w_kv.astype(BF16),
                     _tile_heads(mem_kn_g).reshape(DEPTH, 1, GROUP_W), ones_bd)
    for l in range(DEPTH):
        tiles, kv, c_in = _in_proj(l, x.reshape(BATCH * SEQ, D_MODEL), norm_g3, w_in_b, head_gains, ones_bd)
        zc = _fnet(l, c_in, g1, m2, m3, wf_bd)
        x = _mixer(l, x, tiles.reshape(BATCH, SEQ, TILE_COLS), kv.reshape(BATCH, SEQ, KV_COLS), zc, km, vm,
                   w_out_b, ln_g3, ln_b3, ws_cat, bs_tab, bias_tab)
    return x
```

```python
import functools

import numpy as np
import jax
import jax.numpy as jnp
from jax import lax
from jax.experimental import pallas as pl
from jax.experimental.pallas import tpu as pltpu

D_MODEL = 1024
BATCH = 8
SEQ = 4096
DEPTH = 2
N_MEM = 256
GROUP_W = 256
N_SUB = 4
HEAD_DIM = 64
CHUNK = 128
GRID_W = 64
GRID_H = SEQ // GRID_W
WIN_H = 8
WIN_W = 16
N_IN_SLOTS = 11
IN_COLS = N_IN_SLOTS * GROUP_W
EPS = 1e-6
NEG = -1e30
LOG2E = 1.4426950408889634

IN_SLOTS = ("a_u", "a_v", "a_g", "b_q", "b_k", "b_v", "b_g", "c_in", "c_g", "d_q", "d_g")
TILE_SLOTS = ("a_u", "a_v", "a_g", "b_q", "b_g", "c_g", "d_q", "d_g")
KV_SLOTS = ("b_k", "b_v")
TILE_COLS = len(TILE_SLOTS) * GROUP_W
KV_COLS = len(KV_SLOTS) * GROUP_W

MEM_KV_BATCH = 4
TM = 512
ROWS_PER_TILE = TM // GRID_W
TM_IN = 1024
FFT_G = 8

LANES = 128
F32 = jnp.float32
BF16 = jnp.bfloat16
VMEM_LIMIT = 56 * 1024 * 1024


@functools.lru_cache(maxsize=None)
def _dft_tables():
    g = np.arange(GRID_W // FFT_G)
    j = np.arange(FFT_G)
    k = np.arange(GRID_W)
    n = np.arange(GRID_W)
    s = GRID_W * n[None, None, None, :] + FFT_G * g[:, None, None, None] + j[None, :, None, None]
    phase = (k[None, None, :, None] * s) % SEQ
    ang = 2.0 * np.pi * phase.astype(np.float64) / SEQ
    g1 = np.zeros((len(g), 2, GRID_W, FFT_G, GRID_W, FFT_G), np.float64)
    for jj in range(FFT_G):
        g1[:, 0, :, jj, :, jj] = np.cos(ang[:, jj])
        g1[:, 1, :, jj, :, jj] = np.sin(ang[:, jj])
    g1 = g1.reshape(len(g), 2 * GRID_W * FFT_G, GRID_W * FFT_G)
    ang2 = 2.0 * np.pi * ((k[:, None] * n[None, :]) % GRID_W).astype(np.float64) / GRID_W
    c2, s2 = np.cos(ang2), np.sin(ang2)
    m2 = np.block([[c2, -s2], [s2, c2]])
    d = np.arange(HEAD_DIM)
    ang3 = 2.0 * np.pi * ((d[:, None] * d[None, :]) % HEAD_DIM).astype(np.float64) / HEAD_DIM
    scale = 1.0 / np.sqrt(float(SEQ * HEAD_DIM))
    cd = np.kron(np.eye(N_SUB), np.cos(ang3)) * scale
    sd = np.kron(np.eye(N_SUB), np.sin(ang3)) * scale
    m3 = np.concatenate([cd, -sd], axis=0)
    return g1.astype(np.float32), m2.astype(np.float32), m3.astype(np.float32)


@functools.lru_cache(maxsize=None)
def _head_mean_matrix():
    return np.kron(np.eye(N_SUB), np.full((HEAD_DIM, HEAD_DIM), 1.0 / HEAD_DIM)).astype(np.float32)


def _natten_bias_table(rpb):
    n_lh = rpb.shape[0] * rpb.shape[1]
    rpb = rpb.reshape(n_lh, 2 * WIN_H - 1, 2 * WIN_W - 1).astype(F32)
    c = np.arange(GRID_W)
    col_start = np.clip(c - WIN_W // 2, 0, GRID_W - WIN_W)
    col_ok = (c[None, :] >= col_start[:, None]) & (c[None, :] < col_start[:, None] + WIN_W)
    d_col = c[None, :] - c[:, None] + (WIN_W - 1)
    onehot = (d_col[None, :, :] == np.arange(2 * WIN_W - 1)[:, None, None]).astype(np.float32)
    toe = jnp.einsum("hij,jcw->ihcw", rpb, onehot, precision=lax.Precision.HIGHEST)
    toe = jnp.where(col_ok[None, None, :, :], toe, NEG)
    toe = toe.reshape(2 * WIN_H - 1, n_lh * GRID_W, GRID_W)
    return jnp.concatenate([toe[:-1], toe[1:]], axis=-1)


def _head_rmsnorm(y, gain, ones_bd):
    ms = jnp.dot((y * y).astype(BF16), ones_bd, preferred_element_type=F32)
    return y * lax.rsqrt(ms + EPS) * gain


def _silu(g):
    h = 0.5 * g
    return h + h * jnp.tanh(h)


def _lane_head(rows):
    return lax.broadcasted_iota(jnp.int32, (rows, GROUP_W), 1) // HEAD_DIM


def _stack_heads(q, lane_head):
    qf = q.astype(F32)
    return jnp.concatenate(
        [jnp.where(lane_head == h, qf, 0.0) for h in range(N_SUB)], axis=0).astype(BF16)


def _pick_heads(o, lane_head, rows):
    out = o[0:rows]
    for h in range(1, N_SUB):
        out = jnp.where(lane_head == h, o[h * rows:(h + 1) * rows], out)
    return out


def _softmax_pv(s, v):
    m = jnp.max(s, axis=-1, keepdims=True)
    e = jnp.exp2(s - m)
    l = jnp.sum(e, axis=-1, keepdims=True)
    return jnp.dot(e.astype(BF16), v, preferred_element_type=F32) / l


_NT = (((1,), (1,)), ((), ()))


def _mem_kv_kernel(mem_ref, g_ref, w_ref, kg_ref, ones_ref, k_ref, v_ref):
    m = mem_ref[...].reshape(MEM_KV_BATCH * N_MEM, D_MODEL)
    ms = jnp.mean(m * m, axis=-1, keepdims=True)
    mn = (m * lax.rsqrt(ms + EPS) * g_ref[...]).astype(BF16)
    kv = jnp.dot(mn, w_ref[...], preferred_element_type=F32)
    k = _head_rmsnorm(kv[:, :GROUP_W], kg_ref[...], ones_ref[...])
    k_ref[...] = k.astype(BF16).reshape(MEM_KV_BATCH, N_MEM, GROUP_W)
    v_ref[...] = kv[:, GROUP_W:].astype(BF16).reshape(MEM_KV_BATCH, N_MEM, GROUP_W)


def _in_proj_kernel(x_ref, g_ref, w_ref, hg_ref, ones_ref, tile_ref, kv_ref, cin_ref, wb_ref):
    head_gain_row = {"b_q": 0, "b_k": 1, "d_q": 2}

    @pl.when(pl.program_id(0) == 0)
    def _cast_weight_once():
        wb_ref[...] = w_ref[...].astype(BF16)

    x = x_ref[...]
    inv = lax.rsqrt(jnp.mean(x * x, axis=-1, keepdims=True) + EPS)
    h = (x * g_ref[...]).astype(BF16)
    y_all = jnp.dot(h, wb_ref[...], preferred_element_type=F32)
    for j, name in enumerate(IN_SLOTS):
        y = y_all[:, j * GROUP_W:(j + 1) * GROUP_W] * inv
        if name in head_gain_row:
            r = head_gain_row[name]
            y = _head_rmsnorm(y, hg_ref[r:r + 1, :], ones_ref[...])
        if name == "c_in":
            cin_ref[...] = y
        elif name in KV_SLOTS:
            col = KV_SLOTS.index(name) * GROUP_W
            kv_ref[:, col:col + GROUP_W] = y.astype(BF16)
        else:
            col = TILE_SLOTS.index(name) * GROUP_W
            tile_ref[:, col:col + GROUP_W] = y.astype(BF16)


def _fnet_kernel(x_ref, g1_ref, m2_ref, m3_ref, wf_ref, z_ref, t_ref):
    half = GRID_W * FFT_G

    def stage1(g, carry):
        n0 = pl.multiple_of(g * FFT_G, FFT_G)
        x = x_ref[0, :, pl.ds(n0, FFT_G), :].reshape(half, GROUP_W).astype(BF16)
        y = jnp.dot(g1_ref[g], x, preferred_element_type=F32)
        t_ref[:, :, pl.ds(n0, FFT_G), :] = y.reshape(2, GRID_W, FFT_G, GROUP_W)
        return carry

    m3w = jnp.dot(m3_ref[...], wf_ref[...], preferred_element_type=F32).astype(BF16)

    def stage2(g, carry):
        k0 = g * FFT_G
        t = jnp.concatenate(
            [jnp.concatenate([t_ref[0, k0 + j], t_ref[1, k0 + j]], axis=0) for j in range(FFT_G)],
            axis=-1).astype(BF16)
        ab = jnp.dot(m2_ref[...], t, preferred_element_type=F32)
        ab = jnp.concatenate(
            [jnp.concatenate([ab[:GRID_W, j * GROUP_W:(j + 1) * GROUP_W],
                              ab[GRID_W:, j * GROUP_W:(j + 1) * GROUP_W]], axis=-1) for j in range(FFT_G)],
            axis=0).astype(BF16)
        zc = jnp.dot(ab, m3w, preferred_element_type=F32)
        zc = zc.reshape(FFT_G, GRID_W, GROUP_W)
        for hf in range(GROUP_W // LANES):
            z_ref[0, hf, pl.ds(k0, FFT_G), :, :] = zc[:, :, hf * LANES:(hf + 1) * LANES]
        return carry

    n_groups = GRID_W // FFT_G
    lax.fori_loop(0, n_groups, stage1, 0, unroll=True)
    lax.fori_loop(0, n_groups, stage2, 0, unroll=True)


def _mixer_kernel(x_ref, tile_ref, kv_ref, zc_ref, km_ref, vm_ref, wout_ref, lng_ref, lnb_ref, ws_ref, bs_ref,
                  bias_ref, o_ref, y_ref, yb_ref, woutb_ref):
    t = pl.program_id(1)

    @pl.when((pl.program_id(0) == 0) & (t == 0))
    def _cast_weight_once():
        woutb_ref[...] = wout_ref[...].astype(BF16)

    def slot(name, rows=slice(None)):
        col = TILE_SLOTS.index(name) * GROUP_W
        return tile_ref[0, rows, col:col + GROUP_W]

    k_col, v_col = (KV_SLOTS.index(name) * GROUP_W for name in ("b_k", "b_v"))

    lh_tile = _lane_head(TM)
    qs = _stack_heads(slot("d_q"), lh_tile)
    s = lax.dot_general(qs, km_ref[0], _NT, preferred_element_type=F32)
    yd = _pick_heads(_softmax_pv(s, vm_ref[0]), lh_tile, TM)
    y_ref[:, 3 * GROUP_W:4 * GROUP_W] = (yd * _silu(slot("d_g").astype(F32))).astype(BF16)

    lh_row = _lane_head(GRID_W)
    for i in range(ROWS_PER_TILE):
        r = t * ROWS_PER_TILE + i
        rs = jnp.clip(r - WIN_H // 2, 0, GRID_H - WIN_H)
        k0 = pl.multiple_of(rs * GRID_W, GRID_W)
        qs = _stack_heads(slot("b_q", slice(i * GRID_W, (i + 1) * GRID_W)), lh_row)
        kw = kv_ref[0, pl.ds(k0, WIN_H * GRID_W), k_col:k_col + GROUP_W]
        vw = kv_ref[0, pl.ds(k0, WIN_H * GRID_W), v_col:v_col + GROUP_W]
        d0 = (WIN_H - 1) - (r - rs)
        bias = jnp.concatenate([bias_ref[d0 + 2 * m] for m in range(WIN_H // 2)], axis=-1)
        s = lax.dot_general(qs, kw, _NT, preferred_element_type=F32) + bias
        o = _softmax_pv(s, vw)
        yb_ref[i * GRID_W:(i + 1) * GRID_W, :] = _pick_heads(o, lh_row, GRID_W)
    y_ref[:, GROUP_W:2 * GROUP_W] = (yb_ref[...] * _silu(slot("b_g").astype(F32))).astype(BF16)

    lh_chunk = _lane_head(CHUNK)
    for c in range(TM // CHUNK):
        rows = slice(c * CHUNK, (c + 1) * CHUNK)
        v = slot("a_v", rows).astype(F32)
        mu = jnp.mean(v, axis=-1, keepdims=True)
        var = jnp.mean(jnp.square(v - mu), axis=-1, keepdims=True)
        vn = (v - mu) * lax.rsqrt(var + EPS) * lng_ref[...] + lnb_ref[...]
        s = jnp.dot(ws_ref[...], _stack_heads(vn, lh_chunk), preferred_element_type=F32) + bs_ref[...]
        ya = slot("a_u", rows).astype(F32) * s * _silu(slot("a_g", rows).astype(F32))
        y_ref[rows, 0:GROUP_W] = ya.astype(BF16)

    zc_halves = [zc_ref.at[0, hf].reshape(GRID_W * ROWS_PER_TILE, LANES) for hf in range(GROUP_W // LANES)]
    zc = jnp.concatenate(
        [jnp.concatenate([zh[pl.ds(i, GRID_W, stride=ROWS_PER_TILE), :] for zh in zc_halves], axis=-1)
         for i in range(ROWS_PER_TILE)], axis=0)
    y_ref[:, 2 * GROUP_W:3 * GROUP_W] = (zc * _silu(slot("c_g").astype(F32))).astype(BF16)

    o_ref[0] = x_ref[0] + jnp.dot(y_ref[...], woutb_ref[...], preferred_element_type=F32)


def _const_spec(shape):
    nd = len(shape)
    return pl.BlockSpec(shape, lambda *_: (0,) * nd)


def _layer_spec(shape, layer):
    nd = len(shape)
    return pl.BlockSpec((None,) + tuple(shape), lambda *_: (layer,) + (0,) * nd)


def _params(dims):
    return pltpu.CompilerParams(dimension_semantics=("arbitrary",) * dims, vmem_limit_bytes=VMEM_LIMIT)


def _mem_kv(mem, g, w_kv, kg, ones_bd):
    return pl.pallas_call(
        _mem_kv_kernel,
        grid=(DEPTH, BATCH // MEM_KV_BATCH),
        in_specs=[
            pl.BlockSpec((MEM_KV_BATCH, N_MEM, D_MODEL), lambda l, b: (b, 0, 0)),
            pl.BlockSpec((None, 1, D_MODEL), lambda l, b: (l, 0, 0)),
            pl.BlockSpec((None, D_MODEL, 2 * GROUP_W), lambda l, b: (l, 0, 0)),
            pl.BlockSpec((None, 1, GROUP_W), lambda l, b: (l, 0, 0)),
            _const_spec((GROUP_W, GROUP_W)),
        ],
        out_specs=[pl.BlockSpec((None, MEM_KV_BATCH, N_MEM, GROUP_W), lambda l, b: (l, b, 0, 0))] * 2,
        out_shape=[jax.ShapeDtypeStruct((DEPTH, BATCH, N_MEM, GROUP_W), BF16)] * 2,
        compiler_params=_params(2),
        name="mem_kv",
    )(mem, g, w_kv, kg, ones_bd)


def _in_proj(layer, x2d, g, w_in, head_gains, ones_bd):
    n_tok = x2d.shape[0]
    return pl.pallas_call(
        _in_proj_kernel,
        grid=(n_tok // TM_IN,),
        in_specs=[
            pl.BlockSpec((TM_IN, D_MODEL), lambda i: (i, 0)),
            _layer_spec((1, D_MODEL), layer),
            pl.BlockSpec((None, D_MODEL, IN_COLS), lambda i: (layer, 0, 0), pipeline_mode=pl.Buffered(1)),
            _layer_spec((8, GROUP_W), layer),
            _const_spec((GROUP_W, GROUP_W)),
        ],
        out_specs=[
            pl.BlockSpec((TM_IN, TILE_COLS), lambda i: (i, 0)),
            pl.BlockSpec((TM_IN, KV_COLS), lambda i: (i, 0)),
            pl.BlockSpec((TM_IN, GROUP_W), lambda i: (i, 0)),
        ],
        out_shape=[
            jax.ShapeDtypeStruct((n_tok, TILE_COLS), BF16),
            jax.ShapeDtypeStruct((n_tok, KV_COLS), BF16),
            jax.ShapeDtypeStruct((n_tok, GROUP_W), F32),
        ],
        scratch_shapes=[pltpu.VMEM((D_MODEL, IN_COLS), BF16)],
        compiler_params=_params(1),
        name="in_proj",
    )(x2d, g, w_in, head_gains, ones_bd)


def _fnet(layer, c_in, g1, m2, m3, wf_bd):
    x4 = c_in.reshape(BATCH, GRID_W, GRID_W, GROUP_W)
    n_half = GROUP_W // LANES
    return pl.pallas_call(
        _fnet_kernel,
        grid=(BATCH,),
        in_specs=[pl.BlockSpec((1, GRID_W, GRID_W, GROUP_W), lambda b: (b, 0, 0, 0)),
                  _const_spec(g1.shape), _const_spec(m2.shape), _const_spec(m3.shape),
                  _layer_spec((GROUP_W, GROUP_W), layer)],
        out_specs=pl.BlockSpec((1, n_half, GRID_W, GRID_W, LANES), lambda b: (b, 0, 0, 0, 0)),
        out_shape=jax.ShapeDtypeStruct((BATCH, n_half, GRID_W, GRID_W, LANES), F32),
        scratch_shapes=[pltpu.VMEM((2, GRID_W, GRID_W, GROUP_W), F32)],
        compiler_params=_params(1),
        name="fnet",
    )(x4, g1, m2, m3, wf_bd)


def _mixer(layer, x, tiles, kv, zc, km, vm, w_out, ln_g, ln_b, ws_cat, bs_tab, bias_tab):
    mem_spec = pl.BlockSpec((None, 1, N_MEM, GROUP_W), lambda b, t: (layer, b, 0, 0))
    in_specs = [
        pl.BlockSpec((1, TM, D_MODEL), lambda b, t: (b, t, 0)),
        pl.BlockSpec((1, TM, TILE_COLS), lambda b, t: (b, t, 0)),
        pl.BlockSpec((1, SEQ, KV_COLS), lambda b, t: (b, 0, 0)),
        pl.BlockSpec((1, GROUP_W // LANES, GRID_W, ROWS_PER_TILE, LANES), lambda b, t: (b, 0, 0, t, 0)),
        mem_spec, mem_spec,
        _layer_spec((D_MODEL, D_MODEL), layer),
        _layer_spec((1, GROUP_W), layer), _layer_spec((1, GROUP_W), layer),
        _layer_spec((CHUNK, N_SUB * CHUNK), layer),
        _layer_spec((CHUNK, GROUP_W), layer),
        pl.BlockSpec((2 * WIN_H - 2, N_SUB * GRID_W, 2 * GRID_W), lambda b, t: (0, layer, 0)),
    ]
    return pl.pallas_call(
        _mixer_kernel,
        grid=(BATCH, SEQ // TM),
        in_specs=in_specs,
        out_specs=pl.BlockSpec((1, TM, D_MODEL), lambda b, t: (b, t, 0)),
        out_shape=jax.ShapeDtypeStruct((BATCH, SEQ, D_MODEL), F32),
        scratch_shapes=[pltpu.VMEM((TM, D_MODEL), BF16), pltpu.VMEM((TM, GROUP_W), F32),
                        pltpu.VMEM((D_MODEL, D_MODEL), BF16)],
        compiler_params=_params(2),
        name="mixer",
    )(x, tiles, kv, zc, km, vm, w_out, ln_g, ln_b, ws_cat, bs_tab, bias_tab)


def _tile_heads(g):
    return jnp.tile(g.astype(F32), (1, N_SUB))


def kernel(x, mem, norm_g, w_in, w_out, gm_ln_g, gm_ln_b, gm_w_s, gm_b_s, na_qn_g, na_kn_g, na_rpb, fn_w,
           mem_norm_g, mem_w_kv, mem_qn_g, mem_kn_g):
    g1_np, m2_np, m3_np = _dft_tables()
    g1 = jnp.asarray(g1_np).astype(BF16)
    m2 = jnp.asarray(m2_np).astype(BF16)
    m3 = jnp.asarray(m3_np).astype(BF16)
    ones_bd = jnp.asarray(_head_mean_matrix()).astype(BF16)
    qk_scale = HEAD_DIM ** -0.5 * LOG2E

    norm_g3 = norm_g.reshape(DEPTH, 1, D_MODEL)
    head_gains = jnp.stack([_tile_heads(na_qn_g) * qk_scale, _tile_heads(na_kn_g),
                            _tile_heads(mem_qn_g) * qk_scale], axis=1)
    head_gains = jnp.pad(head_gains, ((0, 0), (0, 5), (0, 0)))
    eye = jnp.eye(N_SUB, dtype=fn_w.dtype)
    wf_bd = (fn_w[:, :, :, None, :] * eye[None, :, None, :, None]).reshape(DEPTH, GROUP_W, GROUP_W).astype(BF16)
    ws_cat = gm_w_s.transpose(0, 2, 1, 3).reshape(DEPTH, CHUNK, N_SUB * CHUNK).astype(BF16)
    bs_tab = jnp.repeat(gm_b_s.transpose(0, 2, 1), HEAD_DIM, axis=2)
    ln_g3 = gm_ln_g.reshape(DEPTH, 1, GROUP_W)
    ln_b3 = gm_ln_b.reshape(DEPTH, 1, GROUP_W)
    bias_tab = _natten_bias_table(na_rpb * LOG2E)

    km, vm = _mem_kv(mem, mem_norm_g.reshape(DEPTH, 1, D_MODEL), mem_w_kv.astype(BF16),
                     _tile_heads(mem_kn_g).reshape(DEPTH, 1, GROUP_W), ones_bd)
    for l in range(DEPTH):
        tiles, kv, c_in = _in_proj(l, x.reshape(BATCH * SEQ, D_MODEL), norm_g3, w_in, head_gains, ones_bd)
        zc = _fnet(l, c_in, g1, m2, m3, wf_bd)
        x = _mixer(l, x, tiles.reshape(BATCH, SEQ, TILE_COLS), kv.reshape(BATCH, SEQ, KV_COLS), zc, km, vm,
                   w_out, ln_g3, ln_b3, ws_cat, bs_tab, bias_tab)
    return x
```

```python
import functools

import numpy as np
import jax
import jax.numpy as jnp
from jax import lax
from jax.experimental import pallas as pl
from jax.experimental.pallas import tpu as pltpu

D_MODEL = 1024
BATCH = 8
SEQ = 4096
DEPTH = 2
N_MEM = 256
GROUP_W = 256
N_SUB = 4
HEAD_DIM = 64
CHUNK = 128
GRID_W = 64
GRID_H = SEQ // GRID_W
WIN_H = 8
WIN_W = 16
N_IN_SLOTS = 11
IN_COLS = N_IN_SLOTS * GROUP_W
EPS = 1e-6
NEG = -1e30
LOG2E = 1.4426950408889634

IN_SLOTS = ("a_u", "a_v", "a_g", "b_q", "b_k", "b_v", "b_g", "c_in", "c_g", "d_q", "d_g")
TILE_SLOTS = ("a_u", "a_v", "a_g", "b_q", "b_g", "c_g", "d_q", "d_g")
KV_SLOTS = ("b_k", "b_v")
TILE_COLS = len(TILE_SLOTS) * GROUP_W
KV_COLS = len(KV_SLOTS) * GROUP_W

MEM_KV_BATCH = 8
TM = 512
ROWS_PER_TILE = TM // GRID_W
TM_IN = 1024
FFT_G = 8

LANES = 128
F32 = jnp.float32
BF16 = jnp.bfloat16
VMEM_LIMIT = 56 * 1024 * 1024


@functools.lru_cache(maxsize=None)
def _dft_tables():
    g = np.arange(GRID_W // FFT_G)
    j = np.arange(FFT_G)
    k = np.arange(GRID_W)
    n = np.arange(GRID_W)
    s = GRID_W * n[None, None, None, :] + FFT_G * g[:, None, None, None] + j[None, :, None, None]
    phase = (k[None, None, :, None] * s) % SEQ
    ang = 2.0 * np.pi * phase.astype(np.float64) / SEQ
    g1 = np.zeros((len(g), 2, GRID_W, FFT_G, GRID_W, FFT_G), np.float64)
    for jj in range(FFT_G):
        g1[:, 0, :, jj, :, jj] = np.cos(ang[:, jj])
        g1[:, 1, :, jj, :, jj] = np.sin(ang[:, jj])
    g1 = g1.reshape(len(g), 2 * GRID_W * FFT_G, GRID_W * FFT_G)
    ang2 = 2.0 * np.pi * ((k[:, None] * n[None, :]) % GRID_W).astype(np.float64) / GRID_W
    c2, s2 = np.cos(ang2), np.sin(ang2)
    m2 = np.block([[c2, -s2], [s2, c2]])
    d = np.arange(HEAD_DIM)
    ang3 = 2.0 * np.pi * ((d[:, None] * d[None, :]) % HEAD_DIM).astype(np.float64) / HEAD_DIM
    scale = 1.0 / np.sqrt(float(SEQ * HEAD_DIM))
    cd = np.kron(np.eye(N_SUB), np.cos(ang3)) * scale
    sd = np.kron(np.eye(N_SUB), np.sin(ang3)) * scale
    m3 = np.concatenate([cd, -sd], axis=0)
    return g1.astype(np.float32), m2.astype(np.float32), m3.astype(np.float32)


@functools.lru_cache(maxsize=None)
def _head_mean_matrix():
    return np.kron(np.eye(N_SUB), np.full((HEAD_DIM, HEAD_DIM), 1.0 / HEAD_DIM)).astype(np.float32)


def _natten_bias_table(rpb):
    n_lh = rpb.shape[0] * rpb.shape[1]
    rpb = rpb.reshape(n_lh, 2 * WIN_H - 1, 2 * WIN_W - 1).astype(F32)
    c = np.arange(GRID_W)
    col_start = np.clip(c - WIN_W // 2, 0, GRID_W - WIN_W)
    col_ok = (c[None, :] >= col_start[:, None]) & (c[None, :] < col_start[:, None] + WIN_W)
    d_col = c[None, :] - c[:, None] + (WIN_W - 1)
    onehot = (d_col[None, :, :] == np.arange(2 * WIN_W - 1)[:, None, None]).astype(np.float32)
    toe = jnp.einsum("hij,jcw->ihcw", rpb, onehot, precision=lax.Precision.HIGHEST)
    toe = jnp.where(col_ok[None, None, :, :], toe, NEG)
    toe = toe.reshape(2 * WIN_H - 1, n_lh * GRID_W, GRID_W)
    return jnp.concatenate([toe[:-1], toe[1:]], axis=-1)


def _head_rmsnorm(y, gain, ones_bd):
    ms = jnp.dot((y * y).astype(BF16), ones_bd, preferred_element_type=F32)
    return y * lax.rsqrt(ms + EPS) * gain


def _silu(g):
    h = 0.5 * g
    return h + h * jnp.tanh(h)


def _lane_head(rows):
    return lax.broadcasted_iota(jnp.int32, (rows, GROUP_W), 1) // HEAD_DIM


def _stack_heads(q, lane_head):
    qf = q.astype(F32)
    return jnp.concatenate(
        [jnp.where(lane_head == h, qf, 0.0) for h in range(N_SUB)], axis=0).astype(BF16)


def _pick_heads(o, lane_head, rows):
    out = o[0:rows]
    for h in range(1, N_SUB):
        out = jnp.where(lane_head == h, o[h * rows:(h + 1) * rows], out)
    return out


def _softmax_pv(s, v, lane_head, rows):
    m = jnp.max(s, axis=-1, keepdims=True)
    e = jnp.exp2(s - m)
    inv = 1.0 / jnp.sum(e, axis=-1, keepdims=True)
    o = jnp.dot(e.astype(BF16), v, preferred_element_type=F32)
    return _pick_heads(o, lane_head, rows) * _pick_heads(inv, lane_head, rows)


_NT = (((1,), (1,)), ((), ()))


def _mem_kv_kernel(mem_ref, g_ref, w_ref, kg_ref, ones_ref, k_ref, v_ref):
    m = mem_ref[...].reshape(MEM_KV_BATCH * N_MEM, D_MODEL)
    ms = jnp.mean(m * m, axis=-1, keepdims=True)
    mn = (m * lax.rsqrt(ms + EPS) * g_ref[...]).astype(BF16)
    kv = jnp.dot(mn, w_ref[...], preferred_element_type=F32)
    k = _head_rmsnorm(kv[:, :GROUP_W], kg_ref[...], ones_ref[...])
    k_ref[...] = k.astype(BF16).reshape(MEM_KV_BATCH, N_MEM, GROUP_W)
    v_ref[...] = kv[:, GROUP_W:].astype(BF16).reshape(MEM_KV_BATCH, N_MEM, GROUP_W)


def _in_proj_kernel(x_ref, g_ref, w_ref, hg_ref, ones_ref, tile_ref, kv_ref, cin_ref):
    head_gain_row = {"b_q": 0, "b_k": 1, "d_q": 2}
    x = x_ref[...]
    inv = lax.rsqrt(jnp.mean(x * x, axis=-1, keepdims=True) + EPS)
    h = (x * g_ref[...]).astype(BF16)
    y_all = jnp.dot(h, w_ref[...], preferred_element_type=F32)
    for j, name in enumerate(IN_SLOTS):
        y = y_all[:, j * GROUP_W:(j + 1) * GROUP_W] * inv
        if name in head_gain_row:
            r = head_gain_row[name]
            y = _head_rmsnorm(y, hg_ref[r:r + 1, :], ones_ref[...])
        if name == "c_in":
            cin_ref[...] = y
        elif name in KV_SLOTS:
            col = KV_SLOTS.index(name) * GROUP_W
            kv_ref[:, col:col + GROUP_W] = y.astype(BF16)
        else:
            col = TILE_SLOTS.index(name) * GROUP_W
            tile_ref[:, col:col + GROUP_W] = y.astype(BF16)


def _fnet_kernel(x_ref, g1_ref, m2_ref, m3_ref, wf_ref, z_ref, t_ref):
    half = GRID_W * FFT_G

    def stage1(g, carry):
        n0 = pl.multiple_of(g * FFT_G, FFT_G)
        x = x_ref[0, :, pl.ds(n0, FFT_G), :].reshape(half, GROUP_W).astype(BF16)
        y = jnp.dot(g1_ref[g], x, preferred_element_type=F32)
        t_ref[:, :, pl.ds(n0, FFT_G), :] = y.reshape(2, GRID_W, FFT_G, GROUP_W)
        return carry

    m3w = jnp.dot(m3_ref[...], wf_ref[...], preferred_element_type=F32).astype(BF16)

    def stage2(g, carry):
        k0 = g * FFT_G
        t = jnp.concatenate(
            [jnp.concatenate([t_ref[0, k0 + j], t_ref[1, k0 + j]], axis=0) for j in range(FFT_G)],
            axis=-1).astype(BF16)
        ab = jnp.dot(m2_ref[...], t, preferred_element_type=F32)
        ab = jnp.concatenate(
            [jnp.concatenate([ab[:GRID_W, j * GROUP_W:(j + 1) * GROUP_W],
                              ab[GRID_W:, j * GROUP_W:(j + 1) * GROUP_W]], axis=-1) for j in range(FFT_G)],
            axis=0).astype(BF16)
        zc = jnp.dot(ab, m3w, preferred_element_type=F32)
        zc = zc.reshape(FFT_G, GRID_W, GROUP_W)
        for hf in range(GROUP_W // LANES):
            z_ref[0, hf, pl.ds(k0, FFT_G), :, :] = zc[:, :, hf * LANES:(hf + 1) * LANES]
        return carry

    n_groups = GRID_W // FFT_G
    lax.fori_loop(0, n_groups, stage1, 0, unroll=True)
    lax.fori_loop(0, n_groups, stage2, 0, unroll=True)


def _mixer_kernel(x_ref, tile_ref, kv_ref, zc_ref, km_ref, vm_ref, wout_ref, lng_ref, lnb_ref, ws_ref, bs_ref,
                  bias_ref, o_ref, y_ref, yb_ref):
    t = pl.program_id(1)

    def slot(name, rows=slice(None)):
        col = TILE_SLOTS.index(name) * GROUP_W
        return tile_ref[0, rows, col:col + GROUP_W]

    k_col, v_col = (KV_SLOTS.index(name) * GROUP_W for name in ("b_k", "b_v"))

    lh_tile = _lane_head(TM)
    qs = _stack_heads(slot("d_q"), lh_tile)
    s = lax.dot_general(qs, km_ref[0], _NT, preferred_element_type=F32)
    yd = _softmax_pv(s, vm_ref[0], lh_tile, TM)
    y_ref[:, 3 * GROUP_W:4 * GROUP_W] = (yd * _silu(slot("d_g").astype(F32))).astype(BF16)

    lh_row = _lane_head(GRID_W)
    for i in range(ROWS_PER_TILE):
        r = t * ROWS_PER_TILE + i
        rs = jnp.clip(r - WIN_H // 2, 0, GRID_H - WIN_H)
        k0 = pl.multiple_of(rs * GRID_W, GRID_W)
        qs = _stack_heads(slot("b_q", slice(i * GRID_W, (i + 1) * GRID_W)), lh_row)
        kw = kv_ref[0, pl.ds(k0, WIN_H * GRID_W), k_col:k_col + GROUP_W]
        vw = kv_ref[0, pl.ds(k0, WIN_H * GRID_W), v_col:v_col + GROUP_W]
        d0 = (WIN_H - 1) - (r - rs)
        bias = jnp.concatenate([bias_ref[d0 + 2 * m] for m in range(WIN_H // 2)], axis=-1)
        s = lax.dot_general(qs, kw, _NT, preferred_element_type=F32) + bias
        yb_ref[i * GRID_W:(i + 1) * GRID_W, :] = _softmax_pv(s, vw, lh_row, GRID_W)
    y_ref[:, GROUP_W:2 * GROUP_W] = (yb_ref[...] * _silu(slot("b_g").astype(F32))).astype(BF16)

    lh_chunk = _lane_head(CHUNK)
    for c in range(TM // CHUNK):
        rows = slice(c * CHUNK, (c + 1) * CHUNK)
        v = slot("a_v", rows).astype(F32)
        mu = jnp.mean(v, axis=-1, keepdims=True)
        var = jnp.mean(jnp.square(v - mu), axis=-1, keepdims=True)
        vn = (v - mu) * lax.rsqrt(var + EPS) * lng_ref[...] + lnb_ref[...]
        s = jnp.dot(ws_ref[...], _stack_heads(vn, lh_chunk), preferred_element_type=F32) + bs_ref[...]
        ya = slot("a_u", rows).astype(F32) * s * _silu(slot("a_g", rows).astype(F32))
        y_ref[rows, 0:GROUP_W] = ya.astype(BF16)

    zc_halves = [zc_ref.at[0, hf].reshape(GRID_W * ROWS_PER_TILE, LANES) for hf in range(GROUP_W // LANES)]
    zc = jnp.concatenate(
        [jnp.concatenate([zh[pl.ds(i, GRID_W, stride=ROWS_PER_TILE), :] for zh in zc_halves], axis=-1)
         for i in range(ROWS_PER_TILE)], axis=0)
    y_ref[:, 2 * GROUP_W:3 * GROUP_W] = (zc * _silu(slot("c_g").astype(F32))).astype(BF16)

    o_ref[0] = x_ref[0] + jnp.dot(y_ref[...], wout_ref[...], preferred_element_type=F32)


def _const_spec(shape):
    nd = len(shape)
    return pl.BlockSpec(shape, lambda *_: (0,) * nd)


def _layer_spec(shape, layer):
    nd = len(shape)
    return pl.BlockSpec((None,) + tuple(shape), lambda *_: (layer,) + (0,) * nd)


def _params(dims):
    return pltpu.CompilerParams(dimension_semantics=("arbitrary",) * dims, vmem_limit_bytes=VMEM_LIMIT)


def _mem_kv(mem, g, w_kv, kg, ones_bd):
    return pl.pallas_call(
        _mem_kv_kernel,
        grid=(DEPTH, BATCH // MEM_KV_BATCH),
        in_specs=[
            pl.BlockSpec((MEM_KV_BATCH, N_MEM, D_MODEL), lambda l, b: (b, 0, 0)),
            pl.BlockSpec((None, 1, D_MODEL), lambda l, b: (l, 0, 0)),
            pl.BlockSpec((None, D_MODEL, 2 * GROUP_W), lambda l, b: (l, 0, 0)),
            pl.BlockSpec((None, 1, GROUP_W), lambda l, b: (l, 0, 0)),
            _const_spec((GROUP_W, GROUP_W)),
        ],
        out_specs=[pl.BlockSpec((None, MEM_KV_BATCH, N_MEM, GROUP_W), lambda l, b: (l, b, 0, 0))] * 2,
        out_shape=[jax.ShapeDtypeStruct((DEPTH, BATCH, N_MEM, GROUP_W), BF16)] * 2,
        compiler_params=_params(2),
        name="mem_kv",
    )(mem, g, w_kv, kg, ones_bd)


def _in_proj(layer, x2d, g, w_in, head_gains, ones_bd):
    n_tok = x2d.shape[0]
    return pl.pallas_call(
        _in_proj_kernel,
        grid=(n_tok // TM_IN,),
        in_specs=[
            pl.BlockSpec((TM_IN, D_MODEL), lambda i: (i, 0)),
            _layer_spec((1, D_MODEL), layer),
            _layer_spec((D_MODEL, IN_COLS), layer),
            _layer_spec((8, GROUP_W), layer),
            _const_spec((GROUP_W, GROUP_W)),
        ],
        out_specs=[
            pl.BlockSpec((TM_IN, TILE_COLS), lambda i: (i, 0)),
            pl.BlockSpec((TM_IN, KV_COLS), lambda i: (i, 0)),
            pl.BlockSpec((TM_IN, GROUP_W), lambda i: (i, 0)),
        ],
        out_shape=[
            jax.ShapeDtypeStruct((n_tok, TILE_COLS), BF16),
            jax.ShapeDtypeStruct((n_tok, KV_COLS), BF16),
            jax.ShapeDtypeStruct((n_tok, GROUP_W), F32),
        ],
        compiler_params=_params(1),
        name="in_proj",
    )(x2d, g, w_in, head_gains, ones_bd)


def _fnet(layer, c_in, g1, m2, m3, wf_bd):
    x4 = c_in.reshape(BATCH, GRID_W, GRID_W, GROUP_W)
    n_half = GROUP_W // LANES
    return pl.pallas_call(
        _fnet_kernel,
        grid=(BATCH,),
        in_specs=[pl.BlockSpec((1, GRID_W, GRID_W, GROUP_W), lambda b: (b, 0, 0, 0)),
                  _const_spec(g1.shape), _const_spec(m2.shape), _const_spec(m3.shape),
                  _layer_spec((GROUP_W, GROUP_W), layer)],
        out_specs=pl.BlockSpec((1, n_half, GRID_W, GRID_W, LANES), lambda b: (b, 0, 0, 0, 0)),
        out_shape=jax.ShapeDtypeStruct((BATCH, n_half, GRID_W, GRID_W, LANES), F32),
        scratch_shapes=[pltpu.VMEM((2, GRID_W, GRID_W, GROUP_W), F32)],
        compiler_params=_params(1),
        name="fnet",
    )(x4, g1, m2, m3, wf_bd)


def _mixer(layer, x, tiles, kv, zc, km, vm, w_out, ln_g, ln_b, ws_cat, bs_tab, bias_tab):
    mem_spec = pl.BlockSpec((None, 1, N_MEM, GROUP_W), lambda b, t: (layer, b, 0, 0))
    in_specs = [
        pl.BlockSpec((1, TM, D_MODEL), lambda b, t: (b, t, 0)),
        pl.BlockSpec((1, TM, TILE_COLS), lambda b, t: (b, t, 0)),
        pl.BlockSpec((1, SEQ, KV_COLS), lambda b, t: (b, 0, 0)),
        pl.BlockSpec((1, GROUP_W // LANES, GRID_W, ROWS_PER_TILE, LANES), lambda b, t: (b, 0, 0, t, 0)),
        mem_spec, mem_spec,
        _layer_spec((D_MODEL, D_MODEL), layer),
        _layer_spec((1, GROUP_W), layer), _layer_spec((1, GROUP_W), layer),
        _layer_spec((CHUNK, N_SUB * CHUNK), layer),
        _layer_spec((CHUNK, GROUP_W), layer),
        pl.BlockSpec((2 * WIN_H - 2, N_SUB * GRID_W, 2 * GRID_W), lambda b, t: (0, layer, 0)),
    ]
    return pl.pallas_call(
        _mixer_kernel,
        grid=(BATCH, SEQ // TM),
        in_specs=in_specs,
        out_specs=pl.BlockSpec((1, TM, D_MODEL), lambda b, t: (b, t, 0)),
        out_shape=jax.ShapeDtypeStruct((BATCH, SEQ, D_MODEL), F32),
        scratch_shapes=[pltpu.VMEM((TM, D_MODEL), BF16), pltpu.VMEM((TM, GROUP_W), F32)],
        compiler_params=_params(2),
        name="mixer",
    )(x, tiles, kv, zc, km, vm, w_out, ln_g, ln_b, ws_cat, bs_tab, bias_tab)


def _tile_heads(g):
    return jnp.tile(g.astype(F32), (1, N_SUB))


def kernel(x, mem, norm_g, w_in, w_out, gm_ln_g, gm_ln_b, gm_w_s, gm_b_s, na_qn_g, na_kn_g, na_rpb, fn_w,
           mem_norm_g, mem_w_kv, mem_qn_g, mem_kn_g):
    g1_np, m2_np, m3_np = _dft_tables()
    g1 = jnp.asarray(g1_np).astype(BF16)
    m2 = jnp.asarray(m2_np).astype(BF16)
    m3 = jnp.asarray(m3_np).astype(BF16)
    ones_bd = jnp.asarray(_head_mean_matrix()).astype(BF16)
    qk_scale = HEAD_DIM ** -0.5 * LOG2E

    w_in_b = w_in.astype(BF16)
    w_out_b = w_out.astype(BF16)
    norm_g3 = norm_g.reshape(DEPTH, 1, D_MODEL)
    head_gains = jnp.stack([_tile_heads(na_qn_g) * qk_scale, _tile_heads(na_kn_g),
                            _tile_heads(mem_qn_g) * qk_scale], axis=1)
    head_gains = jnp.pad(head_gains, ((0, 0), (0, 5), (0, 0)))
    eye = jnp.eye(N_SUB, dtype=fn_w.dtype)
    wf_bd = (fn_w[:, :, :, None, :] * eye[None, :, None, :, None]).reshape(DEPTH, GROUP_W, GROUP_W).astype(BF16)
    ws_cat = gm_w_s.transpose(0, 2, 1, 3).reshape(DEPTH, CHUNK, N_SUB * CHUNK).astype(BF16)
    bs_tab = jnp.repeat(gm_b_s.transpose(0, 2, 1), HEAD_DIM, axis=2)
    ln_g3 = gm_ln_g.reshape(DEPTH, 1, GROUP_W)
    ln_b3 = gm_ln_b.reshape(DEPTH, 1, GROUP_W)
    bias_tab = _natten_bias_table(na_rpb * LOG2E)

    km, vm = _mem_kv(mem, mem_norm_g.reshape(DEPTH, 1, D_MODEL), mem_w_kv.astype(BF16),
                     _tile_heads(mem_kn_g).reshape(DEPTH, 1, GROUP_W), ones_bd)
    for l in range(DEPTH):
        tiles, kv, c_in = _in_proj(l, x.reshape(BATCH * SEQ, D_MODEL), norm_g3, w_in_b, head_gains, ones_bd)
        zc = _fnet(l, c_in, g1, m2, m3, wf_bd)
        x = _mixer(l, x, tiles.reshape(BATCH, SEQ, TILE_COLS), kv.reshape(BATCH, SEQ, KV_COLS), zc, km, vm,
                   w_out_b, ln_g3, ln_b3, ws_cat, bs_tab, bias_tab)
    return x
```

```python
import functools

import numpy as np
import jax
import jax.numpy as jnp
from jax import lax
from jax.experimental import pallas as pl
from jax.experimental.pallas import tpu as pltpu

D_MODEL = 1024
BATCH = 8
SEQ = 4096
DEPTH = 2
N_MEM = 256
GROUP_W = 256
N_SUB = 4
HEAD_DIM = 64
CHUNK = 128
GRID_W = 64
GRID_H = SEQ // GRID_W
WIN_H = 8
WIN_W = 16
N_IN_SLOTS = 11
IN_COLS = N_IN_SLOTS * GROUP_W
EPS = 1e-6
NEG = -1e30
LOG2E = 1.4426950408889634

IN_SLOTS = ("a_u", "a_v", "a_g", "b_q", "b_k", "b_v", "b_g", "c_in", "c_g", "d_q", "d_g")
TILE_SLOTS = ("a_u", "a_v", "a_g", "b_q", "b_g", "c_g", "d_q", "d_g")
KV_SLOTS = ("b_k", "b_v")
TILE_COLS = len(TILE_SLOTS) * GROUP_W
KV_COLS = len(KV_SLOTS) * GROUP_W

KEY_HALF = GRID_W // 2
NATTEN_BLOCKS = ((0, 24, (0,)), (24, 16, (0, 1)), (40, 24, (1,)))
MEM_KV_BATCH = 8
TM = 512
ROWS_PER_TILE = TM // GRID_W
TM_IN = 1024
FFT_G = 8

LANES = 128
F32 = jnp.float32
BF16 = jnp.bfloat16
VMEM_LIMIT = 56 * 1024 * 1024


@functools.lru_cache(maxsize=None)
def _dft_tables():
    g = np.arange(GRID_W // FFT_G)
    j = np.arange(FFT_G)
    k = np.arange(GRID_W)
    n = np.arange(GRID_W)
    s = GRID_W * n[None, None, None, :] + FFT_G * g[:, None, None, None] + j[None, :, None, None]
    phase = (k[None, None, :, None] * s) % SEQ
    ang = 2.0 * np.pi * phase.astype(np.float64) / SEQ
    g1 = np.zeros((len(g), 2, GRID_W, FFT_G, GRID_W, FFT_G), np.float64)
    for jj in range(FFT_G):
        g1[:, 0, :, jj, :, jj] = np.cos(ang[:, jj])
        g1[:, 1, :, jj, :, jj] = np.sin(ang[:, jj])
    g1 = g1.reshape(len(g), 2 * GRID_W * FFT_G, GRID_W * FFT_G)
    ang2 = 2.0 * np.pi * ((k[:, None] * n[None, :]) % GRID_W).astype(np.float64) / GRID_W
    c2, s2 = np.cos(ang2), np.sin(ang2)
    m2 = np.block([[c2, -s2], [s2, c2]])
    d = np.arange(HEAD_DIM)
    ang3 = 2.0 * np.pi * ((d[:, None] * d[None, :]) % HEAD_DIM).astype(np.float64) / HEAD_DIM
    scale = 1.0 / np.sqrt(float(SEQ * HEAD_DIM))
    cd = np.kron(np.eye(N_SUB), np.cos(ang3)) * scale
    sd = np.kron(np.eye(N_SUB), np.sin(ang3)) * scale
    m3 = np.concatenate([cd, -sd], axis=0)
    return g1.astype(np.float32), m2.astype(np.float32), m3.astype(np.float32)


@functools.lru_cache(maxsize=None)
def _head_mean_matrix():
    return np.kron(np.eye(N_SUB), np.full((HEAD_DIM, HEAD_DIM), 1.0 / HEAD_DIM)).astype(np.float32)


def _natten_bias_table(rpb):
    n_layers, n_heads = rpb.shape[0], rpb.shape[1]
    n_lh = n_layers * n_heads
    rpb = rpb.reshape(n_lh, 2 * WIN_H - 1, 2 * WIN_W - 1).astype(F32)
    c = np.arange(GRID_W)
    col_start = np.clip(c - WIN_W // 2, 0, GRID_W - WIN_W)
    col_ok = (c[None, :] >= col_start[:, None]) & (c[None, :] < col_start[:, None] + WIN_W)
    d_col = c[None, :] - c[:, None] + (WIN_W - 1)
    onehot = (d_col[None, :, :] == np.arange(2 * WIN_W - 1)[:, None, None]).astype(np.float32)
    toe = jnp.einsum("hij,jcw->ihcw", rpb, onehot, precision=lax.Precision.HIGHEST)
    toe = jnp.where(col_ok[None, None, :, :], toe, NEG)
    toe = toe.reshape(2 * WIN_H - 1, n_layers, n_heads, GRID_W, GRID_W)
    blocks = []
    for q0, nq, halves in NATTEN_BLOCKS:
        for wb in range(2):
            assert wb in halves or not col_ok[q0:q0 + nq, wb * KEY_HALF:(wb + 1) * KEY_HALF].any()
        blocks.append(toe[:, :, :, q0:q0 + nq, :].reshape(2 * WIN_H - 1, n_layers, n_heads * nq, GRID_W))
    toe = jnp.concatenate(blocks, axis=2).reshape(2 * WIN_H - 1, n_layers * N_SUB * GRID_W, GRID_W)
    n_i = 2 * WIN_H - 1 - 3
    return jnp.stack([
        jnp.concatenate([toe[d:d + n_i, :, wb * KEY_HALF:(wb + 1) * KEY_HALF] for d in range(4)], axis=-1)
        for wb in range(2)], axis=0)


def _head_rmsnorm(y, gain, ones_bd):
    ms = jnp.dot((y * y).astype(BF16), ones_bd, preferred_element_type=F32)
    return y * lax.rsqrt(ms + EPS) * gain


def _silu(g):
    h = 0.5 * g
    return h + h * jnp.tanh(h)


def _lane_head(rows):
    return lax.broadcasted_iota(jnp.int32, (rows, GROUP_W), 1) // HEAD_DIM


def _stack_heads(q, lane_head):
    qf = q.astype(F32)
    return jnp.concatenate(
        [jnp.where(lane_head == h, qf, 0.0) for h in range(N_SUB)], axis=0).astype(BF16)


def _pick_heads(o, lane_head, rows):
    out = o[0:rows]
    for h in range(1, N_SUB):
        out = jnp.where(lane_head == h, o[h * rows:(h + 1) * rows], out)
    return out


def _softmax_pv(s, v, lane_head, rows):
    m = jnp.max(s, axis=-1, keepdims=True)
    e = jnp.exp2(s - m)
    inv = 1.0 / jnp.sum(e, axis=-1, keepdims=True)
    o = jnp.dot(e.astype(BF16), v, preferred_element_type=F32)
    return _pick_heads(o, lane_head, rows) * _pick_heads(inv, lane_head, rows)


_NT = (((1,), (1,)), ((), ()))


def _mem_kv_kernel(mem_ref, g_ref, w_ref, kg_ref, ones_ref, k_ref, v_ref):
    m = mem_ref[...].reshape(MEM_KV_BATCH * N_MEM, D_MODEL)
    ms = jnp.mean(m * m, axis=-1, keepdims=True)
    mn = (m * lax.rsqrt(ms + EPS) * g_ref[...]).astype(BF16)
    kv = jnp.dot(mn, w_ref[...], preferred_element_type=F32)
    k = _head_rmsnorm(kv[:, :GROUP_W], kg_ref[...], ones_ref[...])
    k_ref[...] = k.astype(BF16).reshape(MEM_KV_BATCH, N_MEM, GROUP_W)
    v_ref[...] = kv[:, GROUP_W:].astype(BF16).reshape(MEM_KV_BATCH, N_MEM, GROUP_W)


def _in_proj_kernel(x_ref, g_ref, w_ref, hg_ref, ones_ref, tile_ref, kv_ref, cin_ref):
    head_gain_row = {"b_q": 0, "b_k": 1, "d_q": 2}
    x = x_ref[...]
    inv = lax.rsqrt(jnp.mean(x * x, axis=-1, keepdims=True) + EPS)
    h = (x * g_ref[...]).astype(BF16)
    y_all = jnp.dot(h, w_ref[...], preferred_element_type=F32)
    for j, name in enumerate(IN_SLOTS):
        y = y_all[:, j * GROUP_W:(j + 1) * GROUP_W] * inv
        if name in head_gain_row:
            r = head_gain_row[name]
            y = _head_rmsnorm(y, hg_ref[r:r + 1, :], ones_ref[...])
        if name == "c_in":
            cin_ref[...] = y
        elif name in KV_SLOTS:
            col = KV_SLOTS.index(name) * GROUP_W
            kv_ref[:, col:col + GROUP_W] = y.astype(BF16)
        else:
            col = TILE_SLOTS.index(name) * GROUP_W
            tile_ref[:, col:col + GROUP_W] = y.astype(BF16)


def _fnet_kernel(x_ref, g1_ref, m2_ref, m3_ref, wf_ref, z_ref, t_ref):
    half = GRID_W * FFT_G

    def stage1(g, carry):
        n0 = pl.multiple_of(g * FFT_G, FFT_G)
        x = x_ref[0, :, pl.ds(n0, FFT_G), :].reshape(half, GROUP_W).astype(BF16)
        y = jnp.dot(g1_ref[g], x, preferred_element_type=F32)
        t_ref[:, :, pl.ds(n0, FFT_G), :] = y.reshape(2, GRID_W, FFT_G, GROUP_W)
        return carry

    m3w = jnp.dot(m3_ref[...], wf_ref[...], preferred_element_type=F32).astype(BF16)

    def stage2(g, carry):
        k0 = g * FFT_G
        t = jnp.concatenate(
            [jnp.concatenate([t_ref[0, k0 + j], t_ref[1, k0 + j]], axis=0) for j in range(FFT_G)],
            axis=-1).astype(BF16)
        ab = jnp.dot(m2_ref[...], t, preferred_element_type=F32)
        ab = jnp.concatenate(
            [jnp.concatenate([ab[:GRID_W, j * GROUP_W:(j + 1) * GROUP_W],
                              ab[GRID_W:, j * GROUP_W:(j + 1) * GROUP_W]], axis=-1) for j in range(FFT_G)],
            axis=0).astype(BF16)
        zc = jnp.dot(ab, m3w, preferred_element_type=F32)
        zc = zc.reshape(FFT_G, GRID_W, GROUP_W)
        for hf in range(GROUP_W // LANES):
            z_ref[0, hf, pl.ds(k0, FFT_G), :, :] = zc[:, :, hf * LANES:(hf + 1) * LANES]
        return carry

    n_groups = GRID_W // FFT_G
    lax.fori_loop(0, n_groups, stage1, 0, unroll=True)
    lax.fori_loop(0, n_groups, stage2, 0, unroll=True)


def _mixer_kernel(x_ref, tile_ref, kv_ref, zc_ref, km_ref, vm_ref, wout_ref, lng_ref, lnb_ref, ws_ref, bs_ref,
                  bias_ref, o_ref, y_ref, yb_ref):
    t = pl.program_id(1)

    def slot(name, rows=slice(None)):
        col = TILE_SLOTS.index(name) * GROUP_W
        return tile_ref[0, rows, col:col + GROUP_W]

    k_col, v_col = (KV_SLOTS.index(name) * GROUP_W for name in ("b_k", "b_v"))

    lh_tile = _lane_head(TM)
    qs = _stack_heads(slot("d_q"), lh_tile)
    s = lax.dot_general(qs, km_ref[0], _NT, preferred_element_type=F32)
    yd = _softmax_pv(s, vm_ref[0], lh_tile, TM)
    y_ref[:, 3 * GROUP_W:4 * GROUP_W] = (yd * _silu(slot("d_g").astype(F32))).astype(BF16)

    lh_blk = {nq: _lane_head(nq) for _, nq, _ in NATTEN_BLOCKS}
    half_keys = WIN_H * KEY_HALF
    for i in range(ROWS_PER_TILE):
        r = t * ROWS_PER_TILE + i
        rs = jnp.clip(r - WIN_H // 2, 0, GRID_H - WIN_H)
        k0 = pl.multiple_of(rs * GRID_W, GRID_W)
        q_row = slot("b_q", slice(i * GRID_W, (i + 1) * GRID_W)).astype(F32)
        qs = jnp.concatenate([jnp.where(lh_blk[nq] == h, q_row[q0:q0 + nq], 0.0)
                              for q0, nq, _ in NATTEN_BLOCKS for h in range(N_SUB)], axis=0).astype(BF16)
        key_rows = [pl.ds(k0 + (k * GRID_W + wb * KEY_HALF), KEY_HALF) for wb in range(2) for k in range(WIN_H)]
        kw = jnp.concatenate([kv_ref[0, rows, k_col:k_col + GROUP_W] for rows in key_rows], axis=0)
        vw = jnp.concatenate([kv_ref[0, rows, v_col:v_col + GROUP_W] for rows in key_rows], axis=0)
        s = lax.dot_general(qs, kw, _NT, preferred_element_type=F32)
        d0 = (WIN_H - 1) - (r - rs)
        probs, invs, row0 = [], [], 0
        for q0, nq, halves in NATTEN_BLOCKS:
            nr = N_SUB * nq
            c0, c1 = halves[0] * half_keys, (halves[-1] + 1) * half_keys
            bias = jnp.concatenate([bias_ref[wb, d0 + 4 * m, row0:row0 + nr, :]
                                    for wb in halves for m in range(2)], axis=-1)
            sb = s[row0:row0 + nr, c0:c1] + bias
            e = jnp.exp2(sb - jnp.max(sb, axis=-1, keepdims=True))
            invs.append(1.0 / jnp.sum(e, axis=-1, keepdims=True))
            pieces = ([jnp.zeros((nr, c0), BF16)] if c0 else []) + [e.astype(BF16)]
            pieces += [jnp.zeros((nr, 2 * half_keys - c1), BF16)] if c1 < 2 * half_keys else []
            probs.append(jnp.concatenate(pieces, axis=-1))
            row0 += nr
        o = jnp.dot(jnp.concatenate(probs, axis=0), vw, preferred_element_type=F32)
        outs, row0 = [], 0
        for (q0, nq, _), inv in zip(NATTEN_BLOCKS, invs):
            nr = N_SUB * nq
            outs.append(_pick_heads(o[row0:row0 + nr], lh_blk[nq], nq) * _pick_heads(inv, lh_blk[nq], nq))
            row0 += nr
        yb_ref[i * GRID_W:(i + 1) * GRID_W, :] = jnp.concatenate(outs, axis=0)
    y_ref[:, GROUP_W:2 * GROUP_W] = (yb_ref[...] * _silu(slot("b_g").astype(F32))).astype(BF16)

    lh_chunk = _lane_head(CHUNK)
    for c in range(TM // CHUNK):
        rows = slice(c * CHUNK, (c + 1) * CHUNK)
        v = slot("a_v", rows).astype(F32)
        mu = jnp.mean(v, axis=-1, keepdims=True)
        var = jnp.mean(jnp.square(v - mu), axis=-1, keepdims=True)
        vn = (v - mu) * lax.rsqrt(var + EPS) * lng_ref[...] + lnb_ref[...]
        s = jnp.dot(ws_ref[...], _stack_heads(vn, lh_chunk), preferred_element_type=F32) + bs_ref[...]
        ya = slot("a_u", rows).astype(F32) * s * _silu(slot("a_g", rows).astype(F32))
        y_ref[rows, 0:GROUP_W] = ya.astype(BF16)

    zc_halves = [zc_ref.at[0, hf].reshape(GRID_W * ROWS_PER_TILE, LANES) for hf in range(GROUP_W // LANES)]
    zc = jnp.concatenate(
        [jnp.concatenate([zh[pl.ds(i, GRID_W, stride=ROWS_PER_TILE), :] for zh in zc_halves], axis=-1)
         for i in range(ROWS_PER_TILE)], axis=0)
    y_ref[:, 2 * GROUP_W:3 * GROUP_W] = (zc * _silu(slot("c_g").astype(F32))).astype(BF16)

    o_ref[0] = x_ref[0] + jnp.dot(y_ref[...], wout_ref[...], preferred_element_type=F32)


def _const_spec(shape):
    nd = len(shape)
    return pl.BlockSpec(shape, lambda *_: (0,) * nd)


def _layer_spec(shape, layer):
    nd = len(shape)
    return pl.BlockSpec((None,) + tuple(shape), lambda *_: (layer,) + (0,) * nd)


def _params(dims):
    return pltpu.CompilerParams(dimension_semantics=("arbitrary",) * dims, vmem_limit_bytes=VMEM_LIMIT)


def _mem_kv(mem, g, w_kv, kg, ones_bd):
    return pl.pallas_call(
        _mem_kv_kernel,
        grid=(DEPTH, BATCH // MEM_KV_BATCH),
        in_specs=[
            pl.BlockSpec((MEM_KV_BATCH, N_MEM, D_MODEL), lambda l, b: (b, 0, 0)),
            pl.BlockSpec((None, 1, D_MODEL), lambda l, b: (l, 0, 0)),
            pl.BlockSpec((None, D_MODEL, 2 * GROUP_W), lambda l, b: (l, 0, 0)),
            pl.BlockSpec((None, 1, GROUP_W), lambda l, b: (l, 0, 0)),
            _const_spec((GROUP_W, GROUP_W)),
        ],
        out_specs=[pl.BlockSpec((None, MEM_KV_BATCH, N_MEM, GROUP_W), lambda l, b: (l, b, 0, 0))] * 2,
        out_shape=[jax.ShapeDtypeStruct((DEPTH, BATCH, N_MEM, GROUP_W), BF16)] * 2,
        compiler_params=_params(2),
        name="mem_kv",
    )(mem, g, w_kv, kg, ones_bd)


def _in_proj(layer, x2d, g, w_in, head_gains, ones_bd):
    n_tok = x2d.shape[0]
    return pl.pallas_call(
        _in_proj_kernel,
        grid=(n_tok // TM_IN,),
        in_specs=[
            pl.BlockSpec((TM_IN, D_MODEL), lambda i: (i, 0)),
            _layer_spec((1, D_MODEL), layer),
            _layer_spec((D_MODEL, IN_COLS), layer),
            _layer_spec((8, GROUP_W), layer),
            _const_spec((GROUP_W, GROUP_W)),
        ],
        out_specs=[
            pl.BlockSpec((TM_IN, TILE_COLS), lambda i: (i, 0)),
            pl.BlockSpec((TM_IN, KV_COLS), lambda i: (i, 0)),
            pl.BlockSpec((TM_IN, GROUP_W), lambda i: (i, 0)),
        ],
        out_shape=[
            jax.ShapeDtypeStruct((n_tok, TILE_COLS), BF16),
            jax.ShapeDtypeStruct((n_tok, KV_COLS), BF16),
            jax.ShapeDtypeStruct((n_tok, GROUP_W), F32),
        ],
        compiler_params=_params(1),
        name="in_proj",
    )(x2d, g, w_in, head_gains, ones_bd)


def _fnet(layer, c_in, g1, m2, m3, wf_bd):
    x4 = c_in.reshape(BATCH, GRID_W, GRID_W, GROUP_W)
    n_half = GROUP_W // LANES
    return pl.pallas_call(
        _fnet_kernel,
        grid=(BATCH,),
        in_specs=[pl.BlockSpec((1, GRID_W, GRID_W, GROUP_W), lambda b: (b, 0, 0, 0)),
                  _const_spec(g1.shape), _const_spec(m2.shape), _const_spec(m3.shape),
                  _layer_spec((GROUP_W, GROUP_W), layer)],
        out_specs=pl.BlockSpec((1, n_half, GRID_W, GRID_W, LANES), lambda b: (b, 0, 0, 0, 0)),
        out_shape=jax.ShapeDtypeStruct((BATCH, n_half, GRID_W, GRID_W, LANES), F32),
        scratch_shapes=[pltpu.VMEM((2, GRID_W, GRID_W, GROUP_W), F32)],
        compiler_params=_params(1),
        name="fnet",
    )(x4, g1, m2, m3, wf_bd)


def _mixer(layer, x, tiles, kv, zc, km, vm, w_out, ln_g, ln_b, ws_cat, bs_tab, bias_tab):
    mem_spec = pl.BlockSpec((None, 1, N_MEM, GROUP_W), lambda b, t: (layer, b, 0, 0))
    in_specs = [
        pl.BlockSpec((1, TM, D_MODEL), lambda b, t: (b, t, 0)),
        pl.BlockSpec((1, TM, TILE_COLS), lambda b, t: (b, t, 0)),
        pl.BlockSpec((1, SEQ, KV_COLS), lambda b, t: (b, 0, 0)),
        pl.BlockSpec((1, GROUP_W // LANES, GRID_W, ROWS_PER_TILE, LANES), lambda b, t: (b, 0, 0, t, 0)),
        mem_spec, mem_spec,
        _layer_spec((D_MODEL, D_MODEL), layer),
        _layer_spec((1, GROUP_W), layer), _layer_spec((1, GROUP_W), layer),
        _layer_spec((CHUNK, N_SUB * CHUNK), layer),
        _layer_spec((CHUNK, GROUP_W), layer),
        pl.BlockSpec((2, 2 * WIN_H - 4, N_SUB * GRID_W, LANES), lambda b, t: (0, 0, layer, 0)),
    ]
    return pl.pallas_call(
        _mixer_kernel,
        grid=(BATCH, SEQ // TM),
        in_specs=in_specs,
        out_specs=pl.BlockSpec((1, TM, D_MODEL), lambda b, t: (b, t, 0)),
        out_shape=jax.ShapeDtypeStruct((BATCH, SEQ, D_MODEL), F32),
        scratch_shapes=[pltpu.VMEM((TM, D_MODEL), BF16), pltpu.VMEM((TM, GROUP_W), F32)],
        compiler_params=_params(2),
        name="mixer",
    )(x, tiles, kv, zc, km, vm, w_out, ln_g, ln_b, ws_cat, bs_tab, bias_tab)


def _tile_heads(g):
    return jnp.tile(g.astype(F32), (1, N_SUB))


def kernel(x, mem, norm_g, w_in, w_out, gm_ln_g, gm_ln_b, gm_w_s, gm_b_s, na_qn_g, na_kn_g, na_rpb, fn_w,
           mem_norm_g, mem_w_kv, mem_qn_g, mem_kn_g):
    g1_np, m2_np, m3_np = _dft_tables()
    g1 = jnp.asarray(g1_np).astype(BF16)
    m2 = jnp.asarray(m2_np).astype(BF16)
    m3 = jnp.asarray(m3_np).astype(BF16)
    ones_bd = jnp.asarray(_head_mean_matrix()).astype(BF16)
    qk_scale = HEAD_DIM ** -0.5 * LOG2E

    w_in_b = w_in.astype(BF16)
    w_out_b = w_out.astype(BF16)
    norm_g3 = norm_g.reshape(DEPTH, 1, D_MODEL)
    head_gains = jnp.stack([_tile_heads(na_qn_g) * qk_scale, _tile_heads(na_kn_g),
                            _tile_heads(mem_qn_g) * qk_scale], axis=1)
    head_gains = jnp.pad(head_gains, ((0, 0), (0, 5), (0, 0)))
    eye = jnp.eye(N_SUB, dtype=fn_w.dtype)
    wf_bd = (fn_w[:, :, :, None, :] * eye[None, :, None, :, None]).reshape(DEPTH, GROUP_W, GROUP_W).astype(BF16)
    ws_cat = gm_w_s.transpose(0, 2, 1, 3).reshape(DEPTH, CHUNK, N_SUB * CHUNK).astype(BF16)
    bs_tab = jnp.repeat(gm_b_s.transpose(0, 2, 1), HEAD_DIM, axis=2)
    ln_g3 = gm_ln_g.reshape(DEPTH, 1, GROUP_W)
    ln_b3 = gm_ln_b.reshape(DEPTH, 1, GROUP_W)
    bias_tab = _natten_bias_table(na_rpb * LOG2E)

    km, vm = _mem_kv(mem, mem_norm_g.reshape(DEPTH, 1, D_MODEL), mem_w_kv.astype(BF16),
                     _tile_heads(mem_kn_g).reshape(DEPTH, 1, GROUP_W), ones_bd)
    for l in range(DEPTH):
        tiles, kv, c_in = _in_proj(l, x.reshape(BATCH * SEQ, D_MODEL), norm_g3, w_in_b, head_gains, ones_bd)
        zc = _fnet(l, c_in, g1, m2, m3, wf_bd)
        x = _mixer(l, x, tiles.reshape(BATCH, SEQ, TILE_COLS), kv.reshape(BATCH, SEQ, KV_COLS), zc, km, vm,
                   w_out_b, ln_g3, ln_b3, ws_cat, bs_tab, bias_tab)
    return x
```

```python
import functools

import numpy as np
import jax
import jax.numpy as jnp
from jax import lax
from jax.experimental import pallas as pl
from jax.experimental.pallas import tpu as pltpu

D_MODEL = 1024
BATCH = 8
SEQ = 4096
DEPTH = 2
N_MEM = 256
GROUP_W = 256
N_SUB = 4
HEAD_DIM = 64
CHUNK = 128
GRID_W = 64
GRID_H = SEQ // GRID_W
WIN_H = 8
WIN_W = 16
N_IN_SLOTS = 11
IN_COLS = N_IN_SLOTS * GROUP_W
EPS = 1e-6
NEG = -1e30
LOG2E = 1.4426950408889634

IN_SLOTS = ("a_u", "a_v", "a_g", "b_q", "b_k", "b_v", "b_g", "c_in", "c_g", "d_q", "d_g")
TILE_SLOTS = ("a_u", "a_v", "a_g", "b_q", "b_g", "c_g", "d_q", "d_g")
KV_SLOTS = ("b_k", "b_v")
TILE_COLS = len(TILE_SLOTS) * GROUP_W
KV_COLS = len(KV_SLOTS) * GROUP_W

KEY_HALF = GRID_W // 2
NATTEN_BLOCKS = ((0, 24, (0,)), (24, 16, (0, 1)), (40, 24, (1,)))
MEM_KV_BATCH = 8
TM = 512
ROWS_PER_TILE = TM // GRID_W
TM_IN = 1024
FFT_G = 8

LANES = 128
F32 = jnp.float32
BF16 = jnp.bfloat16
VMEM_LIMIT = 56 * 1024 * 1024


@functools.lru_cache(maxsize=None)
def _dft_tables():
    g = np.arange(GRID_W // FFT_G)
    j = np.arange(FFT_G)
    k = np.arange(GRID_W)
    n = np.arange(GRID_W)
    s = GRID_W * n[None, None, None, :] + FFT_G * g[:, None, None, None] + j[None, :, None, None]
    phase = (k[None, None, :, None] * s) % SEQ
    ang = 2.0 * np.pi * phase.astype(np.float64) / SEQ
    g1 = np.zeros((len(g), 2, GRID_W, FFT_G, GRID_W, FFT_G), np.float64)
    for jj in range(FFT_G):
        g1[:, 0, :, jj, :, jj] = np.cos(ang[:, jj])
        g1[:, 1, :, jj, :, jj] = np.sin(ang[:, jj])
    g1 = g1.reshape(len(g), 2 * GRID_W * FFT_G, GRID_W * FFT_G)
    ang2 = 2.0 * np.pi * ((k[:, None] * n[None, :]) % GRID_W).astype(np.float64) / GRID_W
    c2, s2 = np.cos(ang2), np.sin(ang2)
    m2 = np.block([[c2, -s2], [s2, c2]])
    d = np.arange(HEAD_DIM)
    ang3 = 2.0 * np.pi * ((d[:, None] * d[None, :]) % HEAD_DIM).astype(np.float64) / HEAD_DIM
    scale = 1.0 / np.sqrt(float(SEQ * HEAD_DIM))
    cd = np.kron(np.eye(N_SUB), np.cos(ang3)) * scale
    sd = np.kron(np.eye(N_SUB), np.sin(ang3)) * scale
    m3 = np.concatenate([cd, -sd], axis=0)
    return g1.astype(np.float32), m2.astype(np.float32), m3.astype(np.float32)


@functools.lru_cache(maxsize=None)
def _head_mean_matrix():
    return np.kron(np.eye(N_SUB), np.full((HEAD_DIM, HEAD_DIM), 1.0 / HEAD_DIM)).astype(np.float32)


def _natten_bias_table(rpb):
    n_layers, n_heads = rpb.shape[0], rpb.shape[1]
    n_lh = n_layers * n_heads
    rpb = rpb.reshape(n_lh, 2 * WIN_H - 1, 2 * WIN_W - 1).astype(F32)
    c = np.arange(GRID_W)
    col_start = np.clip(c - WIN_W // 2, 0, GRID_W - WIN_W)
    col_ok = (c[None, :] >= col_start[:, None]) & (c[None, :] < col_start[:, None] + WIN_W)
    n_i, n_d = 2 * WIN_H - 1 - 3, 4
    d_col = (c[None, :] - c[:, None] + (WIN_W - 1)).reshape(GRID_W, 2, KEY_HALF)
    col_sel = (d_col[None] == np.arange(2 * WIN_W - 1)[:, None, None, None]).astype(np.float32)
    row_sel = (np.arange(2 * WIN_H - 1)[:, None, None]
               == np.arange(n_i)[None, :, None] + np.arange(n_d)[None, None, :]).astype(np.float32)
    tab = jnp.einsum("hrj,rid,jcbw->bihcdw", rpb, row_sel, col_sel, precision=lax.Precision.HIGHEST)
    ok = col_ok.reshape(GRID_W, 2, KEY_HALF).transpose(1, 0, 2)
    tab = jnp.where(ok[:, None, None, :, None, :], tab, NEG)
    tab = tab.reshape(2, n_i, n_layers, n_heads, GRID_W, n_d * KEY_HALF)
    blocks = []
    for q0, nq, halves in NATTEN_BLOCKS:
        for wb in range(2):
            assert wb in halves or not col_ok[q0:q0 + nq, wb * KEY_HALF:(wb + 1) * KEY_HALF].any()
        blocks.append(tab[:, :, :, :, q0:q0 + nq, :].reshape(2, n_i, n_layers, n_heads * nq, LANES))
    return jnp.concatenate(blocks, axis=3).reshape(2, n_i, n_layers * N_SUB * GRID_W, LANES)


def _head_rmsnorm(y, gain, ones_bd):
    ms = jnp.dot((y * y).astype(BF16), ones_bd, preferred_element_type=F32)
    return y * lax.rsqrt(ms + EPS) * gain


def _silu(g):
    h = 0.5 * g
    return h + h * jnp.tanh(h)


def _lane_head(rows):
    return lax.broadcasted_iota(jnp.int32, (rows, GROUP_W), 1) // HEAD_DIM


def _stack_heads(q, lane_head):
    qf = q.astype(F32)
    return jnp.concatenate(
        [jnp.where(lane_head == h, qf, 0.0) for h in range(N_SUB)], axis=0).astype(BF16)


def _pick_heads(o, lane_head, rows):
    out = o[0:rows]
    for h in range(1, N_SUB):
        out = jnp.where(lane_head == h, o[h * rows:(h + 1) * rows], out)
    return out


def _softmax_pv(s, v, lane_head, rows):
    m = jnp.max(s, axis=-1, keepdims=True)
    e = jnp.exp2(s - m)
    inv = 1.0 / jnp.sum(e, axis=-1, keepdims=True)
    o = jnp.dot(e.astype(BF16), v, preferred_element_type=F32)
    return _pick_heads(o, lane_head, rows) * _pick_heads(inv, lane_head, rows)


_NT = (((1,), (1,)), ((), ()))


def _mem_kv_kernel(mem_ref, g_ref, w_ref, kg_ref, ones_ref, k_ref, v_ref):
    m = mem_ref[...].reshape(MEM_KV_BATCH * N_MEM, D_MODEL)
    ms = jnp.mean(m * m, axis=-1, keepdims=True)
    mn = (m * lax.rsqrt(ms + EPS) * g_ref[...]).astype(BF16)
    kv = jnp.dot(mn, w_ref[...], preferred_element_type=F32)
    k = _head_rmsnorm(kv[:, :GROUP_W], kg_ref[...], ones_ref[...])
    k_ref[...] = k.astype(BF16).reshape(MEM_KV_BATCH, N_MEM, GROUP_W)
    v_ref[...] = kv[:, GROUP_W:].astype(BF16).reshape(MEM_KV_BATCH, N_MEM, GROUP_W)


def _in_proj_kernel(x_ref, g_ref, w_ref, hg_ref, ones_ref, tile_ref, kv_ref, cin_ref):
    head_gain_row = {"b_q": 0, "b_k": 1, "d_q": 2}
    x = x_ref[...]
    inv = lax.rsqrt(jnp.mean(x * x, axis=-1, keepdims=True) + EPS)
    h = (x * g_ref[...]).astype(BF16)
    y_all = jnp.dot(h, w_ref[...], preferred_element_type=F32)
    for j, name in enumerate(IN_SLOTS):
        y = y_all[:, j * GROUP_W:(j + 1) * GROUP_W] * inv
        if name in head_gain_row:
            r = head_gain_row[name]
            y = _head_rmsnorm(y, hg_ref[r:r + 1, :], ones_ref[...])
        if name == "c_in":
            cin_ref[...] = y
        elif name in KV_SLOTS:
            col = KV_SLOTS.index(name) * GROUP_W
            kv_ref[:, col:col + GROUP_W] = y.astype(BF16)
        else:
            col = TILE_SLOTS.index(name) * GROUP_W
            tile_ref[:, col:col + GROUP_W] = y.astype(BF16)


def _fnet_kernel(x_ref, g1_ref, m2_ref, m3_ref, wf_ref, z_ref, t_ref):
    half = GRID_W * FFT_G

    def stage1(g, carry):
        n0 = pl.multiple_of(g * FFT_G, FFT_G)
        x = x_ref[0, :, pl.ds(n0, FFT_G), :].reshape(half, GROUP_W).astype(BF16)
        y = jnp.dot(g1_ref[g], x, preferred_element_type=F32)
        t_ref[:, :, pl.ds(n0, FFT_G), :] = y.reshape(2, GRID_W, FFT_G, GROUP_W)
        return carry

    m3w = jnp.dot(m3_ref[...], wf_ref[...], preferred_element_type=F32).astype(BF16)

    def stage2(g, carry):
        k0 = g * FFT_G
        t = jnp.concatenate(
            [jnp.concatenate([t_ref[0, k0 + j], t_ref[1, k0 + j]], axis=0) for j in range(FFT_G)],
            axis=-1).astype(BF16)
        ab = jnp.dot(m2_ref[...], t, preferred_element_type=F32)
        ab = jnp.concatenate(
            [jnp.concatenate([ab[:GRID_W, j * GROUP_W:(j + 1) * GROUP_W],
                              ab[GRID_W:, j * GROUP_W:(j + 1) * GROUP_W]], axis=-1) for j in range(FFT_G)],
            axis=0).astype(BF16)
        zc = jnp.dot(ab, m3w, preferred_element_type=F32)
        zc = zc.reshape(FFT_G, GRID_W, GROUP_W)
        for hf in range(GROUP_W // LANES):
            z_ref[0, hf, pl.ds(k0, FFT_G), :, :] = zc[:, :, hf * LANES:(hf + 1) * LANES]
        return carry

    n_groups = GRID_W // FFT_G
    lax.fori_loop(0, n_groups, stage1, 0, unroll=True)
    lax.fori_loop(0, n_groups, stage2, 0, unroll=True)


def _mixer_kernel(x_ref, tile_ref, kv_ref, zc_ref, km_ref, vm_ref, wout_ref, lng_ref, lnb_ref, ws_ref, bs_ref,
                  bias_ref, o_ref, y_ref, yb_ref):
    t = pl.program_id(1)

    def slot(name, rows=slice(None)):
        col = TILE_SLOTS.index(name) * GROUP_W
        return tile_ref[0, rows, col:col + GROUP_W]

    k_col, v_col = (KV_SLOTS.index(name) * GROUP_W for name in ("b_k", "b_v"))

    lh_tile = _lane_head(TM)
    qs = _stack_heads(slot("d_q"), lh_tile)
    s = lax.dot_general(qs, km_ref[0], _NT, preferred_element_type=F32)
    yd = _softmax_pv(s, vm_ref[0], lh_tile, TM)
    y_ref[:, 3 * GROUP_W:4 * GROUP_W] = (yd * _silu(slot("d_g").astype(F32))).astype(BF16)

    lh_blk = {nq: _lane_head(nq) for _, nq, _ in NATTEN_BLOCKS}
    half_keys = WIN_H * KEY_HALF
    for i in range(ROWS_PER_TILE):
        r = t * ROWS_PER_TILE + i
        rs = jnp.clip(r - WIN_H // 2, 0, GRID_H - WIN_H)
        k0 = pl.multiple_of(rs * GRID_W, GRID_W)
        q_row = slot("b_q", slice(i * GRID_W, (i + 1) * GRID_W)).astype(F32)
        qs = jnp.concatenate([jnp.where(lh_blk[nq] == h, q_row[q0:q0 + nq], 0.0)
                              for q0, nq, _ in NATTEN_BLOCKS for h in range(N_SUB)], axis=0).astype(BF16)
        key_rows = [pl.ds(k0 + (k * GRID_W + wb * KEY_HALF), KEY_HALF) for wb in range(2) for k in range(WIN_H)]
        kw = jnp.concatenate([kv_ref[0, rows, k_col:k_col + GROUP_W] for rows in key_rows], axis=0)
        vw = jnp.concatenate([kv_ref[0, rows, v_col:v_col + GROUP_W] for rows in key_rows], axis=0)
        s = lax.dot_general(qs, kw, _NT, preferred_element_type=F32)
        d0 = (WIN_H - 1) - (r - rs)
        probs, invs, row0 = [], [], 0
        for q0, nq, halves in NATTEN_BLOCKS:
            nr = N_SUB * nq
            c0, c1 = halves[0] * half_keys, (halves[-1] + 1) * half_keys
            bias = jnp.concatenate([bias_ref[wb, d0 + 4 * m, row0:row0 + nr, :]
                                    for wb in halves for m in range(2)], axis=-1)
            sb = s[row0:row0 + nr, c0:c1] + bias
            e = jnp.exp2(sb - jnp.max(sb, axis=-1, keepdims=True))
            invs.append(1.0 / jnp.sum(e, axis=-1, keepdims=True))
            pieces = ([jnp.zeros((nr, c0), BF16)] if c0 else []) + [e.astype(BF16)]
            pieces += [jnp.zeros((nr, 2 * half_keys - c1), BF16)] if c1 < 2 * half_keys else []
            probs.append(jnp.concatenate(pieces, axis=-1))
            row0 += nr
        o = jnp.dot(jnp.concatenate(probs, axis=0), vw, preferred_element_type=F32)
        outs, row0 = [], 0
        for (q0, nq, _), inv in zip(NATTEN_BLOCKS, invs):
            nr = N_SUB * nq
            outs.append(_pick_heads(o[row0:row0 + nr], lh_blk[nq], nq) * _pick_heads(inv, lh_blk[nq], nq))
            row0 += nr
        yb_ref[i * GRID_W:(i + 1) * GRID_W, :] = jnp.concatenate(outs, axis=0)
    y_ref[:, GROUP_W:2 * GROUP_W] = (yb_ref[...] * _silu(slot("b_g").astype(F32))).astype(BF16)

    lh_chunk = _lane_head(CHUNK)
    for c in range(TM // CHUNK):
        rows = slice(c * CHUNK, (c + 1) * CHUNK)
        v = slot("a_v", rows).astype(F32)
        mu = jnp.mean(v, axis=-1, keepdims=True)
        var = jnp.mean(jnp.square(v - mu), axis=-1, keepdims=True)
        vn = (v - mu) * lax.rsqrt(var + EPS) * lng_ref[...] + lnb_ref[...]
        s = jnp.dot(ws_ref[...], _stack_heads(vn, lh_chunk), preferred_element_type=F32) + bs_ref[...]
        ya = slot("a_u", rows).astype(F32) * s * _silu(slot("a_g", rows).astype(F32))
        y_ref[rows, 0:GROUP_W] = ya.astype(BF16)

    zc_halves = [zc_ref.at[0, hf].reshape(GRID_W * ROWS_PER_TILE, LANES) for hf in range(GROUP_W // LANES)]
    zc = jnp.concatenate(
        [jnp.concatenate([zh[pl.ds(i, GRID_W, stride=ROWS_PER_TILE), :] for zh in zc_halves], axis=-1)
         for i in range(ROWS_PER_TILE)], axis=0)
    y_ref[:, 2 * GROUP_W:3 * GROUP_W] = (zc * _silu(slot("c_g").astype(F32))).astype(BF16)

    o_ref[0] = x_ref[0] + jnp.dot(y_ref[...], wout_ref[...], preferred_element_type=F32)


def _const_spec(shape):
    nd = len(shape)
    return pl.BlockSpec(shape, lambda *_: (0,) * nd)


def _layer_spec(shape, layer):
    nd = len(shape)
    return pl.BlockSpec((None,) + tuple(shape), lambda *_: (layer,) + (0,) * nd)


def _params(dims):
    return pltpu.CompilerParams(dimension_semantics=("arbitrary",) * dims, vmem_limit_bytes=VMEM_LIMIT)


def _mem_kv(mem, g, w_kv, kg, ones_bd):
    return pl.pallas_call(
        _mem_kv_kernel,
        grid=(DEPTH, BATCH // MEM_KV_BATCH),
        in_specs=[
            pl.BlockSpec((MEM_KV_BATCH, N_MEM, D_MODEL), lambda l, b: (b, 0, 0)),
            pl.BlockSpec((None, 1, D_MODEL), lambda l, b: (l, 0, 0)),
            pl.BlockSpec((None, D_MODEL, 2 * GROUP_W), lambda l, b: (l, 0, 0)),
            pl.BlockSpec((None, 1, GROUP_W), lambda l, b: (l, 0, 0)),
            _const_spec((GROUP_W, GROUP_W)),
        ],
        out_specs=[pl.BlockSpec((None, MEM_KV_BATCH, N_MEM, GROUP_W), lambda l, b: (l, b, 0, 0))] * 2,
        out_shape=[jax.ShapeDtypeStruct((DEPTH, BATCH, N_MEM, GROUP_W), BF16)] * 2,
        compiler_params=_params(2),
        name="mem_kv",
    )(mem, g, w_kv, kg, ones_bd)


def _in_proj(layer, x2d, g, w_in, head_gains, ones_bd):
    n_tok = x2d.shape[0]
    return pl.pallas_call(
        _in_proj_kernel,
        grid=(n_tok // TM_IN,),
        in_specs=[
            pl.BlockSpec((TM_IN, D_MODEL), lambda i: (i, 0)),
            _layer_spec((1, D_MODEL), layer),
            _layer_spec((D_MODEL, IN_COLS), layer),
            _layer_spec((8, GROUP_W), layer),
            _const_spec((GROUP_W, GROUP_W)),
        ],
        out_specs=[
            pl.BlockSpec((TM_IN, TILE_COLS), lambda i: (i, 0)),
            pl.BlockSpec((TM_IN, KV_COLS), lambda i: (i, 0)),
            pl.BlockSpec((TM_IN, GROUP_W), lambda i: (i, 0)),
        ],
        out_shape=[
            jax.ShapeDtypeStruct((n_tok, TILE_COLS), BF16),
            jax.ShapeDtypeStruct((n_tok, KV_COLS), BF16),
            jax.ShapeDtypeStruct((n_tok, GROUP_W), F32),
        ],
        compiler_params=_params(1),
        name="in_proj",
    )(x2d, g, w_in, head_gains, ones_bd)


def _fnet(layer, c_in, g1, m2, m3, wf_bd):
    x4 = c_in.reshape(BATCH, GRID_W, GRID_W, GROUP_W)
    n_half = GROUP_W // LANES
    return pl.pallas_call(
        _fnet_kernel,
        grid=(BATCH,),
        in_specs=[pl.BlockSpec((1, GRID_W, GRID_W, GROUP_W), lambda b: (b, 0, 0, 0)),
                  _const_spec(g1.shape), _const_spec(m2.shape), _const_spec(m3.shape),
                  _layer_spec((GROUP_W, GROUP_W), layer)],
        out_specs=pl.BlockSpec((1, n_half, GRID_W, GRID_W, LANES), lambda b: (b, 0, 0, 0, 0)),
        out_shape=jax.ShapeDtypeStruct((BATCH, n_half, GRID_W, GRID_W, LANES), F32),
        scratch_shapes=[pltpu.VMEM((2, GRID_W, GRID_W, GROUP_W), F32)],
        compiler_params=_params(1),
        name="fnet",
    )(x4, g1, m2, m3, wf_bd)


def _mixer(layer, x, tiles, kv, zc, km, vm, w_out, ln_g, ln_b, ws_cat, bs_tab, bias_tab):
    mem_spec = pl.BlockSpec((None, 1, N_MEM, GROUP_W), lambda b, t: (layer, b, 0, 0))
    in_specs = [
        pl.BlockSpec((1, TM, D_MODEL), lambda b, t: (b, t, 0)),
        pl.BlockSpec((1, TM, TILE_COLS), lambda b, t: (b, t, 0)),
        pl.BlockSpec((1, SEQ, KV_COLS), lambda b, t: (b, 0, 0)),
        pl.BlockSpec((1, GROUP_W // LANES, GRID_W, ROWS_PER_TILE, LANES), lambda b, t: (b, 0, 0, t, 0)),
        mem_spec, mem_spec,
        _layer_spec((D_MODEL, D_MODEL), layer),
        _layer_spec((1, GROUP_W), layer), _layer_spec((1, GROUP_W), layer),
        _layer_spec((CHUNK, N_SUB * CHUNK), layer),
        _layer_spec((CHUNK, GROUP_W), layer),
        pl.BlockSpec((2, 2 * WIN_H - 4, N_SUB * GRID_W, LANES), lambda b, t: (0, 0, layer, 0)),
    ]
    return pl.pallas_call(
        _mixer_kernel,
        grid=(BATCH, SEQ // TM),
        in_specs=in_specs,
        out_specs=pl.BlockSpec((1, TM, D_MODEL), lambda b, t: (b, t, 0)),
        out_shape=jax.ShapeDtypeStruct((BATCH, SEQ, D_MODEL), F32),
        scratch_shapes=[pltpu.VMEM((TM, D_MODEL), BF16), pltpu.VMEM((TM, GROUP_W), F32)],
        compiler_params=_params(2),
        name="mixer",
    )(x, tiles, kv, zc, km, vm, w_out, ln_g, ln_b, ws_cat, bs_tab, bias_tab)


def _tile_heads(g):
    return jnp.tile(g.astype(F32), (1, N_SUB))


def kernel(x, mem, norm_g, w_in, w_out, gm_ln_g, gm_ln_b, gm_w_s, gm_b_s, na_qn_g, na_kn_g, na_rpb, fn_w,
           mem_norm_g, mem_w_kv, mem_qn_g, mem_kn_g):
    g1_np, m2_np, m3_np = _dft_tables()
    g1 = jnp.asarray(g1_np).astype(BF16)
    m2 = jnp.asarray(m2_np).astype(BF16)
    m3 = jnp.asarray(m3_np).astype(BF16)
    ones_bd = jnp.asarray(_head_mean_matrix()).astype(BF16)
    qk_scale = HEAD_DIM ** -0.5 * LOG2E

    w_in_b = w_in.astype(BF16)
    w_out_b = w_out.astype(BF16)
    norm_g3 = norm_g.reshape(DEPTH, 1, D_MODEL)
    head_gains = jnp.stack([_tile_heads(na_qn_g) * qk_scale, _tile_heads(na_kn_g),
                            _tile_heads(mem_qn_g) * qk_scale], axis=1)
    head_gains = jnp.pad(head_gains, ((0, 0), (0, 5), (0, 0)))
    eye = jnp.eye(N_SUB, dtype=fn_w.dtype)
    wf_bd = (fn_w[:, :, :, None, :] * eye[None, :, None, :, None]).reshape(DEPTH, GROUP_W, GROUP_W).astype(BF16)
    ws_cat = gm_w_s.transpose(0, 2, 1, 3).reshape(DEPTH, CHUNK, N_SUB * CHUNK).astype(BF16)
    bs_tab = jnp.repeat(gm_b_s.transpose(0, 2, 1), HEAD_DIM, axis=2)
    ln_g3 = gm_ln_g.reshape(DEPTH, 1, GROUP_W)
    ln_b3 = gm_ln_b.reshape(DEPTH, 1, GROUP_W)
    bias_tab = _natten_bias_table(na_rpb * LOG2E)

    km, vm = _mem_kv(mem, mem_norm_g.reshape(DEPTH, 1, D_MODEL), mem_w_kv.astype(BF16),
                     _tile_heads(mem_kn_g).reshape(DEPTH, 1, GROUP_W), ones_bd)
    for l in range(DEPTH):
        tiles, kv, c_in = _in_proj(l, x.reshape(BATCH * SEQ, D_MODEL), norm_g3, w_in_b, head_gains, ones_bd)
        zc = _fnet(l, c_in, g1, m2, m3, wf_bd)
        x = _mixer(l, x, tiles.reshape(BATCH, SEQ, TILE_COLS), kv.reshape(BATCH, SEQ, KV_COLS), zc, km, vm,
                   w_out_b, ln_g3, ln_b3, ws_cat, bs_tab, bias_tab)
    return x
```

```python
import functools

import numpy as np
import jax
import jax.numpy as jnp
from jax import lax
from jax.experimental import pallas as pl
from jax.experimental.pallas import tpu as pltpu

D_MODEL = 1024
BATCH = 8
SEQ = 4096
DEPTH = 2
N_MEM = 256
GROUP_W = 256
N_SUB = 4
HEAD_DIM = 64
CHUNK = 128
GRID_W = 64
GRID_H = SEQ // GRID_W
WIN_H = 8
WIN_W = 16
N_IN_SLOTS = 11
IN_COLS = N_IN_SLOTS * GROUP_W
EPS = 1e-6
NEG = -1e30
LOG2E = 1.4426950408889634

IN_SLOTS = ("a_u", "a_v", "a_g", "b_q", "b_k", "b_v", "b_g", "c_in", "c_g", "d_q", "d_g")
TILE_SLOTS = ("a_u", "a_v", "a_g", "b_q", "b_g", "c_g", "d_q", "d_g")
KV_SLOTS = ("b_k", "b_v")
TILE_COLS = len(TILE_SLOTS) * GROUP_W
KV_COLS = len(KV_SLOTS) * GROUP_W

KEY_HALF = GRID_W // 2
NATTEN_BLOCKS = ((0, 24, (0,)), (24, 16, (0, 1)), (40, 24, (1,)))
MEM_KV_BATCH = 8
TM = 512
ROWS_PER_TILE = TM // GRID_W
TM_IN = 1024
FFT_G = 8

LANES = 128
F32 = jnp.float32
BF16 = jnp.bfloat16
VMEM_LIMIT = 56 * 1024 * 1024


@functools.lru_cache(maxsize=None)
def _dft_tables():
    g = np.arange(GRID_W // FFT_G)
    j = np.arange(FFT_G)
    k = np.arange(GRID_W)
    n = np.arange(GRID_W)
    s = GRID_W * n[None, None, None, :] + FFT_G * g[:, None, None, None] + j[None, :, None, None]
    phase = (k[None, None, :, None] * s) % SEQ
    ang = 2.0 * np.pi * phase.astype(np.float64) / SEQ
    g1 = np.zeros((len(g), 2, GRID_W, FFT_G, GRID_W, FFT_G), np.float64)
    for jj in range(FFT_G):
        g1[:, 0, :, jj, :, jj] = np.cos(ang[:, jj])
        g1[:, 1, :, jj, :, jj] = np.sin(ang[:, jj])
    g1 = g1.reshape(len(g), 2 * GRID_W * FFT_G, GRID_W * FFT_G)
    ang2 = 2.0 * np.pi * ((k[:, None] * n[None, :]) % GRID_W).astype(np.float64) / GRID_W
    c2, s2 = np.cos(ang2), np.sin(ang2)
    m2 = np.block([[c2, -s2], [s2, c2]])
    d = np.arange(HEAD_DIM)
    ang3 = 2.0 * np.pi * ((d[:, None] * d[None, :]) % HEAD_DIM).astype(np.float64) / HEAD_DIM
    scale = 1.0 / np.sqrt(float(SEQ * HEAD_DIM))
    cd = np.kron(np.eye(N_SUB), np.cos(ang3)) * scale
    sd = np.kron(np.eye(N_SUB), np.sin(ang3)) * scale
    m3 = np.concatenate([cd, -sd], axis=0)
    return g1.astype(np.float32), m2.astype(np.float32), m3.astype(np.float32)


@functools.lru_cache(maxsize=None)
def _head_mean_matrix():
    return np.kron(np.eye(N_SUB), np.full((HEAD_DIM, HEAD_DIM), 1.0 / HEAD_DIM)).astype(np.float32)


def _natten_bias_table(rpb):
    n_lh = rpb.shape[0] * rpb.shape[1]
    rpb = rpb.reshape(n_lh, 2 * WIN_H - 1, 2 * WIN_W - 1).astype(F32)
    c = np.arange(GRID_W)
    col_start = np.clip(c - WIN_W // 2, 0, GRID_W - WIN_W)
    col_ok = (c[None, :] >= col_start[:, None]) & (c[None, :] < col_start[:, None] + WIN_W)
    for q0, nq, halves in NATTEN_BLOCKS:
        for wb in range(2):
            assert wb in halves or not col_ok[q0:q0 + nq, wb * KEY_HALF:(wb + 1) * KEY_HALF].any()
    d_col = c[None, :] - c[:, None] + (WIN_W - 1)
    onehot = (d_col[None, :, :] == np.arange(2 * WIN_W - 1)[:, None, None]).astype(np.float32)
    toe = jnp.einsum("hij,jcw->ihcw", rpb, onehot, precision=lax.Precision.HIGHEST)
    toe = jnp.where(col_ok[None, None, :, :], toe, NEG)
    toe = toe.reshape(2 * WIN_H - 1, n_lh * GRID_W, GRID_W)
    n_i, n_d = 2 * WIN_H - 1 - 3, 4
    toe4 = jnp.concatenate([toe[d:d + n_i] for d in range(n_d)], axis=-1)
    src = np.arange(n_d * GRID_W).reshape(n_d, 2, KEY_HALF).transpose(1, 0, 2).reshape(-1)
    perm = (np.arange(n_d * GRID_W)[:, None] == src[None, :]).astype(np.float32)
    return jnp.einsum("irk,kn->irn", toe4, perm, precision=lax.Precision.HIGHEST)


def _head_rmsnorm(y, gain, ones_bd):
    ms = jnp.dot((y * y).astype(BF16), ones_bd, preferred_element_type=F32)
    return y * lax.rsqrt(ms + EPS) * gain


def _silu(g):
    h = 0.5 * g
    return h + h * jnp.tanh(h)


def _lane_head(rows):
    return lax.broadcasted_iota(jnp.int32, (rows, GROUP_W), 1) // HEAD_DIM


def _stack_heads(q, lane_head):
    qf = q.astype(F32)
    return jnp.concatenate(
        [jnp.where(lane_head == h, qf, 0.0) for h in range(N_SUB)], axis=0).astype(BF16)


def _pick_heads(o, lane_head, rows):
    out = o[0:rows]
    for h in range(1, N_SUB):
        out = jnp.where(lane_head == h, o[h * rows:(h + 1) * rows], out)
    return out


def _softmax_pv(s, v, lane_head, rows):
    m = jnp.max(s, axis=-1, keepdims=True)
    e = jnp.exp2(s - m)
    inv = 1.0 / jnp.sum(e, axis=-1, keepdims=True)
    o = jnp.dot(e.astype(BF16), v, preferred_element_type=F32)
    return _pick_heads(o, lane_head, rows) * _pick_heads(inv, lane_head, rows)


_NT = (((1,), (1,)), ((), ()))


def _mem_kv_kernel(mem_ref, g_ref, w_ref, kg_ref, ones_ref, k_ref, v_ref):
    m = mem_ref[...].reshape(MEM_KV_BATCH * N_MEM, D_MODEL)
    ms = jnp.mean(m * m, axis=-1, keepdims=True)
    mn = (m * lax.rsqrt(ms + EPS) * g_ref[...]).astype(BF16)
    kv = jnp.dot(mn, w_ref[...], preferred_element_type=F32)
    k = _head_rmsnorm(kv[:, :GROUP_W], kg_ref[...], ones_ref[...])
    k_ref[...] = k.astype(BF16).reshape(MEM_KV_BATCH, N_MEM, GROUP_W)
    v_ref[...] = kv[:, GROUP_W:].astype(BF16).reshape(MEM_KV_BATCH, N_MEM, GROUP_W)


def _in_proj_kernel(x_ref, g_ref, w_ref, hg_ref, ones_ref, tile_ref, kv_ref, cin_ref):
    head_gain_row = {"b_q": 0, "b_k": 1, "d_q": 2}
    x = x_ref[...]
    inv = lax.rsqrt(jnp.mean(x * x, axis=-1, keepdims=True) + EPS)
    h = (x * g_ref[...]).astype(BF16)
    y_all = jnp.dot(h, w_ref[...], preferred_element_type=F32)
    for j, name in enumerate(IN_SLOTS):
        y = y_all[:, j * GROUP_W:(j + 1) * GROUP_W] * inv
        if name in head_gain_row:
            r = head_gain_row[name]
            y = _head_rmsnorm(y, hg_ref[r:r + 1, :], ones_ref[...])
        if name == "c_in":
            cin_ref[...] = y
        elif name in KV_SLOTS:
            col = KV_SLOTS.index(name) * GROUP_W
            kv_ref[:, col:col + GROUP_W] = y.astype(BF16)
        else:
            col = TILE_SLOTS.index(name) * GROUP_W
            tile_ref[:, col:col + GROUP_W] = y.astype(BF16)


def _fnet_kernel(x_ref, g1_ref, m2_ref, m3_ref, wf_ref, z_ref, t_ref):
    half = GRID_W * FFT_G

    def stage1(g, carry):
        n0 = pl.multiple_of(g * FFT_G, FFT_G)
        x = x_ref[0, :, pl.ds(n0, FFT_G), :].reshape(half, GROUP_W).astype(BF16)
        y = jnp.dot(g1_ref[g], x, preferred_element_type=F32)
        t_ref[:, :, pl.ds(n0, FFT_G), :] = y.reshape(2, GRID_W, FFT_G, GROUP_W)
        return carry

    m3w = jnp.dot(m3_ref[...], wf_ref[...], preferred_element_type=F32).astype(BF16)

    def stage2(g, carry):
        k0 = g * FFT_G
        t = jnp.concatenate(
            [jnp.concatenate([t_ref[0, k0 + j], t_ref[1, k0 + j]], axis=0) for j in range(FFT_G)],
            axis=-1).astype(BF16)
        ab = jnp.dot(m2_ref[...], t, preferred_element_type=F32)
        ab = jnp.concatenate(
            [jnp.concatenate([ab[:GRID_W, j * GROUP_W:(j + 1) * GROUP_W],
                              ab[GRID_W:, j * GROUP_W:(j + 1) * GROUP_W]], axis=-1) for j in range(FFT_G)],
            axis=0).astype(BF16)
        zc = jnp.dot(ab, m3w, preferred_element_type=F32)
        zc = zc.reshape(FFT_G, GRID_W, GROUP_W)
        for hf in range(GROUP_W // LANES):
            z_ref[0, hf, pl.ds(k0, FFT_G), :, :] = zc[:, :, hf * LANES:(hf + 1) * LANES]
        return carry

    n_groups = GRID_W // FFT_G
    lax.fori_loop(0, n_groups, stage1, 0, unroll=True)
    lax.fori_loop(0, n_groups, stage2, 0, unroll=True)


def _mixer_kernel(x_ref, tile_ref, kv_ref, zc_ref, km_ref, vm_ref, wout_ref, lng_ref, lnb_ref, ws_ref, bs_ref,
                  bias_ref, o_ref, y_ref, yb_ref):
    t = pl.program_id(1)

    def slot(name, rows=slice(None)):
        col = TILE_SLOTS.index(name) * GROUP_W
        return tile_ref[0, rows, col:col + GROUP_W]

    k_col, v_col = (KV_SLOTS.index(name) * GROUP_W for name in ("b_k", "b_v"))

    lh_tile = _lane_head(TM)
    qs = _stack_heads(slot("d_q"), lh_tile)
    s = lax.dot_general(qs, km_ref[0], _NT, preferred_element_type=F32)
    yd = _softmax_pv(s, vm_ref[0], lh_tile, TM)
    y_ref[:, 3 * GROUP_W:4 * GROUP_W] = (yd * _silu(slot("d_g").astype(F32))).astype(BF16)

    lh_blk = {nq: _lane_head(nq) for _, nq, _ in NATTEN_BLOCKS}
    half_keys = WIN_H * KEY_HALF
    for i in range(ROWS_PER_TILE):
        r = t * ROWS_PER_TILE + i
        rs = jnp.clip(r - WIN_H // 2, 0, GRID_H - WIN_H)
        k0 = pl.multiple_of(rs * GRID_W, GRID_W)
        q_row = slot("b_q", slice(i * GRID_W, (i + 1) * GRID_W)).astype(F32)
        qs = jnp.concatenate([jnp.where(lh_blk[nq] == h, q_row[q0:q0 + nq], 0.0)
                              for q0, nq, _ in NATTEN_BLOCKS for h in range(N_SUB)], axis=0).astype(BF16)
        key_rows = [pl.ds(k0 + (k * GRID_W + wb * KEY_HALF), KEY_HALF) for wb in range(2) for k in range(WIN_H)]
        kw = jnp.concatenate([kv_ref[0, rows, k_col:k_col + GROUP_W] for rows in key_rows], axis=0)
        vw = jnp.concatenate([kv_ref[0, rows, v_col:v_col + GROUP_W] for rows in key_rows], axis=0)
        s = lax.dot_general(qs, kw, _NT, preferred_element_type=F32)
        d0 = (WIN_H - 1) - (r - rs)
        probs, invs, row0 = [], [], 0
        for q0, nq, halves in NATTEN_BLOCKS:
            nr = N_SUB * nq
            c0, c1 = halves[0] * half_keys, (halves[-1] + 1) * half_keys
            bias = jnp.concatenate(
                [jnp.concatenate([bias_ref[d0 + 4 * m, h * GRID_W + q0:h * GRID_W + q0 + nq,
                                           wb * LANES:(wb + 1) * LANES] for h in range(N_SUB)], axis=0)
                 for wb in halves for m in range(2)], axis=-1)
            sb = s[row0:row0 + nr, c0:c1] + bias
            e = jnp.exp2(sb - jnp.max(sb, axis=-1, keepdims=True))
            invs.append(1.0 / jnp.sum(e, axis=-1, keepdims=True))
            pieces = ([jnp.zeros((nr, c0), BF16)] if c0 else []) + [e.astype(BF16)]
            pieces += [jnp.zeros((nr, 2 * half_keys - c1), BF16)] if c1 < 2 * half_keys else []
            probs.append(jnp.concatenate(pieces, axis=-1))
            row0 += nr
        o = jnp.dot(jnp.concatenate(probs, axis=0), vw, preferred_element_type=F32)
        outs, row0 = [], 0
        for (q0, nq, _), inv in zip(NATTEN_BLOCKS, invs):
            nr = N_SUB * nq
            outs.append(_pick_heads(o[row0:row0 + nr], lh_blk[nq], nq) * _pick_heads(inv, lh_blk[nq], nq))
            row0 += nr
        yb_ref[i * GRID_W:(i + 1) * GRID_W, :] = jnp.concatenate(outs, axis=0)
    y_ref[:, GROUP_W:2 * GROUP_W] = (yb_ref[...] * _silu(slot("b_g").astype(F32))).astype(BF16)

    lh_chunk = _lane_head(CHUNK)
    for c in range(TM // CHUNK):
        rows = slice(c * CHUNK, (c + 1) * CHUNK)
        v = slot("a_v", rows).astype(F32)
        mu = jnp.mean(v, axis=-1, keepdims=True)
        var = jnp.mean(jnp.square(v - mu), axis=-1, keepdims=True)
        vn = (v - mu) * lax.rsqrt(var + EPS) * lng_ref[...] + lnb_ref[...]
        s = jnp.dot(ws_ref[...], _stack_heads(vn, lh_chunk), preferred_element_type=F32) + bs_ref[...]
        ya = slot("a_u", rows).astype(F32) * s * _silu(slot("a_g", rows).astype(F32))
        y_ref[rows, 0:GROUP_W] = ya.astype(BF16)

    zc_halves = [zc_ref.at[0, hf].reshape(GRID_W * ROWS_PER_TILE, LANES) for hf in range(GROUP_W // LANES)]
    zc = jnp.concatenate(
        [jnp.concatenate([zh[pl.ds(i, GRID_W, stride=ROWS_PER_TILE), :] for zh in zc_halves], axis=-1)
         for i in range(ROWS_PER_TILE)], axis=0)
    y_ref[:, 2 * GROUP_W:3 * GROUP_W] = (zc * _silu(slot("c_g").astype(F32))).astype(BF16)

    o_ref[0] = x_ref[0] + jnp.dot(y_ref[...], wout_ref[...], preferred_element_type=F32)


def _const_spec(shape):
    nd = len(shape)
    return pl.BlockSpec(shape, lambda *_: (0,) * nd)


def _layer_spec(shape, layer):
    nd = len(shape)
    return pl.BlockSpec((None,) + tuple(shape), lambda *_: (layer,) + (0,) * nd)


def _params(dims):
    return pltpu.CompilerParams(dimension_semantics=("arbitrary",) * dims, vmem_limit_bytes=VMEM_LIMIT)


def _mem_kv(mem, g, w_kv, kg, ones_bd):
    return pl.pallas_call(
        _mem_kv_kernel,
        grid=(DEPTH, BATCH // MEM_KV_BATCH),
        in_specs=[
            pl.BlockSpec((MEM_KV_BATCH, N_MEM, D_MODEL), lambda l, b: (b, 0, 0)),
            pl.BlockSpec((None, 1, D_MODEL), lambda l, b: (l, 0, 0)),
            pl.BlockSpec((None, D_MODEL, 2 * GROUP_W), lambda l, b: (l, 0, 0)),
            pl.BlockSpec((None, 1, GROUP_W), lambda l, b: (l, 0, 0)),
            _const_spec((GROUP_W, GROUP_W)),
        ],
        out_specs=[pl.BlockSpec((None, MEM_KV_BATCH, N_MEM, GROUP_W), lambda l, b: (l, b, 0, 0))] * 2,
        out_shape=[jax.ShapeDtypeStruct((DEPTH, BATCH, N_MEM, GROUP_W), BF16)] * 2,
        compiler_params=_params(2),
        name="mem_kv",
    )(mem, g, w_kv, kg, ones_bd)


def _in_proj(layer, x2d, g, w_in, head_gains, ones_bd):
    n_tok = x2d.shape[0]
    return pl.pallas_call(
        _in_proj_kernel,
        grid=(n_tok // TM_IN,),
        in_specs=[
            pl.BlockSpec((TM_IN, D_MODEL), lambda i: (i, 0)),
            _layer_spec((1, D_MODEL), layer),
            _layer_spec((D_MODEL, IN_COLS), layer),
            _layer_spec((8, GROUP_W), layer),
            _const_spec((GROUP_W, GROUP_W)),
        ],
        out_specs=[
            pl.BlockSpec((TM_IN, TILE_COLS), lambda i: (i, 0)),
            pl.BlockSpec((TM_IN, KV_COLS), lambda i: (i, 0)),
            pl.BlockSpec((TM_IN, GROUP_W), lambda i: (i, 0)),
        ],
        out_shape=[
            jax.ShapeDtypeStruct((n_tok, TILE_COLS), BF16),
            jax.ShapeDtypeStruct((n_tok, KV_COLS), BF16),
            jax.ShapeDtypeStruct((n_tok, GROUP_W), F32),
        ],
        compiler_params=_params(1),
        name="in_proj",
    )(x2d, g, w_in, head_gains, ones_bd)


def _fnet(layer, c_in, g1, m2, m3, wf_bd):
    x4 = c_in.reshape(BATCH, GRID_W, GRID_W, GROUP_W)
    n_half = GROUP_W // LANES
    return pl.pallas_call(
        _fnet_kernel,
        grid=(BATCH,),
        in_specs=[pl.BlockSpec((1, GRID_W, GRID_W, GROUP_W), lambda b: (b, 0, 0, 0)),
                  _const_spec(g1.shape), _const_spec(m2.shape), _const_spec(m3.shape),
                  _layer_spec((GROUP_W, GROUP_W), layer)],
        out_specs=pl.BlockSpec((1, n_half, GRID_W, GRID_W, LANES), lambda b: (b, 0, 0, 0, 0)),
        out_shape=jax.ShapeDtypeStruct((BATCH, n_half, GRID_W, GRID_W, LANES), F32),
        scratch_shapes=[pltpu.VMEM((2, GRID_W, GRID_W, GROUP_W), F32)],
        compiler_params=_params(1),
        name="fnet",
    )(x4, g1, m2, m3, wf_bd)


def _mixer(layer, x, tiles, kv, zc, km, vm, w_out, ln_g, ln_b, ws_cat, bs_tab, bias_tab):
    mem_spec = pl.BlockSpec((None, 1, N_MEM, GROUP_W), lambda b, t: (layer, b, 0, 0))
    in_specs = [
        pl.BlockSpec((1, TM, D_MODEL), lambda b, t: (b, t, 0)),
        pl.BlockSpec((1, TM, TILE_COLS), lambda b, t: (b, t, 0)),
        pl.BlockSpec((1, SEQ, KV_COLS), lambda b, t: (b, 0, 0)),
        pl.BlockSpec((1, GROUP_W // LANES, GRID_W, ROWS_PER_TILE, LANES), lambda b, t: (b, 0, 0, t, 0)),
        mem_spec, mem_spec,
        _layer_spec((D_MODEL, D_MODEL), layer),
        _layer_spec((1, GROUP_W), layer), _layer_spec((1, GROUP_W), layer),
        _layer_spec((CHUNK, N_SUB * CHUNK), layer),
        _layer_spec((CHUNK, GROUP_W), layer),
        pl.BlockSpec((2 * WIN_H - 4, N_SUB * GRID_W, 2 * LANES), lambda b, t: (0, layer, 0)),
    ]
    return pl.pallas_call(
        _mixer_kernel,
        grid=(BATCH, SEQ // TM),
        in_specs=in_specs,
        out_specs=pl.BlockSpec((1, TM, D_MODEL), lambda b, t: (b, t, 0)),
        out_shape=jax.ShapeDtypeStruct((BATCH, SEQ, D_MODEL), F32),
        scratch_shapes=[pltpu.VMEM((TM, D_MODEL), BF16), pltpu.VMEM((TM, GROUP_W), F32)],
        compiler_params=_params(2),
        name="mixer",
    )(x, tiles, kv, zc, km, vm, w_out, ln_g, ln_b, ws_cat, bs_tab, bias_tab)


def _tile_heads(g):
    return jnp.tile(g.astype(F32), (1, N_SUB))


def kernel(x, mem, norm_g, w_in, w_out, gm_ln_g, gm_ln_b, gm_w_s, gm_b_s, na_qn_g, na_kn_g, na_rpb, fn_w,
           mem_norm_g, mem_w_kv, mem_qn_g, mem_kn_g):
    g1_np, m2_np, m3_np = _dft_tables()
    g1 = jnp.asarray(g1_np).astype(BF16)
    m2 = jnp.asarray(m2_np).astype(BF16)
    m3 = jnp.asarray(m3_np).astype(BF16)
    ones_bd = jnp.asarray(_head_mean_matrix()).astype(BF16)
    qk_scale = HEAD_DIM ** -0.5 * LOG2E

    w_in_b = w_in.astype(BF16)
    w_out_b = w_out.astype(BF16)
    norm_g3 = norm_g.reshape(DEPTH, 1, D_MODEL)
    head_gains = jnp.stack([_tile_heads(na_qn_g) * qk_scale, _tile_heads(na_kn_g),
                            _tile_heads(mem_qn_g) * qk_scale], axis=1)
    head_gains = jnp.pad(head_gains, ((0, 0), (0, 5), (0, 0)))
    eye = jnp.eye(N_SUB, dtype=fn_w.dtype)
    wf_bd = (fn_w[:, :, :, None, :] * eye[None, :, None, :, None]).reshape(DEPTH, GROUP_W, GROUP_W).astype(BF16)
    ws_cat = gm_w_s.transpose(0, 2, 1, 3).reshape(DEPTH, CHUNK, N_SUB * CHUNK).astype(BF16)
    bs_tab = jnp.repeat(gm_b_s.transpose(0, 2, 1), HEAD_DIM, axis=2)
    ln_g3 = gm_ln_g.reshape(DEPTH, 1, GROUP_W)
    ln_b3 = gm_ln_b.reshape(DEPTH, 1, GROUP_W)
    bias_tab = _natten_bias_table(na_rpb * LOG2E)

    km, vm = _mem_kv(mem, mem_norm_g.reshape(DEPTH, 1, D_MODEL), mem_w_kv.astype(BF16),
                     _tile_heads(mem_kn_g).reshape(DEPTH, 1, GROUP_W), ones_bd)
    for l in range(DEPTH):
        tiles, kv, c_in = _in_proj(l, x.reshape(BATCH * SEQ, D_MODEL), norm_g3, w_in_b, head_gains, ones_bd)
        zc = _fnet(l, c_in, g1, m2, m3, wf_bd)
        x = _mixer(l, x, tiles.reshape(BATCH, SEQ, TILE_COLS), kv.reshape(BATCH, SEQ, KV_COLS), zc, km, vm,
                   w_out_b, ln_g3, ln_b3, ws_cat, bs_tab, bias_tab)
    return x
```

```python
import functools

import numpy as np
import jax
import jax.numpy as jnp
from jax import lax
from jax.experimental import pallas as pl
from jax.experimental.pallas import tpu as pltpu

D_MODEL = 1024
BATCH = 8
SEQ = 4096
DEPTH = 2
N_MEM = 256
GROUP_W = 256
N_SUB = 4
HEAD_DIM = 64
CHUNK = 128
GRID_W = 64
GRID_H = SEQ // GRID_W
WIN_H = 8
WIN_W = 16
N_IN_SLOTS = 11
IN_COLS = N_IN_SLOTS * GROUP_W
EPS = 1e-6
NEG = -1e30
LOG2E = 1.4426950408889634

IN_SLOTS = ("a_u", "a_v", "a_g", "b_q", "b_k", "b_v", "b_g", "c_in", "c_g", "d_q", "d_g")
TILE_SLOTS = ("a_u", "a_v", "a_g", "b_q", "b_g", "c_g", "d_q", "d_g")
KV_SLOTS = ("b_k", "b_v")
TILE_COLS = len(TILE_SLOTS) * GROUP_W
KV_COLS = len(KV_SLOTS) * GROUP_W

KEY_HALF = GRID_W // 2
NATTEN_BLOCKS = ((0, 24, (0,)), (24, 16, (0, 1)), (40, 24, (1,)))
MEM_KV_BATCH = 8
TM = 1024
ROWS_PER_TILE = TM // GRID_W
TM_IN = 1024
FFT_G = 8

LANES = 128
F32 = jnp.float32
BF16 = jnp.bfloat16
VMEM_LIMIT = 56 * 1024 * 1024


@functools.lru_cache(maxsize=None)
def _dft_tables():
    g = np.arange(GRID_W // FFT_G)
    j = np.arange(FFT_G)
    k = np.arange(GRID_W)
    n = np.arange(GRID_W)
    s = GRID_W * n[None, None, None, :] + FFT_G * g[:, None, None, None] + j[None, :, None, None]
    phase = (k[None, None, :, None] * s) % SEQ
    ang = 2.0 * np.pi * phase.astype(np.float64) / SEQ
    g1 = np.zeros((len(g), 2, GRID_W, FFT_G, GRID_W, FFT_G), np.float64)
    for jj in range(FFT_G):
        g1[:, 0, :, jj, :, jj] = np.cos(ang[:, jj])
        g1[:, 1, :, jj, :, jj] = np.sin(ang[:, jj])
    g1 = g1.reshape(len(g), 2 * GRID_W * FFT_G, GRID_W * FFT_G)
    ang2 = 2.0 * np.pi * ((k[:, None] * n[None, :]) % GRID_W).astype(np.float64) / GRID_W
    c2, s2 = np.cos(ang2), np.sin(ang2)
    m2 = np.block([[c2, -s2], [s2, c2]])
    d = np.arange(HEAD_DIM)
    ang3 = 2.0 * np.pi * ((d[:, None] * d[None, :]) % HEAD_DIM).astype(np.float64) / HEAD_DIM
    scale = 1.0 / np.sqrt(float(SEQ * HEAD_DIM))
    cd = np.kron(np.eye(N_SUB), np.cos(ang3)) * scale
    sd = np.kron(np.eye(N_SUB), np.sin(ang3)) * scale
    m3 = np.concatenate([cd, -sd], axis=0)
    return g1.astype(np.float32), m2.astype(np.float32), m3.astype(np.float32)


@functools.lru_cache(maxsize=None)
def _head_mean_matrix():
    return np.kron(np.eye(N_SUB), np.full((HEAD_DIM, HEAD_DIM), 1.0 / HEAD_DIM)).astype(np.float32)


def _natten_bias_table(rpb):
    n_lh = rpb.shape[0] * rpb.shape[1]
    rpb = rpb.reshape(n_lh, 2 * WIN_H - 1, 2 * WIN_W - 1).astype(F32)
    c = np.arange(GRID_W)
    col_start = np.clip(c - WIN_W // 2, 0, GRID_W - WIN_W)
    col_ok = (c[None, :] >= col_start[:, None]) & (c[None, :] < col_start[:, None] + WIN_W)
    for q0, nq, halves in NATTEN_BLOCKS:
        for wb in range(2):
            assert wb in halves or not col_ok[q0:q0 + nq, wb * KEY_HALF:(wb + 1) * KEY_HALF].any()
    d_col = c[None, :] - c[:, None] + (WIN_W - 1)
    onehot = (d_col[None, :, :] == np.arange(2 * WIN_W - 1)[:, None, None]).astype(np.float32)
    toe = jnp.einsum("hij,jcw->ihcw", rpb, onehot, precision=lax.Precision.HIGHEST)
    toe = jnp.where(col_ok[None, None, :, :], toe, NEG)
    toe = toe.reshape(2 * WIN_H - 1, n_lh * GRID_W, GRID_W)
    n_i, n_d = 2 * WIN_H - 1 - 3, 4
    toe4 = jnp.concatenate([toe[d:d + n_i] for d in range(n_d)], axis=-1)
    src = np.arange(n_d * GRID_W).reshape(n_d, 2, KEY_HALF).transpose(1, 0, 2).reshape(-1)
    perm = (np.arange(n_d * GRID_W)[:, None] == src[None, :]).astype(np.float32)
    return jnp.einsum("irk,kn->irn", toe4, perm, precision=lax.Precision.HIGHEST)


def _head_rmsnorm(y, gain, ones_bd):
    ms = jnp.dot((y * y).astype(BF16), ones_bd, preferred_element_type=F32)
    return y * lax.rsqrt(ms + EPS) * gain


def _silu(g):
    h = 0.5 * g
    return h + h * jnp.tanh(h)


def _lane_head(rows):
    return lax.broadcasted_iota(jnp.int32, (rows, GROUP_W), 1) // HEAD_DIM


def _stack_heads(q, lane_head):
    qf = q.astype(F32)
    return jnp.concatenate(
        [jnp.where(lane_head == h, qf, 0.0) for h in range(N_SUB)], axis=0).astype(BF16)


def _pick_heads(o, lane_head, rows):
    out = o[0:rows]
    for h in range(1, N_SUB):
        out = jnp.where(lane_head == h, o[h * rows:(h + 1) * rows], out)
    return out


def _softmax_pv(s, v, lane_head, rows):
    m = jnp.max(s, axis=-1, keepdims=True)
    e = jnp.exp2(s - m)
    inv = 1.0 / jnp.sum(e, axis=-1, keepdims=True)
    o = jnp.dot(e.astype(BF16), v, preferred_element_type=F32)
    return _pick_heads(o, lane_head, rows) * _pick_heads(inv, lane_head, rows)


_NT = (((1,), (1,)), ((), ()))


def _mem_kv_kernel(mem_ref, g_ref, w_ref, kg_ref, ones_ref, k_ref, v_ref):
    m = mem_ref[...].reshape(MEM_KV_BATCH * N_MEM, D_MODEL)
    ms = jnp.mean(m * m, axis=-1, keepdims=True)
    mn = (m * lax.rsqrt(ms + EPS) * g_ref[...]).astype(BF16)
    kv = jnp.dot(mn, w_ref[...], preferred_element_type=F32)
    k = _head_rmsnorm(kv[:, :GROUP_W], kg_ref[...], ones_ref[...])
    k_ref[...] = k.astype(BF16).reshape(MEM_KV_BATCH, N_MEM, GROUP_W)
    v_ref[...] = kv[:, GROUP_W:].astype(BF16).reshape(MEM_KV_BATCH, N_MEM, GROUP_W)


def _in_proj_kernel(x_ref, g_ref, w_ref, hg_ref, ones_ref, tile_ref, kv_ref, cin_ref):
    head_gain_row = {"b_q": 0, "b_k": 1, "d_q": 2}
    x = x_ref[...]
    inv = lax.rsqrt(jnp.mean(x * x, axis=-1, keepdims=True) + EPS)
    h = (x * g_ref[...]).astype(BF16)
    y_all = jnp.dot(h, w_ref[...], preferred_element_type=F32)
    for j, name in enumerate(IN_SLOTS):
        y = y_all[:, j * GROUP_W:(j + 1) * GROUP_W] * inv
        if name in head_gain_row:
            r = head_gain_row[name]
            y = _head_rmsnorm(y, hg_ref[r:r + 1, :], ones_ref[...])
        if name == "c_in":
            cin_ref[...] = y
        elif name in KV_SLOTS:
            col = KV_SLOTS.index(name) * GROUP_W
            kv_ref[:, col:col + GROUP_W] = y.astype(BF16)
        else:
            col = TILE_SLOTS.index(name) * GROUP_W
            tile_ref[:, col:col + GROUP_W] = y.astype(BF16)


def _fnet_kernel(x_ref, g1_ref, m2_ref, m3_ref, wf_ref, z_ref, t_ref):
    half = GRID_W * FFT_G

    def stage1(g, carry):
        n0 = pl.multiple_of(g * FFT_G, FFT_G)
        x = x_ref[0, :, pl.ds(n0, FFT_G), :].reshape(half, GROUP_W).astype(BF16)
        y = jnp.dot(g1_ref[g], x, preferred_element_type=F32)
        t_ref[:, :, pl.ds(n0, FFT_G), :] = y.reshape(2, GRID_W, FFT_G, GROUP_W)
        return carry

    m3w = jnp.dot(m3_ref[...], wf_ref[...], preferred_element_type=F32).astype(BF16)

    def stage2(g, carry):
        k0 = g * FFT_G
        t = jnp.concatenate(
            [jnp.concatenate([t_ref[0, k0 + j], t_ref[1, k0 + j]], axis=0) for j in range(FFT_G)],
            axis=-1).astype(BF16)
        ab = jnp.dot(m2_ref[...], t, preferred_element_type=F32)
        ab = jnp.concatenate(
            [jnp.concatenate([ab[:GRID_W, j * GROUP_W:(j + 1) * GROUP_W],
                              ab[GRID_W:, j * GROUP_W:(j + 1) * GROUP_W]], axis=-1) for j in range(FFT_G)],
            axis=0).astype(BF16)
        zc = jnp.dot(ab, m3w, preferred_element_type=F32)
        zc = zc.reshape(FFT_G, GRID_W, GROUP_W)
        for hf in range(GROUP_W // LANES):
            z_ref[0, hf, pl.ds(k0, FFT_G), :, :] = zc[:, :, hf * LANES:(hf + 1) * LANES]
        return carry

    n_groups = GRID_W // FFT_G
    lax.fori_loop(0, n_groups, stage1, 0, unroll=True)
    lax.fori_loop(0, n_groups, stage2, 0, unroll=True)


def _mixer_kernel(x_ref, tile_ref, kv_ref, zc_ref, km_ref, vm_ref, wout_ref, lng_ref, lnb_ref, ws_ref, bs_ref,
                  bias_ref, o_ref, y_ref, yb_ref):
    t = pl.program_id(1)

    def slot(name, rows=slice(None)):
        col = TILE_SLOTS.index(name) * GROUP_W
        return tile_ref[0, rows, col:col + GROUP_W]

    k_col, v_col = (KV_SLOTS.index(name) * GROUP_W for name in ("b_k", "b_v"))

    lh_tile = _lane_head(TM)
    qs = _stack_heads(slot("d_q"), lh_tile)
    s = lax.dot_general(qs, km_ref[0], _NT, preferred_element_type=F32)
    yd = _softmax_pv(s, vm_ref[0], lh_tile, TM)
    y_ref[:, 3 * GROUP_W:4 * GROUP_W] = (yd * _silu(slot("d_g").astype(F32))).astype(BF16)

    lh_blk = {nq: _lane_head(nq) for _, nq, _ in NATTEN_BLOCKS}
    half_keys = WIN_H * KEY_HALF
    for i in range(ROWS_PER_TILE):
        r = t * ROWS_PER_TILE + i
        rs = jnp.clip(r - WIN_H // 2, 0, GRID_H - WIN_H)
        k0 = pl.multiple_of(rs * GRID_W, GRID_W)
        q_row = slot("b_q", slice(i * GRID_W, (i + 1) * GRID_W)).astype(F32)
        qs = jnp.concatenate([jnp.where(lh_blk[nq] == h, q_row[q0:q0 + nq], 0.0)
                              for q0, nq, _ in NATTEN_BLOCKS for h in range(N_SUB)], axis=0).astype(BF16)
        key_rows = [pl.ds(k0 + (k * GRID_W + wb * KEY_HALF), KEY_HALF) for wb in range(2) for k in range(WIN_H)]
        kw = jnp.concatenate([kv_ref[0, rows, k_col:k_col + GROUP_W] for rows in key_rows], axis=0)
        vw = jnp.concatenate([kv_ref[0, rows, v_col:v_col + GROUP_W] for rows in key_rows], axis=0)
        s = lax.dot_general(qs, kw, _NT, preferred_element_type=F32)
        d0 = (WIN_H - 1) - (r - rs)
        probs, invs, row0 = [], [], 0
        for q0, nq, halves in NATTEN_BLOCKS:
            nr = N_SUB * nq
            c0, c1 = halves[0] * half_keys, (halves[-1] + 1) * half_keys
            bias = jnp.concatenate(
                [jnp.concatenate([bias_ref[d0 + 4 * m, h * GRID_W + q0:h * GRID_W + q0 + nq,
                                           wb * LANES:(wb + 1) * LANES] for h in range(N_SUB)], axis=0)
                 for wb in halves for m in range(2)], axis=-1)
            sb = s[row0:row0 + nr, c0:c1] + bias
            e = jnp.exp2(sb - jnp.max(sb, axis=-1, keepdims=True))
            invs.append(1.0 / jnp.sum(e, axis=-1, keepdims=True))
            pieces = ([jnp.zeros((nr, c0), BF16)] if c0 else []) + [e.astype(BF16)]
            pieces += [jnp.zeros((nr, 2 * half_keys - c1), BF16)] if c1 < 2 * half_keys else []
            probs.append(jnp.concatenate(pieces, axis=-1))
            row0 += nr
        o = jnp.dot(jnp.concatenate(probs, axis=0), vw, preferred_element_type=F32)
        outs, row0 = [], 0
        for (q0, nq, _), inv in zip(NATTEN_BLOCKS, invs):
            nr = N_SUB * nq
            outs.append(_pick_heads(o[row0:row0 + nr], lh_blk[nq], nq) * _pick_heads(inv, lh_blk[nq], nq))
            row0 += nr
        yb_ref[i * GRID_W:(i + 1) * GRID_W, :] = jnp.concatenate(outs, axis=0)
    y_ref[:, GROUP_W:2 * GROUP_W] = (yb_ref[...] * _silu(slot("b_g").astype(F32))).astype(BF16)

    lh_chunk = _lane_head(CHUNK)
    for c in range(TM // CHUNK):
        rows = slice(c * CHUNK, (c + 1) * CHUNK)
        v = slot("a_v", rows).astype(F32)
        mu = jnp.mean(v, axis=-1, keepdims=True)
        var = jnp.mean(jnp.square(v - mu), axis=-1, keepdims=True)
        vn = (v - mu) * lax.rsqrt(var + EPS) * lng_ref[...] + lnb_ref[...]
        s = jnp.dot(ws_ref[...], _stack_heads(vn, lh_chunk), preferred_element_type=F32) + bs_ref[...]
        ya = slot("a_u", rows).astype(F32) * s * _silu(slot("a_g", rows).astype(F32))
        y_ref[rows, 0:GROUP_W] = ya.astype(BF16)

    zc_halves = [zc_ref.at[0, hf].reshape(GRID_W * ROWS_PER_TILE, LANES) for hf in range(GROUP_W // LANES)]
    zc = jnp.concatenate(
        [jnp.concatenate([zh[pl.ds(i, GRID_W, stride=ROWS_PER_TILE), :] for zh in zc_halves], axis=-1)
         for i in range(ROWS_PER_TILE)], axis=0)
    y_ref[:, 2 * GROUP_W:3 * GROUP_W] = (zc * _silu(slot("c_g").astype(F32))).astype(BF16)

    o_ref[0] = x_ref[0] + jnp.dot(y_ref[...], wout_ref[...], preferred_element_type=F32)


def _const_spec(shape):
    nd = len(shape)
    return pl.BlockSpec(shape, lambda *_: (0,) * nd)


def _layer_spec(shape, layer):
    nd = len(shape)
    return pl.BlockSpec((None,) + tuple(shape), lambda *_: (layer,) + (0,) * nd, pipeline_mode=pl.Buffered(1))


def _params(dims):
    return pltpu.CompilerParams(dimension_semantics=("arbitrary",) * dims, vmem_limit_bytes=VMEM_LIMIT)


def _mem_kv(mem, g, w_kv, kg, ones_bd):
    return pl.pallas_call(
        _mem_kv_kernel,
        grid=(DEPTH, BATCH // MEM_KV_BATCH),
        in_specs=[
            pl.BlockSpec((MEM_KV_BATCH, N_MEM, D_MODEL), lambda l, b: (b, 0, 0)),
            pl.BlockSpec((None, 1, D_MODEL), lambda l, b: (l, 0, 0)),
            pl.BlockSpec((None, D_MODEL, 2 * GROUP_W), lambda l, b: (l, 0, 0)),
            pl.BlockSpec((None, 1, GROUP_W), lambda l, b: (l, 0, 0)),
            _const_spec((GROUP_W, GROUP_W)),
        ],
        out_specs=[pl.BlockSpec((None, MEM_KV_BATCH, N_MEM, GROUP_W), lambda l, b: (l, b, 0, 0))] * 2,
        out_shape=[jax.ShapeDtypeStruct((DEPTH, BATCH, N_MEM, GROUP_W), BF16)] * 2,
        compiler_params=_params(2),
        name="mem_kv",
    )(mem, g, w_kv, kg, ones_bd)


def _in_proj(layer, x2d, g, w_in, head_gains, ones_bd):
    n_tok = x2d.shape[0]
    return pl.pallas_call(
        _in_proj_kernel,
        grid=(n_tok // TM_IN,),
        in_specs=[
            pl.BlockSpec((TM_IN, D_MODEL), lambda i: (i, 0)),
            _layer_spec((1, D_MODEL), layer),
            _layer_spec((D_MODEL, IN_COLS), layer),
            _layer_spec((8, GROUP_W), layer),
            _const_spec((GROUP_W, GROUP_W)),
        ],
        out_specs=[
            pl.BlockSpec((TM_IN, TILE_COLS), lambda i: (i, 0)),
            pl.BlockSpec((TM_IN, KV_COLS), lambda i: (i, 0)),
            pl.BlockSpec((TM_IN, GROUP_W), lambda i: (i, 0)),
        ],
        out_shape=[
            jax.ShapeDtypeStruct((n_tok, TILE_COLS), BF16),
            jax.ShapeDtypeStruct((n_tok, KV_COLS), BF16),
            jax.ShapeDtypeStruct((n_tok, GROUP_W), F32),
        ],
        compiler_params=_params(1),
        name="in_proj",
    )(x2d, g, w_in, head_gains, ones_bd)


def _fnet(layer, c_in, g1, m2, m3, wf_bd):
    x4 = c_in.reshape(BATCH, GRID_W, GRID_W, GROUP_W)
    n_half = GROUP_W // LANES
    return pl.pallas_call(
        _fnet_kernel,
        grid=(BATCH,),
        in_specs=[pl.BlockSpec((1, GRID_W, GRID_W, GROUP_W), lambda b: (b, 0, 0, 0)),
                  _const_spec(g1.shape), _const_spec(m2.shape), _const_spec(m3.shape),
                  _layer_spec((GROUP_W, GROUP_W), layer)],
        out_specs=pl.BlockSpec((1, n_half, GRID_W, GRID_W, LANES), lambda b: (b, 0, 0, 0, 0)),
        out_shape=jax.ShapeDtypeStruct((BATCH, n_half, GRID_W, GRID_W, LANES), F32),
        scratch_shapes=[pltpu.VMEM((2, GRID_W, GRID_W, GROUP_W), F32)],
        compiler_params=_params(1),
        name="fnet",
    )(x4, g1, m2, m3, wf_bd)


def _mixer(layer, x, tiles, kv, zc, km, vm, w_out, ln_g, ln_b, ws_cat, bs_tab, bias_tab):
    mem_spec = pl.BlockSpec((None, 1, N_MEM, GROUP_W), lambda b, t: (layer, b, 0, 0))
    in_specs = [
        pl.BlockSpec((1, TM, D_MODEL), lambda b, t: (b, t, 0)),
        pl.BlockSpec((1, TM, TILE_COLS), lambda b, t: (b, t, 0)),
        pl.BlockSpec((1, SEQ, KV_COLS), lambda b, t: (b, 0, 0)),
        pl.BlockSpec((1, GROUP_W // LANES, GRID_W, ROWS_PER_TILE, LANES), lambda b, t: (b, 0, 0, t, 0)),
        mem_spec, mem_spec,
        _layer_spec((D_MODEL, D_MODEL), layer),
        _layer_spec((1, GROUP_W), layer), _layer_spec((1, GROUP_W), layer),
        _layer_spec((CHUNK, N_SUB * CHUNK), layer),
        _layer_spec((CHUNK, GROUP_W), layer),
        pl.BlockSpec((2 * WIN_H - 4, N_SUB * GRID_W, 2 * LANES), lambda b, t: (0, layer, 0)),
    ]
    return pl.pallas_call(
        _mixer_kernel,
        grid=(BATCH, SEQ // TM),
        in_specs=in_specs,
        out_specs=pl.BlockSpec((1, TM, D_MODEL), lambda b, t: (b, t, 0)),
        out_shape=jax.ShapeDtypeStruct((BATCH, SEQ, D_MODEL), F32),
        scratch_shapes=[pltpu.VMEM((TM, D_MODEL), BF16), pltpu.VMEM((TM, GROUP_W), F32)],
        compiler_params=_params(2),
        name="mixer",
    )(x, tiles, kv, zc, km, vm, w_out, ln_g, ln_b, ws_cat, bs_tab, bias_tab)


def _tile_heads(g):
    return jnp.tile(g.astype(F32), (1, N_SUB))


def kernel(x, mem, norm_g, w_in, w_out, gm_ln_g, gm_ln_b, gm_w_s, gm_b_s, na_qn_g, na_kn_g, na_rpb, fn_w,
           mem_norm_g, mem_w_kv, mem_qn_g, mem_kn_g):
    g1_np, m2_np, m3_np = _dft_tables()
    g1 = jnp.asarray(g1_np).astype(BF16)
    m2 = jnp.asarray(m2_np).astype(BF16)
    m3 = jnp.asarray(m3_np).astype(BF16)
    ones_bd = jnp.asarray(_head_mean_matrix()).astype(BF16)
    qk_scale = HEAD_DIM ** -0.5 * LOG2E

    w_in_b = w_in.astype(BF16)
    w_out_b = w_out.astype(BF16)
    norm_g3 = norm_g.reshape(DEPTH, 1, D_MODEL)
    head_gains = jnp.stack([_tile_heads(na_qn_g) * qk_scale, _tile_heads(na_kn_g),
                            _tile_heads(mem_qn_g) * qk_scale], axis=1)
    head_gains = jnp.pad(head_gains, ((0, 0), (0, 5), (0, 0)))
    eye = jnp.eye(N_SUB, dtype=fn_w.dtype)
    wf_bd = (fn_w[:, :, :, None, :] * eye[None, :, None, :, None]).reshape(DEPTH, GROUP_W, GROUP_W).astype(BF16)
    ws_cat = gm_w_s.transpose(0, 2, 1, 3).reshape(DEPTH, CHUNK, N_SUB * CHUNK).astype(BF16)
    bs_tab = jnp.repeat(gm_b_s.transpose(0, 2, 1), HEAD_DIM, axis=2)
    ln_g3 = gm_ln_g.reshape(DEPTH, 1, GROUP_W)
    ln_b3 = gm_ln_b.reshape(DEPTH, 1, GROUP_W)
    bias_tab = _natten_bias_table(na_rpb * LOG2E)

    km, vm = _mem_kv(mem, mem_norm_g.reshape(DEPTH, 1, D_MODEL), mem_w_kv.astype(BF16),
                     _tile_heads(mem_kn_g).reshape(DEPTH, 1, GROUP_W), ones_bd)
    for l in range(DEPTH):
        tiles, kv, c_in = _in_proj(l, x.reshape(BATCH * SEQ, D_MODEL), norm_g3, w_in_b, head_gains, ones_bd)
        zc = _fnet(l, c_in, g1, m2, m3, wf_bd)
        x = _mixer(l, x, tiles.reshape(BATCH, SEQ, TILE_COLS), kv.reshape(BATCH, SEQ, KV_COLS), zc, km, vm,
                   w_out_b, ln_g3, ln_b3, ws_cat, bs_tab, bias_tab)
    return x
```

```python
import functools

import numpy as np
import jax
import jax.numpy as jnp
from jax import lax
from jax.experimental import pallas as pl
from jax.experimental.pallas import tpu as pltpu

D_MODEL = 1024
BATCH = 8
SEQ = 4096
DEPTH = 2
N_MEM = 256
GROUP_W = 256
N_SUB = 4
HEAD_DIM = 64
CHUNK = 128
GRID_W = 64
GRID_H = SEQ // GRID_W
WIN_H = 8
WIN_W = 16
N_IN_SLOTS = 11
IN_COLS = N_IN_SLOTS * GROUP_W
EPS = 1e-6
NEG = -1e30
LOG2E = 1.4426950408889634

IN_SLOTS = ("a_u", "a_v", "a_g", "b_q", "b_k", "b_v", "b_g", "c_in", "c_g", "d_q", "d_g")
TILE_SLOTS = ("a_u", "a_v", "a_g", "b_q", "b_g", "c_g", "d_q", "d_g")
KV_SLOTS = ("b_k", "b_v")
TILE_COLS = len(TILE_SLOTS) * GROUP_W
KV_COLS = len(KV_SLOTS) * GROUP_W

KEY_HALF = GRID_W // 2
NATTEN_BLOCKS = ((0, 24, (0,)), (24, 16, (0, 1)), (40, 24, (1,)))
MEM_KV_BATCH = 8
TM = 1024
ROWS_PER_TILE = TM // GRID_W
TM_IN = 1024
FFT_G = 8

LANES = 128
F32 = jnp.float32
BF16 = jnp.bfloat16
VMEM_LIMIT = 56 * 1024 * 1024


@functools.lru_cache(maxsize=None)
def _dft_tables():
    g = np.arange(GRID_W // FFT_G)
    j = np.arange(FFT_G)
    k = np.arange(GRID_W)
    n = np.arange(GRID_W)
    s = GRID_W * n[None, None, None, :] + FFT_G * g[:, None, None, None] + j[None, :, None, None]
    phase = (k[None, None, :, None] * s) % SEQ
    ang = 2.0 * np.pi * phase.astype(np.float64) / SEQ
    g1 = np.zeros((len(g), 2, GRID_W, FFT_G, GRID_W, FFT_G), np.float64)
    for jj in range(FFT_G):
        g1[:, 0, :, jj, :, jj] = np.cos(ang[:, jj])
        g1[:, 1, :, jj, :, jj] = np.sin(ang[:, jj])
    g1 = g1.reshape(len(g), 2 * GRID_W * FFT_G, GRID_W * FFT_G)
    ang2 = 2.0 * np.pi * ((k[:, None] * n[None, :]) % GRID_W).astype(np.float64) / GRID_W
    c2, s2 = np.cos(ang2), np.sin(ang2)
    m2 = np.block([[c2, -s2], [s2, c2]])
    d = np.arange(HEAD_DIM)
    ang3 = 2.0 * np.pi * ((d[:, None] * d[None, :]) % HEAD_DIM).astype(np.float64) / HEAD_DIM
    scale = 1.0 / np.sqrt(float(SEQ * HEAD_DIM))
    cd = np.kron(np.eye(N_SUB), np.cos(ang3)) * scale
    sd = np.kron(np.eye(N_SUB), np.sin(ang3)) * scale
    m3 = np.concatenate([cd, -sd], axis=0)
    return g1.astype(np.float32), m2.astype(np.float32), m3.astype(np.float32)


@functools.lru_cache(maxsize=None)
def _head_mean_matrix():
    return np.kron(np.eye(N_SUB), np.full((HEAD_DIM, HEAD_DIM), 1.0 / HEAD_DIM)).astype(np.float32)


def _natten_bias_table(rpb):
    n_lh = rpb.shape[0] * rpb.shape[1]
    rpb = rpb.reshape(n_lh, 2 * WIN_H - 1, 2 * WIN_W - 1).astype(F32)
    c = np.arange(GRID_W)
    col_start = np.clip(c - WIN_W // 2, 0, GRID_W - WIN_W)
    col_ok = (c[None, :] >= col_start[:, None]) & (c[None, :] < col_start[:, None] + WIN_W)
    for q0, nq, halves in NATTEN_BLOCKS:
        for wb in range(2):
            assert wb in halves or not col_ok[q0:q0 + nq, wb * KEY_HALF:(wb + 1) * KEY_HALF].any()
    d_col = c[None, :] - c[:, None] + (WIN_W - 1)
    onehot = (d_col[None, :, :] == np.arange(2 * WIN_W - 1)[:, None, None]).astype(np.float32)
    toe = jnp.einsum("hij,jcw->ihcw", rpb, onehot, precision=lax.Precision.HIGHEST)
    toe = jnp.where(col_ok[None, None, :, :], toe, NEG)
    toe = toe.reshape(2 * WIN_H - 1, n_lh * GRID_W, GRID_W)
    n_i, n_d = 2 * WIN_H - 1 - 3, 4
    toe4 = jnp.concatenate([toe[d:d + n_i] for d in range(n_d)], axis=-1)
    src = np.arange(n_d * GRID_W).reshape(n_d, 2, KEY_HALF).transpose(1, 0, 2).reshape(-1)
    perm = (np.arange(n_d * GRID_W)[:, None] == src[None, :]).astype(np.float32)
    return jnp.einsum("irk,kn->irn", toe4, perm, precision=lax.Precision.HIGHEST)


def _head_rmsnorm(y, gain, ones_bd):
    ms = jnp.dot((y * y).astype(BF16), ones_bd, preferred_element_type=F32)
    return y * lax.rsqrt(ms + EPS) * gain


def _silu(g):
    h = 0.5 * g
    return h + h * jnp.tanh(h)


def _lane_head(rows):
    return lax.broadcasted_iota(jnp.int32, (rows, GROUP_W), 1) // HEAD_DIM


def _stack_heads(q, lane_head):
    qf = q.astype(F32)
    return jnp.concatenate(
        [jnp.where(lane_head == h, qf, 0.0) for h in range(N_SUB)], axis=0).astype(BF16)


def _pick_heads(o, lane_head, rows):
    out = o[0:rows]
    for h in range(1, N_SUB):
        out = jnp.where(lane_head == h, o[h * rows:(h + 1) * rows], out)
    return out


def _softmax_pv(s, v, lane_head, rows):
    m = jnp.max(s, axis=-1, keepdims=True)
    e = jnp.exp2(s - m)
    inv = 1.0 / jnp.sum(e, axis=-1, keepdims=True)
    o = jnp.dot(e.astype(BF16), v, preferred_element_type=F32)
    return _pick_heads(o, lane_head, rows) * _pick_heads(inv, lane_head, rows)


_NT = (((1,), (1,)), ((), ()))


def _mem_kv_kernel(mem_ref, g_ref, w_ref, kg_ref, ones_ref, k_ref, v_ref):
    m = mem_ref[...].reshape(MEM_KV_BATCH * N_MEM, D_MODEL)
    ms = jnp.mean(m * m, axis=-1, keepdims=True)
    mn = (m * lax.rsqrt(ms + EPS) * g_ref[...]).astype(BF16)
    kv = jnp.dot(mn, w_ref[...], preferred_element_type=F32)
    k = _head_rmsnorm(kv[:, :GROUP_W], kg_ref[...], ones_ref[...])
    k_ref[...] = k.astype(BF16).reshape(MEM_KV_BATCH, N_MEM, GROUP_W)
    v_ref[...] = kv[:, GROUP_W:].astype(BF16).reshape(MEM_KV_BATCH, N_MEM, GROUP_W)


def _in_proj_kernel(x_ref, g_ref, w_ref, hg_ref, ones_ref, tile_ref, kv_ref, cin_ref):
    head_gain_row = {"b_q": 0, "b_k": 1, "d_q": 2}
    x = x_ref[...]
    inv = lax.rsqrt(jnp.mean(x * x, axis=-1, keepdims=True) + EPS)
    h = (x * g_ref[...]).astype(BF16)
    y_all = jnp.dot(h, w_ref[...], preferred_element_type=F32)
    for j, name in enumerate(IN_SLOTS):
        y = y_all[:, j * GROUP_W:(j + 1) * GROUP_W] * inv
        if name in head_gain_row:
            r = head_gain_row[name]
            y = _head_rmsnorm(y, hg_ref[r:r + 1, :], ones_ref[...])
        if name == "c_in":
            cin_ref[...] = y
        elif name in KV_SLOTS:
            col = KV_SLOTS.index(name) * GROUP_W
            kv_ref[:, col:col + GROUP_W] = y.astype(BF16)
        else:
            col = TILE_SLOTS.index(name) * GROUP_W
            tile_ref[:, col:col + GROUP_W] = y.astype(BF16)


def _fnet_kernel(x_ref, g1_ref, m2_ref, m3_ref, wf_ref, z_ref, t_ref):
    half = GRID_W * FFT_G

    def stage1(g, carry):
        n0 = pl.multiple_of(g * FFT_G, FFT_G)
        x = x_ref[0, :, pl.ds(n0, FFT_G), :].reshape(half, GROUP_W).astype(BF16)
        y = jnp.dot(g1_ref[g], x, preferred_element_type=F32)
        t_ref[:, :, pl.ds(n0, FFT_G), :] = y.reshape(2, GRID_W, FFT_G, GROUP_W)
        return carry

    m3w = jnp.dot(m3_ref[...], wf_ref[...], preferred_element_type=F32).astype(BF16)

    def stage2(g, carry):
        k0 = g * FFT_G
        t = jnp.concatenate(
            [jnp.concatenate([t_ref[0, k0 + j], t_ref[1, k0 + j]], axis=0) for j in range(FFT_G)],
            axis=-1).astype(BF16)
        ab = jnp.dot(m2_ref[...], t, preferred_element_type=F32)
        ab = jnp.concatenate(
            [jnp.concatenate([ab[:GRID_W, j * GROUP_W:(j + 1) * GROUP_W],
                              ab[GRID_W:, j * GROUP_W:(j + 1) * GROUP_W]], axis=-1) for j in range(FFT_G)],
            axis=0).astype(BF16)
        zc = jnp.dot(ab, m3w, preferred_element_type=F32)
        zc = zc.reshape(FFT_G, GRID_W, GROUP_W)
        for hf in range(GROUP_W // LANES):
            z_ref[0, hf, pl.ds(k0, FFT_G), :, :] = zc[:, :, hf * LANES:(hf + 1) * LANES]
        return carry

    n_groups = GRID_W // FFT_G
    lax.fori_loop(0, n_groups, stage1, 0, unroll=True)
    lax.fori_loop(0, n_groups, stage2, 0, unroll=True)


def _mixer_kernel(x_ref, tile_ref, kv_ref, zc_ref, km_ref, vm_ref, wout_ref, lng_ref, lnb_ref, ws_ref, bs_ref,
                  bias_ref, o_ref, y_ref, yb_ref):
    t = pl.program_id(1)

    def slot(name, rows=slice(None)):
        col = TILE_SLOTS.index(name) * GROUP_W
        return tile_ref[0, rows, col:col + GROUP_W]

    k_col, v_col = (KV_SLOTS.index(name) * GROUP_W for name in ("b_k", "b_v"))

    lh_tile = _lane_head(TM)
    qs = _stack_heads(slot("d_q"), lh_tile)
    s = lax.dot_general(qs, km_ref[0], _NT, preferred_element_type=F32)
    yd = _softmax_pv(s, vm_ref[0], lh_tile, TM)
    y_ref[:, 3 * GROUP_W:4 * GROUP_W] = (yd * _silu(slot("d_g").astype(F32))).astype(BF16)

    lh_blk = {nq: _lane_head(nq) for _, nq, _ in NATTEN_BLOCKS}
    half_keys = WIN_H * KEY_HALF
    for i in range(ROWS_PER_TILE):
        r = t * ROWS_PER_TILE + i
        rs = jnp.clip(r - WIN_H // 2, 0, GRID_H - WIN_H)
        k0 = pl.multiple_of(rs * GRID_W, GRID_W)
        q_row = slot("b_q", slice(i * GRID_W, (i + 1) * GRID_W)).astype(F32)
        qs = jnp.concatenate([jnp.where(lh_blk[nq] == h, q_row[q0:q0 + nq], 0.0)
                              for q0, nq, _ in NATTEN_BLOCKS for h in range(N_SUB)], axis=0).astype(BF16)
        key_rows = [pl.ds(k0 + (k * GRID_W + wb * KEY_HALF), KEY_HALF) for wb in range(2) for k in range(WIN_H)]
        kw = jnp.concatenate([kv_ref[0, rows, k_col:k_col + GROUP_W] for rows in key_rows], axis=0)
        vw = jnp.concatenate([kv_ref[0, rows, v_col:v_col + GROUP_W] for rows in key_rows], axis=0)
        s = lax.dot_general(qs, kw, _NT, preferred_element_type=F32)
        d0 = (WIN_H - 1) - (r - rs)
        probs, invs, row0 = [], [], 0
        for q0, nq, halves in NATTEN_BLOCKS:
            nr = N_SUB * nq
            c0, c1 = halves[0] * half_keys, (halves[-1] + 1) * half_keys
            bias = jnp.concatenate(
                [jnp.concatenate([bias_ref[d0 + 4 * m, h * GRID_W + q0:h * GRID_W + q0 + nq,
                                           wb * LANES:(wb + 1) * LANES] for h in range(N_SUB)], axis=0)
                 for wb in halves for m in range(2)], axis=-1)
            sb = s[row0:row0 + nr, c0:c1] + bias
            e = jnp.exp2(sb - jnp.max(sb, axis=-1, keepdims=True))
            invs.append(1.0 / jnp.sum(e, axis=-1, keepdims=True))
            pieces = ([jnp.zeros((nr, c0), BF16)] if c0 else []) + [e.astype(BF16)]
            pieces += [jnp.zeros((nr, 2 * half_keys - c1), BF16)] if c1 < 2 * half_keys else []
            probs.append(jnp.concatenate(pieces, axis=-1))
            row0 += nr
        o = jnp.dot(jnp.concatenate(probs, axis=0), vw, preferred_element_type=F32)
        outs, row0 = [], 0
        for (q0, nq, _), inv in zip(NATTEN_BLOCKS, invs):
            nr = N_SUB * nq
            outs.append(_pick_heads(o[row0:row0 + nr], lh_blk[nq], nq) * _pick_heads(inv, lh_blk[nq], nq))
            row0 += nr
        yb_ref[i * GRID_W:(i + 1) * GRID_W, :] = jnp.concatenate(outs, axis=0)
    y_ref[:, GROUP_W:2 * GROUP_W] = (yb_ref[...] * _silu(slot("b_g").astype(F32))).astype(BF16)

    lh_chunk = _lane_head(CHUNK)
    for c in range(TM // CHUNK):
        rows = slice(c * CHUNK, (c + 1) * CHUNK)
        v = slot("a_v", rows).astype(F32)
        mu = jnp.mean(v, axis=-1, keepdims=True)
        var = jnp.mean(jnp.square(v - mu), axis=-1, keepdims=True)
        vn = (v - mu) * lax.rsqrt(var + EPS) * lng_ref[...] + lnb_ref[...]
        s = jnp.dot(ws_ref[...], _stack_heads(vn, lh_chunk), preferred_element_type=F32) + bs_ref[...]
        ya = slot("a_u", rows).astype(F32) * s * _silu(slot("a_g", rows).astype(F32))
        y_ref[rows, 0:GROUP_W] = ya.astype(BF16)

    zc_halves = [zc_ref.at[0, hf].reshape(GRID_W * ROWS_PER_TILE, LANES) for hf in range(GROUP_W // LANES)]
    zc = jnp.concatenate(
        [jnp.concatenate([zh[pl.ds(i, GRID_W, stride=ROWS_PER_TILE), :] for zh in zc_halves], axis=-1)
         for i in range(ROWS_PER_TILE)], axis=0)
    y_ref[:, 2 * GROUP_W:3 * GROUP_W] = (zc * _silu(slot("c_g").astype(F32))).astype(BF16)

    o_ref[0] = x_ref[0] + jnp.dot(y_ref[...], wout_ref[...], preferred_element_type=F32)


def _const_spec(shape):
    nd = len(shape)
    return pl.BlockSpec(shape, lambda *_: (0,) * nd, pipeline_mode=pl.Buffered(1))


def _layer_spec(shape, layer):
    nd = len(shape)
    return pl.BlockSpec((None,) + tuple(shape), lambda *_: (layer,) + (0,) * nd, pipeline_mode=pl.Buffered(1))


def _params(dims):
    return pltpu.CompilerParams(dimension_semantics=("arbitrary",) * dims, vmem_limit_bytes=VMEM_LIMIT)


def _mem_kv(mem, g, w_kv, kg, ones_bd):
    return pl.pallas_call(
        _mem_kv_kernel,
        grid=(DEPTH, BATCH // MEM_KV_BATCH),
        in_specs=[
            pl.BlockSpec((MEM_KV_BATCH, N_MEM, D_MODEL), lambda l, b: (b, 0, 0)),
            pl.BlockSpec((None, 1, D_MODEL), lambda l, b: (l, 0, 0)),
            pl.BlockSpec((None, D_MODEL, 2 * GROUP_W), lambda l, b: (l, 0, 0)),
            pl.BlockSpec((None, 1, GROUP_W), lambda l, b: (l, 0, 0)),
            _const_spec((GROUP_W, GROUP_W)),
        ],
        out_specs=[pl.BlockSpec((None, MEM_KV_BATCH, N_MEM, GROUP_W), lambda l, b: (l, b, 0, 0))] * 2,
        out_shape=[jax.ShapeDtypeStruct((DEPTH, BATCH, N_MEM, GROUP_W), BF16)] * 2,
        compiler_params=_params(2),
        name="mem_kv",
    )(mem, g, w_kv, kg, ones_bd)


def _in_proj(layer, x2d, g, w_in, head_gains, ones_bd):
    n_tok = x2d.shape[0]
    return pl.pallas_call(
        _in_proj_kernel,
        grid=(n_tok // TM_IN,),
        in_specs=[
            pl.BlockSpec((TM_IN, D_MODEL), lambda i: (i, 0)),
            _layer_spec((1, D_MODEL), layer),
            _layer_spec((D_MODEL, IN_COLS), layer),
            _layer_spec((8, GROUP_W), layer),
            _const_spec((GROUP_W, GROUP_W)),
        ],
        out_specs=[
            pl.BlockSpec((TM_IN, TILE_COLS), lambda i: (i, 0)),
            pl.BlockSpec((TM_IN, KV_COLS), lambda i: (i, 0)),
            pl.BlockSpec((TM_IN, GROUP_W), lambda i: (i, 0)),
        ],
        out_shape=[
            jax.ShapeDtypeStruct((n_tok, TILE_COLS), BF16),
            jax.ShapeDtypeStruct((n_tok, KV_COLS), BF16),
            jax.ShapeDtypeStruct((n_tok, GROUP_W), F32),
        ],
        compiler_params=_params(1),
        name="in_proj",
    )(x2d, g, w_in, head_gains, ones_bd)


def _fnet(layer, c_in, g1, m2, m3, wf_bd):
    x4 = c_in.reshape(BATCH, GRID_W, GRID_W, GROUP_W)
    n_half = GROUP_W // LANES
    return pl.pallas_call(
        _fnet_kernel,
        grid=(BATCH,),
        in_specs=[pl.BlockSpec((1, GRID_W, GRID_W, GROUP_W), lambda b: (b, 0, 0, 0)),
                  _const_spec(g1.shape), _const_spec(m2.shape), _const_spec(m3.shape),
                  _layer_spec((GROUP_W, GROUP_W), layer)],
        out_specs=pl.BlockSpec((1, n_half, GRID_W, GRID_W, LANES), lambda b: (b, 0, 0, 0, 0)),
        out_shape=jax.ShapeDtypeStruct((BATCH, n_half, GRID_W, GRID_W, LANES), F32),
        scratch_shapes=[pltpu.VMEM((2, GRID_W, GRID_W, GROUP_W), F32)],
        compiler_params=_params(1),
        name="fnet",
    )(x4, g1, m2, m3, wf_bd)


def _mixer(layer, x, tiles, kv, zc, km, vm, w_out, ln_g, ln_b, ws_cat, bs_tab, bias_tab):
    mem_spec = pl.BlockSpec((None, 1, N_MEM, GROUP_W), lambda b, t: (layer, b, 0, 0))
    in_specs = [
        pl.BlockSpec((1, TM, D_MODEL), lambda b, t: (b, t, 0)),
        pl.BlockSpec((1, TM, TILE_COLS), lambda b, t: (b, t, 0)),
        pl.BlockSpec((1, SEQ, KV_COLS), lambda b, t: (b, 0, 0)),
        pl.BlockSpec((1, GROUP_W // LANES, GRID_W, ROWS_PER_TILE, LANES), lambda b, t: (b, 0, 0, t, 0)),
        mem_spec, mem_spec,
        _layer_spec((D_MODEL, D_MODEL), layer),
        _layer_spec((1, GROUP_W), layer), _layer_spec((1, GROUP_W), layer),
        _layer_spec((CHUNK, N_SUB * CHUNK), layer),
        _layer_spec((CHUNK, GROUP_W), layer),
        pl.BlockSpec((2 * WIN_H - 4, N_SUB * GRID_W, 2 * LANES), lambda b, t: (0, layer, 0)),
    ]
    return pl.pallas_call(
        _mixer_kernel,
        grid=(BATCH, SEQ // TM),
        in_specs=in_specs,
        out_specs=pl.BlockSpec((1, TM, D_MODEL), lambda b, t: (b, t, 0)),
        out_shape=jax.ShapeDtypeStruct((BATCH, SEQ, D_MODEL), F32),
        scratch_shapes=[pltpu.VMEM((TM, D_MODEL), BF16), pltpu.VMEM((TM, GROUP_W), F32)],
        compiler_params=_params(2),
        name="mixer",
    )(x, tiles, kv, zc, km, vm, w_out, ln_g, ln_b, ws_cat, bs_tab, bias_tab)


def _tile_heads(g):
    return jnp.tile(g.astype(F32), (1, N_SUB))


def kernel(x, mem, norm_g, w_in, w_out, gm_ln_g, gm_ln_b, gm_w_s, gm_b_s, na_qn_g, na_kn_g, na_rpb, fn_w,
           mem_norm_g, mem_w_kv, mem_qn_g, mem_kn_g):
    g1_np, m2_np, m3_np = _dft_tables()
    g1 = jnp.asarray(g1_np).astype(BF16)
    m2 = jnp.asarray(m2_np).astype(BF16)
    m3 = jnp.asarray(m3_np).astype(BF16)
    ones_bd = jnp.asarray(_head_mean_matrix()).astype(BF16)
    qk_scale = HEAD_DIM ** -0.5 * LOG2E

    w_in_b = w_in.astype(BF16)
    w_out_b = w_out.astype(BF16)
    norm_g3 = norm_g.reshape(DEPTH, 1, D_MODEL)
    head_gains = jnp.stack([_tile_heads(na_qn_g) * qk_scale, _tile_heads(na_kn_g),
                            _tile_heads(mem_qn_g) * qk_scale], axis=1)
    head_gains = jnp.pad(head_gains, ((0, 0), (0, 5), (0, 0)))
    eye = jnp.eye(N_SUB, dtype=fn_w.dtype)
    wf_bd = (fn_w[:, :, :, None, :] * eye[None, :, None, :, None]).reshape(DEPTH, GROUP_W, GROUP_W).astype(BF16)
    ws_cat = gm_w_s.transpose(0, 2, 1, 3).reshape(DEPTH, CHUNK, N_SUB * CHUNK).astype(BF16)
    bs_tab = jnp.repeat(gm_b_s.transpose(0, 2, 1), HEAD_DIM, axis=2)
    ln_g3 = gm_ln_g.reshape(DEPTH, 1, GROUP_W)
    ln_b3 = gm_ln_b.reshape(DEPTH, 1, GROUP_W)
    bias_tab = _natten_bias_table(na_rpb * LOG2E)

    km, vm = _mem_kv(mem, mem_norm_g.reshape(DEPTH, 1, D_MODEL), mem_w_kv.astype(BF16),
                     _tile_heads(mem_kn_g).reshape(DEPTH, 1, GROUP_W), ones_bd)
    for l in range(DEPTH):
        tiles, kv, c_in = _in_proj(l, x.reshape(BATCH * SEQ, D_MODEL), norm_g3, w_in_b, head_gains, ones_bd)
        zc = _fnet(l, c_in, g1, m2, m3, wf_bd)
        x = _mixer(l, x, tiles.reshape(BATCH, SEQ, TILE_COLS), kv.reshape(BATCH, SEQ, KV_COLS), zc, km, vm,
                   w_out_b, ln_g3, ln_b3, ws_cat, bs_tab, bias_tab)
    return x
```

```python
import functools

import numpy as np
import jax
import jax.numpy as jnp
from jax import lax
from jax.experimental import pallas as pl
from jax.experimental.pallas import tpu as pltpu

D_MODEL = 1024
BATCH = 8
SEQ = 4096
DEPTH = 2
N_MEM = 256
GROUP_W = 256
N_SUB = 4
HEAD_DIM = 64
CHUNK = 128
GRID_W = 64
GRID_H = SEQ // GRID_W
WIN_H = 8
WIN_W = 16
N_IN_SLOTS = 11
IN_COLS = N_IN_SLOTS * GROUP_W
EPS = 1e-6
NEG = -1e30
LOG2E = 1.4426950408889634

IN_SLOTS = ("a_u", "a_v", "a_g", "b_q", "b_k", "b_v", "b_g", "c_in", "c_g", "d_q", "d_g")
TILE_SLOTS = ("a_u", "a_v", "a_g", "b_q", "b_g", "c_g", "d_q", "d_g")
KV_SLOTS = ("b_k", "b_v")
TILE_COLS = len(TILE_SLOTS) * GROUP_W
KV_COLS = len(KV_SLOTS) * GROUP_W

KEY_HALF = GRID_W // 2
NATTEN_BLOCKS = ((0, 24, (0,)), (24, 16, (0, 1)), (40, 24, (1,)))
MEM_KV_BATCH = 8
TM = 1024
ROWS_PER_TILE = TM // GRID_W
TM_IN = 1024
FFT_G = 8

LANES = 128
F32 = jnp.float32
BF16 = jnp.bfloat16
VMEM_LIMIT = 56 * 1024 * 1024


@functools.lru_cache(maxsize=None)
def _dft_tables():
    g = np.arange(GRID_W // FFT_G)
    j = np.arange(FFT_G)
    k = np.arange(GRID_W)
    n = np.arange(GRID_W)
    s = GRID_W * n[None, None, None, :] + FFT_G * g[:, None, None, None] + j[None, :, None, None]
    phase = (k[None, None, :, None] * s) % SEQ
    ang = 2.0 * np.pi * phase.astype(np.float64) / SEQ
    g1 = np.zeros((len(g), 2, GRID_W, FFT_G, GRID_W, FFT_G), np.float64)
    for jj in range(FFT_G):
        g1[:, 0, :, jj, :, jj] = np.cos(ang[:, jj])
        g1[:, 1, :, jj, :, jj] = np.sin(ang[:, jj])
    g1 = g1.reshape(len(g), 2 * GRID_W * FFT_G, GRID_W * FFT_G)
    ang2 = 2.0 * np.pi * ((k[:, None] * n[None, :]) % GRID_W).astype(np.float64) / GRID_W
    c2, s2 = np.cos(ang2), np.sin(ang2)
    m2 = np.block([[c2, -s2], [s2, c2]])
    d = np.arange(HEAD_DIM)
    ang3 = 2.0 * np.pi * ((d[:, None] * d[None, :]) % HEAD_DIM).astype(np.float64) / HEAD_DIM
    scale = 1.0 / np.sqrt(float(SEQ * HEAD_DIM))
    cd = np.kron(np.eye(N_SUB), np.cos(ang3)) * scale
    sd = np.kron(np.eye(N_SUB), np.sin(ang3)) * scale
    m3 = np.concatenate([cd, -sd], axis=0)
    return g1.astype(np.float32), m2.astype(np.float32), m3.astype(np.float32)


@functools.lru_cache(maxsize=None)
def _head_mean_matrix():
    return np.kron(np.eye(N_SUB), np.full((HEAD_DIM, HEAD_DIM), 1.0 / HEAD_DIM)).astype(np.float32)


def _natten_bias_table(rpb):
    n_lh = rpb.shape[0] * rpb.shape[1]
    rpb = rpb.reshape(n_lh, 2 * WIN_H - 1, 2 * WIN_W - 1).astype(F32)
    c = np.arange(GRID_W)
    col_start = np.clip(c - WIN_W // 2, 0, GRID_W - WIN_W)
    col_ok = (c[None, :] >= col_start[:, None]) & (c[None, :] < col_start[:, None] + WIN_W)
    for q0, nq, halves in NATTEN_BLOCKS:
        for wb in range(2):
            assert wb in halves or not col_ok[q0:q0 + nq, wb * KEY_HALF:(wb + 1) * KEY_HALF].any()
    d_col = c[None, :] - c[:, None] + (WIN_W - 1)
    onehot = (d_col[None, :, :] == np.arange(2 * WIN_W - 1)[:, None, None]).astype(np.float32)
    toe = jnp.einsum("hij,jcw->ihcw", rpb, onehot, precision=lax.Precision.HIGHEST)
    toe = jnp.where(col_ok[None, None, :, :], toe, NEG)
    toe = toe.reshape(2 * WIN_H - 1, n_lh * GRID_W, GRID_W)
    n_i, n_d = 2 * WIN_H - 1 - 3, 4
    toe4 = jnp.concatenate([toe[d:d + n_i] for d in range(n_d)], axis=-1)
    src = np.arange(n_d * GRID_W).reshape(n_d, 2, KEY_HALF).transpose(1, 0, 2).reshape(-1)
    perm = (np.arange(n_d * GRID_W)[:, None] == src[None, :]).astype(np.float32)
    return jnp.einsum("irk,kn->irn", toe4, perm, precision=lax.Precision.HIGHEST)


def _head_rmsnorm(y, gain, ones_bd):
    ms = jnp.dot((y * y).astype(BF16), ones_bd, preferred_element_type=F32)
    return y * lax.rsqrt(ms + EPS) * gain


def _silu(g):
    h = 0.5 * g
    return h + h * jnp.tanh(h)


def _lane_head(rows):
    return lax.broadcasted_iota(jnp.int32, (rows, GROUP_W), 1) // HEAD_DIM


def _stack_heads(q, lane_head):
    qf = q.astype(F32)
    return jnp.concatenate(
        [jnp.where(lane_head == h, qf, 0.0) for h in range(N_SUB)], axis=0).astype(BF16)


def _pick_heads(o, lane_head, rows):
    out = o[0:rows]
    for h in range(1, N_SUB):
        out = jnp.where(lane_head == h, o[h * rows:(h + 1) * rows], out)
    return out


def _softmax_pv(s, v, lane_head, rows):
    m = jnp.max(s, axis=-1, keepdims=True)
    e = jnp.exp2(s - m)
    inv = 1.0 / jnp.sum(e, axis=-1, keepdims=True)
    o = jnp.dot(e.astype(BF16), v, preferred_element_type=F32)
    return _pick_heads(o, lane_head, rows) * _pick_heads(inv, lane_head, rows)


_NT = (((1,), (1,)), ((), ()))


def _mem_kv_kernel(mem_ref, g_ref, w_ref, kg_ref, ones_ref, k_ref, v_ref):
    m = mem_ref[...].reshape(MEM_KV_BATCH * N_MEM, D_MODEL)
    ms = jnp.mean(m * m, axis=-1, keepdims=True)
    mn = (m * lax.rsqrt(ms + EPS) * g_ref[...]).astype(BF16)
    kv = jnp.dot(mn, w_ref[...], preferred_element_type=F32)
    k = _head_rmsnorm(kv[:, :GROUP_W], kg_ref[...], ones_ref[...])
    k_ref[...] = k.astype(BF16).reshape(MEM_KV_BATCH, N_MEM, GROUP_W)
    v_ref[...] = kv[:, GROUP_W:].astype(BF16).reshape(MEM_KV_BATCH, N_MEM, GROUP_W)


def _in_proj_kernel(x_ref, g_ref, w_ref, hg_ref, ones_ref, tile_ref, kv_ref, cin_ref):
    head_gain_row = {"b_q": 0, "b_k": 1, "d_q": 2}
    x = x_ref[...]
    inv = lax.rsqrt(jnp.mean(x * x, axis=-1, keepdims=True) + EPS)
    h = (x * g_ref[...]).astype(BF16)
    y_all = jnp.dot(h, w_ref[...], preferred_element_type=F32)
    for j, name in enumerate(IN_SLOTS):
        y = y_all[:, j * GROUP_W:(j + 1) * GROUP_W] * inv
        if name in head_gain_row:
            r = head_gain_row[name]
            y = _head_rmsnorm(y, hg_ref[r:r + 1, :], ones_ref[...])
        if name == "c_in":
            cin_ref[...] = y
        elif name in KV_SLOTS:
            col = KV_SLOTS.index(name) * GROUP_W
            kv_ref[:, col:col + GROUP_W] = y.astype(BF16)
        else:
            col = TILE_SLOTS.index(name) * GROUP_W
            tile_ref[:, col:col + GROUP_W] = y.astype(BF16)


def _fnet_kernel(x_ref, g1_ref, m2_ref, m3_ref, wf_ref, z_ref, t_ref):
    half = GRID_W * FFT_G

    def stage1(g, carry):
        n0 = pl.multiple_of(g * FFT_G, FFT_G)
        x = x_ref[0, :, pl.ds(n0, FFT_G), :].reshape(half, GROUP_W).astype(BF16)
        y = jnp.dot(g1_ref[g], x, preferred_element_type=F32)
        t_ref[:, :, pl.ds(n0, FFT_G), :] = y.reshape(2, GRID_W, FFT_G, GROUP_W)
        return carry

    m3w = jnp.dot(m3_ref[...], wf_ref[...], preferred_element_type=F32).astype(BF16)

    def stage2(g, carry):
        k0 = g * FFT_G
        t = jnp.concatenate(
            [jnp.concatenate([t_ref[0, k0 + j], t_ref[1, k0 + j]], axis=0) for j in range(FFT_G)],
            axis=-1).astype(BF16)
        ab = jnp.dot(m2_ref[...], t, preferred_element_type=F32)
        ab = jnp.concatenate(
            [jnp.concatenate([ab[:GRID_W, j * GROUP_W:(j + 1) * GROUP_W],
                              ab[GRID_W:, j * GROUP_W:(j + 1) * GROUP_W]], axis=-1) for j in range(FFT_G)],
            axis=0).astype(BF16)
        zc = jnp.dot(ab, m3w, preferred_element_type=F32)
        zc = zc.reshape(FFT_G, GRID_W, GROUP_W)
        for hf in range(GROUP_W // LANES):
            z_ref[0, hf, pl.ds(k0, FFT_G), :, :] = zc[:, :, hf * LANES:(hf + 1) * LANES]
        return carry

    n_groups = GRID_W // FFT_G
    lax.fori_loop(0, n_groups, stage1, 0, unroll=True)
    lax.fori_loop(0, n_groups, stage2, 0, unroll=True)


def _mixer_kernel(x_ref, tile_ref, kv_ref, zc_ref, km_ref, vm_ref, wout_ref, lng_ref, lnb_ref, ws_ref, bs_ref,
                  bias_ref, o_ref, y_ref, yb_ref):
    t = pl.program_id(1)

    def slot(name, rows=slice(None)):
        col = TILE_SLOTS.index(name) * GROUP_W
        return tile_ref[0, rows, col:col + GROUP_W]

    k_col, v_col = (KV_SLOTS.index(name) * GROUP_W for name in ("b_k", "b_v"))

    lh_tile = _lane_head(TM)
    qs = _stack_heads(slot("d_q"), lh_tile)
    s = lax.dot_general(qs, km_ref[0], _NT, preferred_element_type=F32)
    yd = _softmax_pv(s, vm_ref[0], lh_tile, TM)
    y_ref[:, 3 * GROUP_W:4 * GROUP_W] = (yd * _silu(slot("d_g").astype(F32))).astype(BF16)

    lh_blk = {nq: _lane_head(nq) for _, nq, _ in NATTEN_BLOCKS}
    half_keys = WIN_H * KEY_HALF
    for i in range(ROWS_PER_TILE):
        r = t * ROWS_PER_TILE + i
        rs = jnp.clip(r - WIN_H // 2, 0, GRID_H - WIN_H)
        k0 = pl.multiple_of(rs * GRID_W, GRID_W)
        q_row = slot("b_q", slice(i * GRID_W, (i + 1) * GRID_W)).astype(F32)
        qs = jnp.concatenate([jnp.where(lh_blk[nq] == h, q_row[q0:q0 + nq], 0.0)
                              for q0, nq, _ in NATTEN_BLOCKS for h in range(N_SUB)], axis=0).astype(BF16)
        key_rows = [pl.ds(k0 + (k * GRID_W + wb * KEY_HALF), KEY_HALF) for wb in range(2) for k in range(WIN_H)]
        kw = jnp.concatenate([kv_ref[0, rows, k_col:k_col + GROUP_W] for rows in key_rows], axis=0)
        vw = jnp.concatenate([kv_ref[0, rows, v_col:v_col + GROUP_W] for rows in key_rows], axis=0)
        s = lax.dot_general(qs, kw, _NT, preferred_element_type=F32)
        d0 = (WIN_H - 1) - (r - rs)
        probs, invs, row0 = [], [], 0
        for q0, nq, halves in NATTEN_BLOCKS:
            nr = N_SUB * nq
            c0, c1 = halves[0] * half_keys, (halves[-1] + 1) * half_keys
            bias = jnp.concatenate(
                [jnp.concatenate([bias_ref[d0 + 4 * m, h * GRID_W + q0:h * GRID_W + q0 + nq,
                                           wb * LANES:(wb + 1) * LANES] for h in range(N_SUB)], axis=0)
                 for wb in halves for m in range(2)], axis=-1)
            sb = s[row0:row0 + nr, c0:c1] + bias
            e = jnp.exp2(sb - jnp.max(sb, axis=-1, keepdims=True))
            invs.append(1.0 / jnp.sum(e, axis=-1, keepdims=True))
            pieces = ([jnp.zeros((nr, c0), BF16)] if c0 else []) + [e.astype(BF16)]
            pieces += [jnp.zeros((nr, 2 * half_keys - c1), BF16)] if c1 < 2 * half_keys else []
            probs.append(jnp.concatenate(pieces, axis=-1))
            row0 += nr
        o = jnp.dot(jnp.concatenate(probs, axis=0), vw, preferred_element_type=F32)
        outs, row0 = [], 0
        for (q0, nq, _), inv in zip(NATTEN_BLOCKS, invs):
            nr = N_SUB * nq
            outs.append(_pick_heads(o[row0:row0 + nr], lh_blk[nq], nq) * _pick_heads(inv, lh_blk[nq], nq))
            row0 += nr
        yb_ref[i * GRID_W:(i + 1) * GRID_W, :] = jnp.concatenate(outs, axis=0)
    y_ref[:, GROUP_W:2 * GROUP_W] = (yb_ref[...] * _silu(slot("b_g").astype(F32))).astype(BF16)

    lh_chunk = _lane_head(CHUNK)
    for c in range(TM // CHUNK):
        rows = slice(c * CHUNK, (c + 1) * CHUNK)
        v = slot("a_v", rows).astype(F32)
        mu = jnp.mean(v, axis=-1, keepdims=True)
        var = jnp.mean(jnp.square(v - mu), axis=-1, keepdims=True)
        vn = (v - mu) * lax.rsqrt(var + EPS) * lng_ref[...] + lnb_ref[...]
        s = jnp.dot(ws_ref[...], _stack_heads(vn, lh_chunk), preferred_element_type=F32) + bs_ref[...]
        ya = slot("a_u", rows).astype(F32) * s * _silu(slot("a_g", rows).astype(F32))
        y_ref[rows, 0:GROUP_W] = ya.astype(BF16)

    zc_halves = [zc_ref.at[0, hf].reshape(GRID_W * ROWS_PER_TILE, LANES) for hf in range(GROUP_W // LANES)]
    zc = jnp.concatenate(
        [jnp.concatenate([zh[pl.ds(i, GRID_W, stride=ROWS_PER_TILE), :] for zh in zc_halves], axis=-1)
         for i in range(ROWS_PER_TILE)], axis=0)
    y_ref[:, 2 * GROUP_W:3 * GROUP_W] = (zc * _silu(slot("c_g").astype(F32))).astype(BF16)

    o_ref[0] = jnp.dot(y_ref[...], wout_ref[...], preferred_element_type=F32) + x_ref[0]


def _const_spec(shape):
    nd = len(shape)
    return pl.BlockSpec(shape, lambda *_: (0,) * nd)


def _layer_spec(shape, layer):
    nd = len(shape)
    return pl.BlockSpec((None,) + tuple(shape), lambda *_: (layer,) + (0,) * nd, pipeline_mode=pl.Buffered(1))


def _params(dims):
    return pltpu.CompilerParams(dimension_semantics=("arbitrary",) * dims, vmem_limit_bytes=VMEM_LIMIT)


def _mem_kv(mem, g, w_kv, kg, ones_bd):
    return pl.pallas_call(
        _mem_kv_kernel,
        grid=(DEPTH, BATCH // MEM_KV_BATCH),
        in_specs=[
            pl.BlockSpec((MEM_KV_BATCH, N_MEM, D_MODEL), lambda l, b: (b, 0, 0)),
            pl.BlockSpec((None, 1, D_MODEL), lambda l, b: (l, 0, 0)),
            pl.BlockSpec((None, D_MODEL, 2 * GROUP_W), lambda l, b: (l, 0, 0)),
            pl.BlockSpec((None, 1, GROUP_W), lambda l, b: (l, 0, 0)),
            _const_spec((GROUP_W, GROUP_W)),
        ],
        out_specs=[pl.BlockSpec((None, MEM_KV_BATCH, N_MEM, GROUP_W), lambda l, b: (l, b, 0, 0))] * 2,
        out_shape=[jax.ShapeDtypeStruct((DEPTH, BATCH, N_MEM, GROUP_W), BF16)] * 2,
        compiler_params=_params(2),
        name="mem_kv",
    )(mem, g, w_kv, kg, ones_bd)


def _in_proj(layer, x2d, g, w_in, head_gains, ones_bd):
    n_tok = x2d.shape[0]
    return pl.pallas_call(
        _in_proj_kernel,
        grid=(n_tok // TM_IN,),
        in_specs=[
            pl.BlockSpec((TM_IN, D_MODEL), lambda i: (i, 0)),
            _layer_spec((1, D_MODEL), layer),
            _layer_spec((D_MODEL, IN_COLS), layer),
            _layer_spec((8, GROUP_W), layer),
            _const_spec((GROUP_W, GROUP_W)),
        ],
        out_specs=[
            pl.BlockSpec((TM_IN, TILE_COLS), lambda i: (i, 0)),
            pl.BlockSpec((TM_IN, KV_COLS), lambda i: (i, 0)),
            pl.BlockSpec((TM_IN, GROUP_W), lambda i: (i, 0)),
        ],
        out_shape=[
            jax.ShapeDtypeStruct((n_tok, TILE_COLS), BF16),
            jax.ShapeDtypeStruct((n_tok, KV_COLS), BF16),
            jax.ShapeDtypeStruct((n_tok, GROUP_W), F32),
        ],
        compiler_params=_params(1),
        name="in_proj",
    )(x2d, g, w_in, head_gains, ones_bd)


def _fnet(layer, c_in, g1, m2, m3, wf_bd):
    x4 = c_in.reshape(BATCH, GRID_W, GRID_W, GROUP_W)
    n_half = GROUP_W // LANES
    return pl.pallas_call(
        _fnet_kernel,
        grid=(BATCH,),
        in_specs=[pl.BlockSpec((1, GRID_W, GRID_W, GROUP_W), lambda b: (b, 0, 0, 0)),
                  _const_spec(g1.shape), _const_spec(m2.shape), _const_spec(m3.shape),
                  _layer_spec((GROUP_W, GROUP_W), layer)],
        out_specs=pl.BlockSpec((1, n_half, GRID_W, GRID_W, LANES), lambda b: (b, 0, 0, 0, 0)),
        out_shape=jax.ShapeDtypeStruct((BATCH, n_half, GRID_W, GRID_W, LANES), F32),
        scratch_shapes=[pltpu.VMEM((2, GRID_W, GRID_W, GROUP_W), F32)],
        compiler_params=_params(1),
        name="fnet",
    )(x4, g1, m2, m3, wf_bd)


def _mixer(layer, x, tiles, kv, zc, km, vm, w_out, ln_g, ln_b, ws_cat, bs_tab, bias_tab):
    mem_spec = pl.BlockSpec((None, 1, N_MEM, GROUP_W), lambda b, t: (layer, b, 0, 0))
    in_specs = [
        pl.BlockSpec((1, TM, D_MODEL), lambda b, t: (b, t, 0)),
        pl.BlockSpec((1, TM, TILE_COLS), lambda b, t: (b, t, 0)),
        pl.BlockSpec((1, SEQ, KV_COLS), lambda b, t: (b, 0, 0)),
        pl.BlockSpec((1, GROUP_W // LANES, GRID_W, ROWS_PER_TILE, LANES), lambda b, t: (b, 0, 0, t, 0)),
        mem_spec, mem_spec,
        _layer_spec((D_MODEL, D_MODEL), layer),
        _layer_spec((1, GROUP_W), layer), _layer_spec((1, GROUP_W), layer),
        _layer_spec((CHUNK, N_SUB * CHUNK), layer),
        _layer_spec((CHUNK, GROUP_W), layer),
        pl.BlockSpec((2 * WIN_H - 4, N_SUB * GRID_W, 2 * LANES), lambda b, t: (0, layer, 0)),
    ]
    return pl.pallas_call(
        _mixer_kernel,
        grid=(BATCH, SEQ // TM),
        in_specs=in_specs,
        out_specs=pl.BlockSpec((1, TM, D_MODEL), lambda b, t: (b, t, 0)),
        out_shape=jax.ShapeDtypeStruct((BATCH, SEQ, D_MODEL), F32),
        scratch_shapes=[pltpu.VMEM((TM, D_MODEL), BF16), pltpu.VMEM((TM, GROUP_W), F32)],
        compiler_params=_params(2),
        name="mixer",
    )(x, tiles, kv, zc, km, vm, w_out, ln_g, ln_b, ws_cat, bs_tab, bias_tab)


def _tile_heads(g):
    return jnp.tile(g.astype(F32), (1, N_SUB))


def kernel(x, mem, norm_g, w_in, w_out, gm_ln_g, gm_ln_b, gm_w_s, gm_b_s, na_qn_g, na_kn_g, na_rpb, fn_w,
           mem_norm_g, mem_w_kv, mem_qn_g, mem_kn_g):
    g1_np, m2_np, m3_np = _dft_tables()
    g1 = jnp.asarray(g1_np).astype(BF16)
    m2 = jnp.asarray(m2_np).astype(BF16)
    m3 = jnp.asarray(m3_np).astype(BF16)
    ones_bd = jnp.asarray(_head_mean_matrix()).astype(BF16)
    qk_scale = HEAD_DIM ** -0.5 * LOG2E

    w_in_b = w_in.astype(BF16)
    w_out_b = w_out.astype(BF16)
    norm_g3 = norm_g.reshape(DEPTH, 1, D_MODEL)
    head_gains = jnp.stack([_tile_heads(na_qn_g) * qk_scale, _tile_heads(na_kn_g),
                            _tile_heads(mem_qn_g) * qk_scale], axis=1)
    head_gains = jnp.pad(head_gains, ((0, 0), (0, 5), (0, 0)))
    eye = jnp.eye(N_SUB, dtype=fn_w.dtype)
    wf_bd = (fn_w[:, :, :, None, :] * eye[None, :, None, :, None]).reshape(DEPTH, GROUP_W, GROUP_W).astype(BF16)
    ws_cat = gm_w_s.transpose(0, 2, 1, 3).reshape(DEPTH, CHUNK, N_SUB * CHUNK).astype(BF16)
    bs_tab = jnp.repeat(gm_b_s.transpose(0, 2, 1), HEAD_DIM, axis=2)
    ln_g3 = gm_ln_g.reshape(DEPTH, 1, GROUP_W)
    ln_b3 = gm_ln_b.reshape(DEPTH, 1, GROUP_W)
    bias_tab = _natten_bias_table(na_rpb * LOG2E)

    km, vm = _mem_kv(mem, mem_norm_g.reshape(DEPTH, 1, D_MODEL), mem_w_kv.astype(BF16),
                     _tile_heads(mem_kn_g).reshape(DEPTH, 1, GROUP_W), ones_bd)
    for l in range(DEPTH):
        tiles, kv, c_in = _in_proj(l, x.reshape(BATCH * SEQ, D_MODEL), norm_g3, w_in_b, head_gains, ones_bd)
        zc = _fnet(l, c_in, g1, m2, m3, wf_bd)
        x = _mixer(l, x, tiles.reshape(BATCH, SEQ, TILE_COLS), kv.reshape(BATCH, SEQ, KV_COLS), zc, km, vm,
                   w_out_b, ln_g3, ln_b3, ws_cat, bs_tab, bias_tab)
    return x
```
